```python
import jax
import jax.numpy as jnp
from jax import lax
import numpy as np

D_MODEL = 2048
BATCH = 2
SEQ = 4096
DEPTH = 1
DEC_BATCH = 32
DEC_SEQ = 8
PAST_LEN = 16384
PAGE_SIZE = 128

N_HEADS = 16
HEAD_DIM = 64
KV_HEADS = 4
Q_PER_KV = N_HEADS // KV_HEADS
ATT_WIDTH = N_HEADS * HEAD_DIM
KV_WIDTH = KV_HEADS * HEAD_DIM
CMP_BLOCK = 32
SEL_BLOCK = 64
CMP_PER_SEL = SEL_BLOCK // CMP_BLOCK
N_SEL = 16
WINDOW = 512
Q_BLOCK = 128
SCALE = HEAD_DIM ** -0.5
D_RNN = 1024
RNN_BLOCKS = 16
RNN_BLOCK_DIM = D_RNN // RNN_BLOCKS
CONV_W = 4
LRU_C = 8.0
IN_SIZES = (ATT_WIDTH, 6 * KV_WIDTH, 3 * N_HEADS, ATT_WIDTH, D_RNN, D_RNN, D_MODEL, D_MODEL)
IN_COLS = 2 * ATT_WIDTH + 6 * KV_WIDTH + 3 * N_HEADS + 2 * D_RNN + 2 * D_MODEL
EPS = 1e-6
NEG = -1e30
TINY = 1e-30

kernel_name = 'nsa_rglru_gated_hybrid_step'


def _rms(x, g):
    xf = x.astype(jnp.float32)
    y = xf * lax.rsqrt(jnp.mean(xf * xf, axis=-1, keepdims=True) + EPS)
    return (y * g.astype(jnp.float32)).astype(x.dtype)


def _masked_softmax(s, mask):
    s = jnp.where(mask, s.astype(jnp.float32), NEG)
    e = jnp.where(mask, jnp.exp(s - s.max(axis=-1, keepdims=True)), 0.0)
    return e / jnp.maximum(e.sum(axis=-1, keepdims=True), TINY)


def _project(x, norm_g, w_in, q_norm_g, k_norm_g):
    B, T, _ = x.shape
    z = jnp.einsum('btd,dc->btc', _rms(x, norm_g), w_in)
    parts, o = [], 0
    for size in IN_SIZES:
        parts.append(z[..., o:o + size])
        o += size
    q, kv, g_nsa, g_att, x_rnn, g_rnn, gm_att, gm_rnn = parts
    q = _rms(q.reshape(B, T, KV_HEADS, Q_PER_KV, HEAD_DIM), q_norm_g)
    kv = kv.reshape(B, T, 3, 2, KV_HEADS, HEAD_DIM)
    kv_cmp = kv[:, :, 0]
    kv_sel = jnp.stack([_rms(kv[:, :, 1, 0], k_norm_g[1]), kv[:, :, 1, 1]], axis=2)
    kv_win = jnp.stack([_rms(kv[:, :, 2, 0], k_norm_g[2]), kv[:, :, 2, 1]], axis=2)
    g_nsa = jax.nn.sigmoid(g_nsa.reshape(B, T, 3, KV_HEADS, Q_PER_KV, 1))
    return q, kv_cmp, kv_sel, kv_win, g_nsa, g_att, x_rnn, g_rnn, gm_att, gm_rnn


def _compress(kv_cmp, w_cmp, k_gain):
    B, L = kv_cmp.shape[:2]
    blocks = kv_cmp.reshape(B, L // CMP_BLOCK, CMP_BLOCK, 2, KV_HEADS, HEAD_DIM)
    c = jnp.einsum('bnjsgd,jsd->bnsgd', blocks, w_cmp).astype(kv_cmp.dtype)
    return _rms(c[:, :, 0], k_gain), c[:, :, 1]


def _cmp_branch(q, k_c, v_c, q_pos):
    n = jnp.arange(k_c.shape[1])
    vis = ((n + 1) * CMP_BLOCK - 1)[None, :] <= q_pos[:, None]
    s = jnp.einsum('btgrd,bngd->btgrn', q, k_c) * SCALE
    p = _masked_softmax(s, vis[None, :, None, None, :])
    o = jnp.einsum('btgrn,bngd->btgrd', p, v_c)
    imp = p.sum(axis=3)
    B, T, G, N = imp.shape
    imp = imp.reshape(B, T, G, N // CMP_PER_SEL, CMP_PER_SEL).sum(axis=-1)
    return o.astype(q.dtype), imp


def _select_blocks(imp, q_pos):
    nsb = imp.shape[-1]
    j = jnp.arange(nsb)[None, :]
    cur = (q_pos // SEL_BLOCK)[:, None]
    cand = (j < cur)[None, :, None, :]
    forced = ((j == 0) | (j == cur - 1))[None, :, None, :]
    score = jnp.where(cand, jnp.where(forced, jnp.inf, imp), -jnp.inf)
    if nsb < N_SEL - 1:
        score = jnp.pad(score, ((0, 0), (0, 0), (0, 0), (0, N_SEL - 1 - nsb)), constant_values=-jnp.inf)
    _, idx = lax.top_k(score, N_SEL - 1)
    valid = idx < cur[None, :, :, None]
    return jnp.where(valid, idx, 0), valid


def _sel_win_prompt(q, kv_sel, kv_win, sel_idx, sel_valid):
    B, T = q.shape[:2]
    nsb = T // SEL_BLOCK
    blocks = kv_sel.reshape(B, nsb, SEL_BLOCK, 2, KV_HEADS, HEAD_DIM)
    win_pad = jnp.pad(kv_win, ((0, 0), (WINDOW, 0), (0, 0), (0, 0), (0, 0)))
    b_ix = jnp.arange(B)[:, None, None, None]
    g_ix = jnp.arange(KV_HEADS)[None, None, :, None]
    offs = jnp.arange(SEL_BLOCK)
    band = jnp.arange(WINDOW + Q_BLOCK)

    def one_block(start):
        t = start + jnp.arange(Q_BLOCK)
        qb = lax.dynamic_slice_in_dim(q, start, Q_BLOCK, axis=1)
        cur = jnp.broadcast_to((t // SEL_BLOCK)[None, :, None, None], (B, Q_BLOCK, KV_HEADS, 1))
        idx = jnp.concatenate([lax.dynamic_slice_in_dim(sel_idx, start, Q_BLOCK, axis=1), cur], axis=-1)
        val = jnp.concatenate([lax.dynamic_slice_in_dim(sel_valid, start, Q_BLOCK, axis=1),
                               jnp.ones(cur.shape, bool)], axis=-1)
        kv = blocks[b_ix, idx, :, :, g_ix]
        kv = kv.reshape(B, Q_BLOCK, KV_HEADS, N_SEL * SEL_BLOCK, 2, HEAD_DIM)
        kpos = (idx[..., None] * SEL_BLOCK + offs).reshape(B, Q_BLOCK, KV_HEADS, N_SEL * SEL_BLOCK)
        smask = jnp.repeat(val, SEL_BLOCK, axis=-1) & (kpos <= t[None, :, None, None])
        s = jnp.einsum('bqgrd,bqgkd->bqgrk', qb, kv[..., 0, :]) * SCALE
        p = _masked_softmax(s, smask[:, :, :, None, :])
        o_sel = jnp.einsum('bqgrk,bqgkd->bqgrd', p, kv[..., 1, :])
        kw = lax.dynamic_slice_in_dim(win_pad, start, WINDOW + Q_BLOCK, axis=1)
        wpos = start - WINDOW + band
        wmask = (wpos[None, :] >= 0) & (wpos[None, :] <= t[:, None]) & (wpos[None, :] > t[:, None] - WINDOW)
        s = jnp.einsum('bqgrd,bkgd->bqgrk', qb, kw[:, :, 0]) * SCALE
        p = _masked_softmax(s, wmask[None, :, None, None, :])
        o_win = jnp.einsum('bqgrk,bkgd->bqgrd', p, kw[:, :, 1])
        return o_sel.astype(q.dtype), o_win.astype(q.dtype)

    o_sel, o_win = lax.map(one_block, jnp.arange(T // Q_BLOCK) * Q_BLOCK)
    return jnp.moveaxis(o_sel, 0, 1).reshape(q.shape), jnp.moveaxis(o_win, 0, 1).reshape(q.shape)


def _sel_win_sample(q, kv_sel_new, kv_win_new, sel_idx, sel_valid, cache_sel, page_table, cache_win, q_pos):
    DB, T = q.shape[:2]
    bpp = PAGE_SIZE // SEL_BLOCK
    pool = cache_sel.reshape(cache_sel.shape[0], bpp, SEL_BLOCK, 2, KV_HEADS, HEAD_DIM)
    b_ix = jnp.arange(DB)[:, None, None, None]
    g_ix = jnp.arange(KV_HEADS)[None, None, :, None]
    phys = page_table[b_ix, sel_idx // bpp]
    kv = pool[phys, sel_idx % bpp, :, :, g_ix]
    kv = kv.reshape(DB, T, KV_HEADS, (N_SEL - 1) * SEL_BLOCK, 2, HEAD_DIM)
    s_past = jnp.einsum('btgrd,btgkd->btgrk', q, kv[..., 0, :])
    s_new = jnp.einsum('btgrd,bsgd->btgrs', q, kv_sel_new[:, :, 0])
    causal = q_pos[None, :] <= q_pos[:, None]
    mask = jnp.concatenate([
        jnp.broadcast_to(jnp.repeat(sel_valid, SEL_BLOCK, axis=-1)[:, :, :, None, :], s_past.shape),
        jnp.broadcast_to(causal[None, :, None, None, :], s_new.shape)], axis=-1)
    p = _masked_softmax(jnp.concatenate([s_past, s_new], axis=-1) * SCALE, mask)
    n_past = s_past.shape[-1]
    o_sel = (jnp.einsum('btgrk,btgkd->btgrd', p[..., :n_past], kv[..., 1, :])
             + jnp.einsum('btgrs,bsgd->btgrd', p[..., n_past:], kv_sel_new[:, :, 1]))
    wb = cache_win.shape[1]
    kw = jnp.concatenate([cache_win, kv_win_new], axis=1)
    wpos = PAST_LEN - wb + jnp.arange(wb + T)
    wmask = (wpos[None, :] <= q_pos[:, None]) & (wpos[None, :] > q_pos[:, None] - WINDOW)
    s = jnp.einsum('btgrd,bkgd->btgrk', q, kw[:, :, 0]) * SCALE
    p = _masked_softmax(s, wmask[None, :, None, None, :])
    o_win = jnp.einsum('btgrk,bkgd->btgrd', p, kw[:, :, 1])
    return o_sel.astype(q.dtype), o_win.astype(q.dtype), kw[:, -wb:]


def _rglru(x_ext, h0, conv_w, conv_b, w_rg, b_rg, w_ig, b_ig, lru_lambda):
    B, L, _ = x_ext.shape
    T = L - (CONV_W - 1)
    xc = conv_b
    for k in range(CONV_W):
        xc = xc + x_ext[:, k:k + T] * conv_w[k]
    xf = xc.astype(jnp.float32)
    xb = xf.reshape(B, T, RNN_BLOCKS, RNN_BLOCK_DIM)
    r = jax.nn.sigmoid(jnp.einsum('btnd,nde->btne', xb, w_rg).reshape(B, T, D_RNN) + b_rg)
    i = jax.nn.sigmoid(jnp.einsum('btnd,nde->btne', xb, w_ig).reshape(B, T, D_RNN) + b_ig)
    log_a = -LRU_C * r * jax.nn.softplus(-lru_lambda.astype(jnp.float32))
    a = jnp.exp(log_a)
    u = jnp.sqrt(-jnp.expm1(2.0 * log_a)) * (i * xf)

    def step(h, au):
        h = au[0] * h + au[1]
        return h, h

    h_last, hs = lax.scan(step, h0.astype(jnp.float32), (jnp.swapaxes(a, 0, 1), jnp.swapaxes(u, 0, 1)))
    return jnp.swapaxes(hs, 0, 1).astype(x_ext.dtype), h_last.astype(x_ext.dtype), x_ext[:, T:]


def _merge_out(x, o_cmp, o_sel, o_win, g_nsa, g_att, h_rnn, g_rnn, gm_att, gm_rnn, w_att_out, w_rnn_out, w_out):
    B, T, _ = x.shape
    o_att = (g_nsa[:, :, 0] * o_cmp + g_nsa[:, :, 1] * o_sel + g_nsa[:, :, 2] * o_win).reshape(B, T, ATT_WIDTH)
    u_att = jnp.einsum('btc,cd->btd', o_att * jax.nn.silu(g_att), w_att_out)
    u_rnn = jnp.einsum('btc,cd->btd', h_rnn * jax.nn.silu(g_rnn), w_rnn_out)
    m = jax.nn.sigmoid(gm_att) * u_att + jax.nn.sigmoid(gm_rnn) * u_rnn
    return x + jnp.einsum('btd,de->bte', m, w_out)


def _prompt_layer(x, prm):
    (norm_g, w_in, q_norm_g, k_norm_g, w_cmp, conv_w, conv_b, w_rg, b_rg, w_ig, b_ig,
     lru_lambda, w_att_out, w_rnn_out, w_out) = prm
    B, T, _ = x.shape
    q, kv_cmp, kv_sel, kv_win, g_nsa, g_att, x_rnn, g_rnn, gm_att, gm_rnn = _project(x, norm_g, w_in, q_norm_g, k_norm_g)
    q_pos = jnp.arange(T)
    k_c, v_c = _compress(kv_cmp, w_cmp, k_norm_g[0])
    o_cmp, imp = _cmp_branch(q, k_c, v_c, q_pos)
    sel_idx, sel_valid = _select_blocks(imp, q_pos)
    o_sel, o_win = _sel_win_prompt(q, kv_sel, kv_win, sel_idx, sel_valid)
    x_ext = jnp.pad(x_rnn, ((0, 0), (CONV_W - 1, 0), (0, 0)))
    h_rnn, h_last, conv_state = _rglru(x_ext, jnp.zeros((B, D_RNN), x.dtype), conv_w, conv_b,
                                       w_rg, b_rg, w_ig, b_ig, lru_lambda)
    y = _merge_out(x, o_cmp, o_sel, o_win, g_nsa, g_att, h_rnn, g_rnn, gm_att, gm_rnn, w_att_out, w_rnn_out, w_out)
    return y, (kv_cmp, kv_sel, kv_win[:, -min(WINDOW, T):], h_last, conv_state)


def _sample_layer(x, cache_cmp, cache_sel, cache_win, state_h, state_conv, page_table, prm):
    (norm_g, w_in, q_norm_g, k_norm_g, w_cmp, conv_w, conv_b, w_rg, b_rg, w_ig, b_ig,
     lru_lambda, w_att_out, w_rnn_out, w_out) = prm
    DB, T, _ = x.shape
    q, kv_cmp, kv_sel, kv_win, g_nsa, g_att, x_rnn, g_rnn, gm_att, gm_rnn = _project(x, norm_g, w_in, q_norm_g, k_norm_g)
    q_pos = PAST_LEN + jnp.arange(T)
    past_cmp = cache_cmp[page_table].reshape(DB, -1, 2, KV_HEADS, HEAD_DIM)
    k_c, v_c = _compress(past_cmp, w_cmp, k_norm_g[0])
    o_cmp, imp = _cmp_branch(q, k_c, v_c, q_pos)
    sel_idx, sel_valid = _select_blocks(imp, q_pos)
    o_sel, o_win, win_state = _sel_win_sample(q, kv_sel, kv_win, sel_idx, sel_valid,
                                              cache_sel, page_table, cache_win, q_pos)
    x_ext = jnp.concatenate([state_conv, x_rnn], axis=1)
    h_rnn, h_last, conv_state = _rglru(x_ext, state_h, conv_w, conv_b, w_rg, b_rg, w_ig, b_ig, lru_lambda)
    y = _merge_out(x, o_cmp, o_sel, o_win, g_nsa, g_att, h_rnn, g_rnn, gm_att, gm_rnn, w_att_out, w_rnn_out, w_out)
    return y, (kv_cmp, kv_sel, win_state, h_last, conv_state)


def setup_inputs(seed: int = 0) -> dict:
    key = jax.random.key(seed)
    ks = jax.random.split(key, 24)
    f32 = jnp.float32
    n_pages = PAST_LEN // PAGE_SIZE
    n_used = DEC_BATCH * n_pages
    n_pool = (5 * n_used) // 4
    wbuf = min(WINDOW, PAST_LEN)

    def nrm(k, shape, scale):
        return scale * jax.random.normal(k, shape, f32)

    page_table = jax.random.permutation(ks[0], n_pool)[:n_used].reshape(DEC_BATCH, n_pages).astype(jnp.int32)
    a0 = jax.random.uniform(ks[1], (DEPTH, D_RNN), f32, 0.9, 0.999) ** (1.0 / LRU_C)
    lru_lambda = jnp.log(a0) - jnp.log1p(-a0)
    return {
        'x_prompt': nrm(ks[2], (BATCH, SEQ, D_MODEL), 1.0),
        'x_sample': nrm(ks[3], (DEC_BATCH, DEC_SEQ, D_MODEL), 1.0),
        'cache_cmp': nrm(ks[4], (DEPTH, n_pool, PAGE_SIZE, 2, KV_HEADS, HEAD_DIM), 1.0),
        'cache_sel': nrm(ks[5], (DEPTH, n_pool, PAGE_SIZE, 2, KV_HEADS, HEAD_DIM), 1.0),
        'cache_win': nrm(ks[6], (DEPTH, DEC_BATCH, wbuf, 2, KV_HEADS, HEAD_DIM), 1.0),
        'state_h': nrm(ks[7], (DEPTH, DEC_BATCH, D_RNN), 0.5),
        'state_conv': nrm(ks[8], (DEPTH, DEC_BATCH, CONV_W - 1, D_RNN), 1.0),
        'page_table': page_table,
        'norm_g': 1.0 + nrm(ks[9], (DEPTH, D_MODEL), 0.02),
        'w_in': nrm(ks[10], (DEPTH, D_MODEL, IN_COLS), D_MODEL ** -0.5),
        'q_norm_g': 1.0 + nrm(ks[11], (DEPTH, HEAD_DIM), 0.02),
        'k_norm_g': 1.0 + nrm(ks[12], (DEPTH, 3, HEAD_DIM), 0.02),
        'w_cmp': (1.0 + nrm(ks[13], (DEPTH, CMP_BLOCK, 2, HEAD_DIM), 0.1)) / CMP_BLOCK,
        'conv_w': nrm(ks[14], (DEPTH, CONV_W, D_RNN), CONV_W ** -0.5),
        'conv_b': nrm(ks[15], (DEPTH, D_RNN), 0.01),
        'w_rg': nrm(ks[16], (DEPTH, RNN_BLOCKS, RNN_BLOCK_DIM, RNN_BLOCK_DIM), RNN_BLOCK_DIM ** -0.5),
        'b_rg': nrm(ks[17], (DEPTH, D_RNN), 0.01),
        'w_ig': nrm(ks[18], (DEPTH, RNN_BLOCKS, RNN_BLOCK_DIM, RNN_BLOCK_DIM), RNN_BLOCK_DIM ** -0.5),
        'b_ig': nrm(ks[19], (DEPTH, D_RNN), 0.01),
        'lru_lambda': lru_lambda,
        'w_att_out': nrm(ks[20], (DEPTH, ATT_WIDTH, D_MODEL), ATT_WIDTH ** -0.5),
        'w_rnn_out': nrm(ks[21], (DEPTH, D_RNN, D_MODEL), D_RNN ** -0.5),
        'w_out': nrm(ks[22], (DEPTH, D_MODEL, D_MODEL), D_MODEL ** -0.5),
    }


def reference(x_prompt, x_sample, cache_cmp, cache_sel, cache_win, state_h, state_conv, page_table,
              norm_g, w_in, q_norm_g, k_norm_g, w_cmp, conv_w, conv_b, w_rg, b_rg, w_ig, b_ig,
              lru_lambda, w_att_out, w_rnn_out, w_out):
    yp, ys = x_prompt, x_sample
    outs_p, outs_s = [], []
    for l in range(DEPTH):
        prm = (norm_g[l], w_in[l], q_norm_g[l], k_norm_g[l], w_cmp[l], conv_w[l], conv_b[l], w_rg[l], b_rg[l],
               w_ig[l], b_ig[l], lru_lambda[l], w_att_out[l], w_rnn_out[l], w_out[l])
        yp, st_p = _prompt_layer(yp, prm)
        ys, st_s = _sample_layer(ys, cache_cmp[l], cache_sel[l], cache_win[l], state_h[l], state_conv[l],
                                 page_table, prm)
        outs_p.append(st_p)
        outs_s.append(st_s)
    cmp_p, sel_p, win_p, h_p, conv_p = [jnp.stack(a) for a in zip(*outs_p)]
    cmp_s, sel_s, win_s, h_s, conv_s = [jnp.stack(a) for a in zip(*outs_s)]
    return (yp, ys, cmp_p, sel_p, win_p, h_p, conv_p, cmp_s, sel_s, win_s, h_s, conv_s)
```

```python
import functools

import jax
import jax.numpy as jnp
from jax import lax
from jax.experimental import pallas as pl
from jax.experimental.pallas import tpu as pltpu

F32 = jnp.float32
BF16 = jnp.bfloat16

D_MODEL = 2048
N_HEADS = 16
HEAD_DIM = 64
KV_HEADS = 4
Q_PER_KV = N_HEADS // KV_HEADS
ATT_WIDTH = N_HEADS * HEAD_DIM
KV_WIDTH = KV_HEADS * HEAD_DIM
CMP_BLOCK = 32
SEL_BLOCK = 64
N_SEL = 16
WINDOW = 512
SCALE = HEAD_DIM ** -0.5
D_RNN = 1024
RNN_BLOCKS = 16
CONV_W = 4
LRU_C = 8.0
PAGE_SIZE = 128
EPS = 1e-6
NEG = -1e30
TINY = 1e-30
SEL_BIAS = -30000.0

LANES = 128
MXU_DIM = 256
VMEM_LIMIT = 56 * 1024 * 1024

C_Q = 0
C_GATT = 1024
C_XRNN = 2048
C_GRNN = 3072
C_GMATT = 4096
C_GMRNN = 6144
C_KV = 8192
C_GNSA = 9728
N_PACK = 10240

PAGES_PER_STEP = 16


def _cparams(sem):
    return pltpu.CompilerParams(dimension_semantics=sem, vmem_limit_bytes=VMEM_LIMIT)


def _proj_kernel(x_ref, g_ref, w_ref, o_ref, xn_ref):
    @pl.when(pl.program_id(1) == 0)
    def _():
        x = x_ref[...]
        ms = jnp.mean(x * x, axis=-1, keepdims=True)
        xn_ref[...] = (x * lax.rsqrt(ms + EPS) * g_ref[...]).astype(BF16)

    o_ref[...] = jnp.dot(xn_ref[...], w_ref[...], preferred_element_type=F32)


def _proj(x2d, norm_g, w_pack, tm, tn):
    m = x2d.shape[0]
    return pl.pallas_call(
        _proj_kernel,
        grid=(m // tm, N_PACK // tn),
        in_specs=[
            pl.BlockSpec((tm, D_MODEL), lambda i, j: (i, 0)),
            pl.BlockSpec((1, D_MODEL), lambda i, j: (0, 0)),
            pl.BlockSpec((D_MODEL, tn), lambda i, j: (0, j)),
        ],
        out_specs=pl.BlockSpec((tm, tn), lambda i, j: (i, j)),
        out_shape=jax.ShapeDtypeStruct((m, N_PACK), F32),
        scratch_shapes=[pltpu.VMEM((tm, D_MODEL), BF16)],
        compiler_params=_cparams(("parallel", "arbitrary")),
        name="proj",
    )(x2d, norm_g.reshape(1, D_MODEL), w_pack)


def _lane_iota(shape):
    return lax.broadcasted_iota(jnp.int32, shape, len(shape) - 1)


def _seg_rms(x, bd):
    outs = []
    for c in range(x.shape[1] // MXU_DIM):
        xc = x[:, c * MXU_DIM:(c + 1) * MXU_DIM]
        x2 = xc * xc
        hi = x2.astype(BF16)
        lo = (x2 - hi.astype(F32)).astype(BF16)
        ss = (jnp.dot(hi, bd, preferred_element_type=F32)
              + jnp.dot(lo, bd, preferred_element_type=F32))
        outs.append(xc * lax.rsqrt(ss * (1.0 / HEAD_DIM) + EPS))
    return outs[0] if len(outs) == 1 else jnp.concatenate(outs, axis=1)


def _head_lo(x, h):
    tile = x[:, (h // 2) * LANES:(h // 2 + 1) * LANES]
    if h % 2:
        tile = pltpu.roll(tile, HEAD_DIM, 1)
    return jnp.where(_lane_iota(tile.shape) < HEAD_DIM, tile, 0.0)


def _head_hi(x, h):
    tile = x[:, (h // 2) * LANES:(h // 2 + 1) * LANES]
    if h % 2 == 0:
        tile = pltpu.roll(tile, HEAD_DIM, 1)
    return jnp.where(_lane_iota(tile.shape) >= HEAD_DIM, tile, 0.0)


def _dot_nt(a, b):
    return lax.dot_general(a, b, (((1,), (1,)), ((), ())), preferred_element_type=F32)


def _pair_cols(lo_col, hi_col, shape):
    return jnp.where(_lane_iota(shape) < HEAD_DIM, lo_col, hi_col)


def _block_diag_ones():
    r = jnp.arange(MXU_DIM) // HEAD_DIM
    return (r[:, None] == r[None, :]).astype(BF16)


def _prep_prompt_kernel(zq_ref, zc_ref, zs_ref, zw_ref, qg_ref, ksg_ref, kwg_ref, wc_ref, bd_ref,
                        qa_ref, lc_ref, ls_ref, lw_ref, cc_ref,
                        kas_ref, v0s_ref, v1s_ref, kaw_ref, v0w_ref, v1w_ref, *, te):
    i = pl.program_id(1)
    bd = bd_ref[...]
    qn = _seg_rms(zq_ref[...], bd) * qg_ref[...] * SCALE
    for h in range(N_HEADS):
        qa_ref[0, h // Q_PER_KV, h % Q_PER_KV] = _head_lo(qn, h).astype(BF16)

    zc = zc_ref[...]
    lc_ref[...] = zc
    cc_ref[0] = jnp.sum(zc.reshape(te // CMP_BLOCK, CMP_BLOCK, 2 * KV_WIDTH) * wc_ref[...][None], axis=1)

    lane = _lane_iota((te, LANES))
    row = lax.broadcasted_iota(jnp.int32, (te, LANES), 0)
    own_block = (i * te + row) // SEL_BLOCK
    onehot = jnp.where(lane - HEAD_DIM == own_block, 1.0, 0.0)

    for z_ref, g_ref, l_ref, ka_ref, v0_ref, v1_ref, with_onehot in (
            (zs_ref, ksg_ref, ls_ref, kas_ref, v0s_ref, v1s_ref, True),
            (zw_ref, kwg_ref, lw_ref, kaw_ref, v0w_ref, v1w_ref, False)):
        z = z_ref[...]
        kn = _seg_rms(z[:, :KV_WIDTH], bd) * g_ref[...]
        v = z[:, KV_WIDTH:]
        l_ref[:, :KV_WIDTH] = kn
        l_ref[:, KV_WIDTH:] = v
        for g in range(KV_HEADS):
            ka = _head_lo(kn, g)
            if with_onehot:
                ka = jnp.where(lane < HEAD_DIM, ka, onehot)
            ka_ref[0, g] = ka.astype(BF16)
            v0_ref[0, g] = _head_lo(v, g).astype(BF16)
            v1_ref[0, g] = _head_hi(v, g).astype(BF16)


def _prep_prompt(z, b, t, q_gain, ks_gain, kw_gain, wc, bd, te=256):
    nt = t // te
    m = b * t

    def zspec(width, col):
        return pl.BlockSpec((te, width), lambda bi, i, c=col // width: (bi * nt + i, c))

    def const(shape):
        return pl.BlockSpec(shape, lambda bi, i: tuple(0 for _ in shape))

    row512 = pl.BlockSpec((te, 2 * KV_WIDTH), lambda bi, i: (bi * nt + i, 0))
    kv_spec = pl.BlockSpec((1, KV_HEADS, te, LANES), lambda bi, i: (bi, 0, i, 0))
    kv_shape = jax.ShapeDtypeStruct((b, KV_HEADS, t, LANES), BF16)
    return pl.pallas_call(
        functools.partial(_prep_prompt_kernel, te=te),
        grid=(b, nt),
        in_specs=[zspec(ATT_WIDTH, C_Q), zspec(512, C_KV), zspec(512, C_KV + 512), zspec(512, C_KV + 1024),
                  const((1, ATT_WIDTH)), const((1, KV_WIDTH)), const((1, KV_WIDTH)),
                  const((CMP_BLOCK, 2 * KV_WIDTH)), const((MXU_DIM, MXU_DIM))],
        out_specs=[pl.BlockSpec((1, KV_HEADS, Q_PER_KV, te, LANES), lambda bi, i: (bi, 0, 0, i, 0)),
                   row512, row512, row512,
                   pl.BlockSpec((1, te // CMP_BLOCK, 2 * KV_WIDTH), lambda bi, i: (bi, i, 0)),
                   kv_spec, kv_spec, kv_spec, kv_spec, kv_spec, kv_spec],
        out_shape=[jax.ShapeDtypeStruct((b, KV_HEADS, Q_PER_KV, t, LANES), BF16),
                   jax.ShapeDtypeStruct((m, 2 * KV_WIDTH), F32),
                   jax.ShapeDtypeStruct((m, 2 * KV_WIDTH), F32),
                   jax.ShapeDtypeStruct((m, 2 * KV_WIDTH), F32),
                   jax.ShapeDtypeStruct((b, t // CMP_BLOCK, 2 * KV_WIDTH), F32),
                   kv_shape, kv_shape, kv_shape, kv_shape, kv_shape, kv_shape],
        compiler_params=_cparams(("parallel", "parallel")),
        name="prep_prompt",
    )(z, z, z, z, q_gain, ks_gain, kw_gain, wc, bd)


def _cmp_prep_kernel(cc_ref, kg_ref, bd_ref, kca_ref, v0c_ref, v1c_ref):
    cc = cc_ref[0]
    kc = _seg_rms(cc[:, :KV_WIDTH], bd_ref[...]) * kg_ref[...]
    vc = cc[:, KV_WIDTH:]
    for g in range(KV_HEADS):
        kca_ref[0, g] = _head_lo(kc, g).astype(BF16)
        v0c_ref[0, g] = _head_lo(vc, g).astype(BF16)
        v1c_ref[0, g] = _head_hi(vc, g).astype(BF16)


def _cmp_prep(cc_perm, kc_gain, bd):
    b, n, _ = cc_perm.shape
    spec = pl.BlockSpec((1, KV_HEADS, n, LANES), lambda bi: (bi, 0, 0, 0))
    shape = jax.ShapeDtypeStruct((b, KV_HEADS, n, LANES), BF16)
    return pl.pallas_call(
        _cmp_prep_kernel,
        grid=(b,),
        in_specs=[pl.BlockSpec((1, n, 2 * KV_WIDTH), lambda bi: (bi, 0, 0)),
                  pl.BlockSpec((1, KV_WIDTH), lambda bi: (0, 0)),
                  pl.BlockSpec((MXU_DIM, MXU_DIM), lambda bi: (0, 0))],
        out_specs=[spec, spec, spec],
        out_shape=[shape, shape, shape],
        compiler_params=_cparams(("parallel",)),
        name="cmp_prep",
    )(cc_perm, kc_gain, bd)


def _rank_select(score, n_keep):
    n, w = score.shape
    sub = lax.broadcasted_iota(jnp.int32, (8, w), 0)
    groups = [score[8 * k:8 * k + 8] for k in range(n // 8)]
    counts = [jnp.zeros((8, w), jnp.int32) for _ in groups]
    for i in range(n):
        si = jnp.broadcast_to(score[i:i + 1, :], (8, w))
        for k, blk in enumerate(groups):
            if 8 * k + 7 < i:
                inc = jnp.where(si > blk, 1, 0)
            elif 8 * k > i:
                inc = jnp.where(si >= blk, 1, 0)
            else:
                inc = jnp.where(sub > (i - 8 * k), jnp.where(si >= blk, 1, 0), jnp.where(si > blk, 1, 0))
            counts[k] = counts[k] + inc
    return jnp.concatenate(counts, axis=0) < n_keep


def _attn_prompt_kernel(qa_ref, kca_ref, v0c_ref, v1c_ref, kas_ref, v0s_ref, v1s_ref,
                        kaw_ref, v0w_ref, v1w_ref, gl_ref, o_ref, *, tq):
    i = pl.program_id(2)
    rows = Q_PER_KV * tq
    t0 = i * tq
    qa = qa_ref[0, 0].reshape(rows, LANES)
    kca = kca_ref[0, 0]
    n_cmp = kca.shape[0]
    n_selb = n_cmp // 2
    lane_pair = _lane_iota((tq, LANES))

    def pair_scale(col, h0):
        return _pair_cols(col[h0 * tq:(h0 + 1) * tq], col[(h0 + 1) * tq:(h0 + 2) * tq], (tq, LANES))

    def pv(p, v0, v1, h0):
        return (jnp.dot(p[h0 * tq:(h0 + 1) * tq], v0, preferred_element_type=F32)
                + jnp.dot(p[(h0 + 1) * tq:(h0 + 2) * tq], v1, preferred_element_type=F32))

    s_t = _dot_nt(kca, qa)
    n_row = lax.broadcasted_iota(jnp.int32, (n_cmp, tq), 0)
    t_col = t0 + _lane_iota((n_cmp, tq))
    cmp_id = 2 * (n_row % n_selb) + n_row // n_selb
    vis_t = (cmp_id + 1) * CMP_BLOCK - 1 <= t_col
    vis_t = jnp.concatenate([vis_t] * Q_PER_KV, axis=1)
    s_t = jnp.where(vis_t, s_t, NEG)
    e_t = jnp.where(vis_t, jnp.exp(s_t - jnp.max(s_t, axis=0, keepdims=True)), 0.0)
    p_t = e_t / jnp.maximum(jnp.sum(e_t, axis=0, keepdims=True), TINY)
    imp = jnp.zeros((n_selb, tq), F32)
    for r in range(Q_PER_KV):
        imp = imp + (p_t[:n_selb, r * tq:(r + 1) * tq] + p_t[n_selb:, r * tq:(r + 1) * tq])

    j_row = lax.broadcasted_iota(jnp.int32, (n_selb, tq), 0)
    cur = (t0 + _lane_iota((n_selb, tq))) // SEL_BLOCK
    cand = j_row < cur
    forced = (j_row == 0) | (j_row == cur - 1)
    score = jnp.where(cand, jnp.where(forced, jnp.inf, imp), -jnp.inf)
    keep = (_rank_select(score, N_SEL - 1) & cand) | (j_row == cur)
    bias_t = jnp.where(keep, 0.0, SEL_BIAS)
    if n_selb < HEAD_DIM:
        bias_t = jnp.concatenate([bias_t, jnp.full((HEAD_DIM - n_selb, tq), SEL_BIAS, F32)], axis=0)
    bias = jnp.concatenate([jnp.zeros((HEAD_DIM, tq), F32), bias_t], axis=0).T
    bias4 = jnp.concatenate([bias] * Q_PER_KV, axis=0).astype(BF16)
    q_sel = jnp.where(_lane_iota((rows, LANES)) >= HEAD_DIM, bias4, qa)

    rr = lax.broadcasted_iota(jnp.int32, (rows, tq), 0) % tq
    kk = _lane_iota((rows, tq))

    s = _dot_nt(qa, kca)
    n_lane = _lane_iota((rows, n_cmp))
    cmp_id = 2 * (n_lane % n_selb) + n_lane // n_selb
    vis = (cmp_id + 1) * CMP_BLOCK - 1 <= t0 + lax.broadcasted_iota(jnp.int32, (rows, n_cmp), 0) % tq
    s = jnp.where(vis, s, NEG)
    e = jnp.where(vis, jnp.exp(s - jnp.max(s, axis=1, keepdims=True)), 0.0)
    p = (e / jnp.maximum(jnp.sum(e, axis=1, keepdims=True), TINY)).astype(BF16)
    o_cmp = [pv(p, v0c_ref[0, 0], v1c_ref[0, 0], h0) for h0 in (0, 2)]

    def online_step(carry, s, v0, v1):
        m, l, a01, a23 = carry
        m_new = jnp.maximum(m, jnp.max(s, axis=1, keepdims=True))
        alpha = jnp.exp(m - m_new)
        pe = jnp.exp(s - m_new)
        l = alpha * l + jnp.sum(pe, axis=1, keepdims=True)
        pb = pe.astype(BF16)
        a01 = a01 * pair_scale(alpha, 0) + pv(pb, v0, v1, 0)
        a23 = a23 * pair_scale(alpha, 2) + pv(pb, v0, v1, 2)
        return m_new, l, a01, a23

    def finish(carry):
        _, l, a01, a23 = carry
        inv = 1.0 / l
        return [a01 * pair_scale(inv, 0), a23 * pair_scale(inv, 2)]

    init = (jnp.full((rows, 1), NEG, F32), jnp.zeros((rows, 1), F32),
            jnp.zeros((tq, LANES), F32), jnp.zeros((tq, LANES), F32))

    def chunk(ref, c):
        return ref[0, 0, pl.ds(pl.multiple_of(c * tq, tq), tq), :]

    def sel_body(c, carry):
        return online_step(carry, _dot_nt(q_sel, chunk(kas_ref, c)), chunk(v0s_ref, c), chunk(v1s_ref, c))

    carry = lax.fori_loop(0, i, sel_body, init)
    s = jnp.where(kk <= rr, _dot_nt(q_sel, chunk(kas_ref, i)), NEG)
    o_sel = finish(online_step(carry, s, chunk(v0s_ref, i), chunk(v1s_ref, i)))

    s = jnp.where(kk <= rr, _dot_nt(qa, chunk(kaw_ref, i)), NEG)
    carry = online_step(init, s, chunk(v0w_ref, i), chunk(v1w_ref, i))
    for back in range(1, WINDOW // tq + 1):
        c = jnp.maximum(i - back, 0)
        ok = i >= back
        if back == WINDOW // tq:
            ok = ok & (kk > rr)
        s = jnp.where(ok, _dot_nt(qa, chunk(kaw_ref, c)), NEG)
        carry = online_step(carry, s, chunk(v0w_ref, c), chunk(v1w_ref, c))
    o_win = finish(carry)

    sig = jax.nn.sigmoid(gl_ref[...])

    def gate(branch, h0):
        c0 = branch * Q_PER_KV + h0
        return jnp.where(lane_pair < HEAD_DIM, sig[:, c0:c0 + 1], sig[:, c0 + 1:c0 + 2])

    outs = []
    for idx, h0 in enumerate((0, 2)):
        outs.append(gate(0, h0) * o_cmp[idx] + gate(1, h0) * o_sel[idx] + gate(2, h0) * o_win[idx])
    o_ref[...] = jnp.concatenate(outs, axis=1)


def _attn_prompt(z, qa, kca, v0c, v1c, kas, v0s, v1s, kaw, v0w, v1w, b, t, tq=256):
    nt = t // tq
    n_cmp = kca.shape[2]
    qspec = pl.BlockSpec((1, 1, Q_PER_KV, tq, LANES), lambda bi, g, i: (bi, g, 0, i, 0))
    cspec = pl.BlockSpec((1, 1, n_cmp, LANES), lambda bi, g, i: (bi, g, 0, 0))
    kspec = pl.BlockSpec((1, 1, t, LANES), lambda bi, g, i: (bi, g, 0, 0))
    gspec = pl.BlockSpec((tq, LANES), lambda bi, g, i: (bi * nt + i, C_GNSA // LANES + g))
    return pl.pallas_call(
        functools.partial(_attn_prompt_kernel, tq=tq),
        grid=(b, KV_HEADS, nt),
        in_specs=[qspec, cspec, cspec, cspec, kspec, kspec, kspec, kspec, kspec, kspec, gspec],
        out_specs=pl.BlockSpec((tq, Q_PER_KV * HEAD_DIM), lambda bi, g, i: (bi * nt + i, g)),
        out_shape=jax.ShapeDtypeStruct((b * t, ATT_WIDTH), F32),
        compiler_params=_cparams(("parallel", "parallel", "arbitrary")),
        name="attn_prompt",
    )(qa, kca, v0c, v1c, kas, v0s, v1s, kaw, v0w, v1w, z)


def _rglru_kernel(x_ref, hist_ref, h0_ref, cw_ref, cb_ref, wr_ref, br_ref, wi_ref, bi_ref, lam_ref,
                  h_ref, hl_ref, cs_ref, carry_ref, tail_ref, a_ref, u_ref, *, tt):
    i = pl.program_id(1)

    @pl.when(i == 0)
    def _():
        carry_ref[...] = h0_ref[0]
        tail_ref[...] = hist_ref[0]

    x = x_ref[...]
    xe = jnp.concatenate([tail_ref[...], x], axis=0)
    cw = cw_ref[...]
    xc = cb_ref[...]
    for k in range(CONV_W):
        lo = 8 - (CONV_W - 1) + k
        xc = xc + xe[lo:lo + tt] * cw[k:k + 1]
    xb = xc.astype(BF16)

    def gates(w_ref, b_ref):
        parts = [jnp.dot(xb[:, c * MXU_DIM:(c + 1) * MXU_DIM], w_ref[c], preferred_element_type=F32)
                 for c in range(D_RNN // MXU_DIM)]
        return jax.nn.sigmoid(jnp.concatenate(parts, axis=1) + b_ref[...])

    r = gates(wr_ref, br_ref)
    ig = gates(wi_ref, bi_ref)
    nl = -lam_ref[...]
    softplus = jnp.maximum(nl, 0.0) + jnp.log1p(jnp.exp(-jnp.abs(nl)))
    log_a = -LRU_C * r * softplus
    a_ref[...] = jnp.exp(log_a)
    th = jnp.tanh(log_a)
    u_ref[...] = jnp.sqrt(-2.0 * th / (1.0 - th)) * (ig * xc)

    def step(t, h):
        h = a_ref[pl.ds(t, 1), :] * h + u_ref[pl.ds(t, 1), :]
        h_ref[pl.ds(t, 1), :] = h
        return h

    h = lax.fori_loop(0, tt, step, carry_ref[...], unroll=8)
    carry_ref[...] = h
    tail_ref[...] = x[tt - 8:tt]

    @pl.when(i == pl.num_programs(1) - 1)
    def _():
        hl_ref[0] = h
        cs_ref[0] = x[tt - 8:tt]


def _rglru(z, b, t, hist8, h0, conv_w, conv_b, wr_bd, b_rg, wi_bd, b_ig, lam, tt):
    nt = t // tt

    def const(shape):
        return pl.BlockSpec(shape, lambda bi, i: tuple(0 for _ in shape))

    row = lambda v: v.reshape(1, D_RNN)
    return pl.pallas_call(
        functools.partial(_rglru_kernel, tt=tt),
        grid=(b, nt),
        in_specs=[pl.BlockSpec((tt, D_RNN), lambda bi, i: (bi * nt + i, C_XRNN // D_RNN)),
                  pl.BlockSpec((1, 8, D_RNN), lambda bi, i: (bi, 0, 0)),
                  pl.BlockSpec((1, 1, D_RNN), lambda bi, i: (bi, 0, 0)),
                  const((CONV_W, D_RNN)), const((1, D_RNN)),
                  const((D_RNN // MXU_DIM, MXU_DIM, MXU_DIM)), const((1, D_RNN)),
                  const((D_RNN // MXU_DIM, MXU_DIM, MXU_DIM)), const((1, D_RNN)),
                  const((1, D_RNN))],
        out_specs=[pl.BlockSpec((tt, D_RNN), lambda bi, i: (bi * nt + i, 0)),
                   pl.BlockSpec((1, 1, D_RNN), lambda bi, i: (bi, 0, 0)),
                   pl.BlockSpec((1, 8, D_RNN), lambda bi, i: (bi, 0, 0))],
        out_shape=[jax.ShapeDtypeStruct((b * t, D_RNN), F32),
                   jax.ShapeDtypeStruct((b, 1, D_RNN), F32),
                   jax.ShapeDtypeStruct((b, 8, D_RNN), F32)],
        scratch_shapes=[pltpu.VMEM((1, D_RNN), F32), pltpu.VMEM((8, D_RNN), F32),
                        pltpu.VMEM((tt, D_RNN), F32), pltpu.VMEM((tt, D_RNN), F32)],
        compiler_params=_cparams(("parallel", "arbitrary")),
        name="rglru",
    )(z, hist8, h0.reshape(b, 1, D_RNN), conv_w, row(conv_b), wr_bd, row(b_rg), wi_bd, row(b_ig), row(lam))


def _out_kernel(x_ref, oatt_ref, gatt_ref, h_ref, grnn_ref, gma_ref, gmr_ref, wa_ref, wr_ref, wo_ref, y_ref):
    a = (oatt_ref[...] * jax.nn.silu(gatt_ref[...])).astype(BF16)
    u_att = jnp.dot(a, wa_ref[...], preferred_element_type=F32)
    r = (h_ref[...] * jax.nn.silu(grnn_ref[...])).astype(BF16)
    u_rnn = jnp.dot(r, wr_ref[...], preferred_element_type=F32)
    m = jax.nn.sigmoid(gma_ref[...]) * u_att + jax.nn.sigmoid(gmr_ref[...]) * u_rnn
    y_ref[...] = x_ref[...] + jnp.dot(m.astype(BF16), wo_ref[...], preferred_element_type=F32)


def _out_proj(x2d, z, o_att, h_rnn, wa, wr, wo, tm):
    m = x2d.shape[0]

    def zspec(width, col):
        return pl.BlockSpec((tm, width), lambda i, c=col // width: (i, c))

    def wspec(shape):
        return pl.BlockSpec(shape, lambda i: (0, 0), pipeline_mode=pl.Buffered(1))

    rows = lambda width: pl.BlockSpec((tm, width), lambda i: (i, 0))
    return pl.pallas_call(
        _out_kernel,
        grid=(m // tm,),
        in_specs=[rows(D_MODEL), rows(ATT_WIDTH), zspec(ATT_WIDTH, C_GATT), rows(D_RNN), zspec(D_RNN, C_GRNN),
                  zspec(D_MODEL, C_GMATT), zspec(D_MODEL, C_GMRNN),
                  wspec((ATT_WIDTH, D_MODEL)), wspec((D_RNN, D_MODEL)), wspec((D_MODEL, D_MODEL))],
        out_specs=rows(D_MODEL),
        out_shape=jax.ShapeDtypeStruct((m, D_MODEL), F32),
        compiler_params=_cparams(("parallel",)),
        name="out_proj",
    )(x2d, o_att, z, h_rnn, z, z, z, wa, wr, wo)


def _prep_sample_kernel(zq_ref, zc_ref, zs_ref, zw_ref, qg_ref, ksg_ref, kwg_ref, bd_ref,
                        qn_ref, lc_ref, ls_ref, lw_ref):
    bd = bd_ref[...]
    qn_ref[...] = _seg_rms(zq_ref[...], bd) * qg_ref[...] * SCALE
    lc_ref[...] = zc_ref[...]
    for z_ref, g_ref, l_ref in ((zs_ref, ksg_ref, ls_ref), (zw_ref, kwg_ref, lw_ref)):
        z = z_ref[...]
        l_ref[:, :KV_WIDTH] = _seg_rms(z[:, :KV_WIDTH], bd) * g_ref[...]
        l_ref[:, KV_WIDTH:] = z[:, KV_WIDTH:]


def _prep_sample(z, q_gain, ks_gain, kw_gain, bd):
    m = z.shape[0]

    def zspec(width, col):
        return pl.BlockSpec((m, width), lambda i, c=col // width: (0, c))

    def const(shape):
        return pl.BlockSpec(shape, lambda i: tuple(0 for _ in shape))

    full = lambda width: pl.BlockSpec((m, width), lambda i: (0, 0))
    leaf = jax.ShapeDtypeStruct((m, 2 * KV_WIDTH), F32)
    return pl.pallas_call(
        _prep_sample_kernel,
        grid=(1,),
        in_specs=[zspec(ATT_WIDTH, C_Q), zspec(512, C_KV), zspec(512, C_KV + 512), zspec(512, C_KV + 1024),
                  const((1, ATT_WIDTH)), const((1, KV_WIDTH)), const((1, KV_WIDTH)), const((MXU_DIM, MXU_DIM))],
        out_specs=[full(ATT_WIDTH), full(512), full(512), full(512)],
        out_shape=[jax.ShapeDtypeStruct((m, ATT_WIDTH), F32), leaf, leaf, leaf],
        compiler_params=_cparams(("arbitrary",)),
        name="prep_sample",
    )(z, z, z, z, q_gain, ks_gain, kw_gain, bd)


def _compress_pages_kernel(pt_ref, *refs):
    del pt_ref
    pages, wc_ref, o_ref = refs[:PAGES_PER_STEP], refs[PAGES_PER_STEP], refs[PAGES_PER_STEP + 1]
    wc = wc_ref[...][None]
    per_page = PAGE_SIZE // CMP_BLOCK
    for k, pg in enumerate(pages):
        blk = pg[0].reshape(per_page, CMP_BLOCK, 2 * KV_WIDTH)
        o_ref[0, k * per_page:(k + 1) * per_page, :] = jnp.sum(blk * wc, axis=1)


def _page_specs(n):
    return [pl.BlockSpec((1, PAGE_SIZE, 2 * KV_WIDTH),
                         lambda bi, s, pt, k=k: (pt[bi, s * PAGES_PER_STEP + k], 0, 0)) for k in range(n)]


def _compress_pages(page_table, pool, wc):
    b, n_pages = page_table.shape
    ns = n_pages // PAGES_PER_STEP
    per_step = PAGES_PER_STEP * PAGE_SIZE // CMP_BLOCK
    grid_spec = pltpu.PrefetchScalarGridSpec(
        num_scalar_prefetch=1,
        grid=(b, ns),
        in_specs=_page_specs(PAGES_PER_STEP) + [pl.BlockSpec((CMP_BLOCK, 2 * KV_WIDTH), lambda bi, s, pt: (0, 0))],
        out_specs=pl.BlockSpec((1, per_step, 2 * KV_WIDTH), lambda bi, s, pt: (bi, s, 0)),
    )
    return pl.pallas_call(
        _compress_pages_kernel,
        grid_spec=grid_spec,
        out_shape=jax.ShapeDtypeStruct((b, ns * per_step, 2 * KV_WIDTH), F32),
        compiler_params=_cparams(("parallel", "arbitrary")),
        name="compress_pages",
    )(page_table, *([pool] * PAGES_PER_STEP), wc)


def _softmax_lanes(parts, masks):
    parts = [jnp.where(mk, s, NEG) for s, mk in zip(parts, masks)]
    mx = functools.reduce(jnp.maximum, [jnp.max(s, axis=1, keepdims=True) for s in parts])
    es = [jnp.where(mk, jnp.exp(s - mx), 0.0) for s, mk in zip(parts, masks)]
    den = jnp.maximum(sum(jnp.sum(e, axis=1, keepdims=True) for e in es), TINY)
    return [e / den for e in es]


def _sample_cmp_win_kernel(qbd_ref, cc_ref, kg_ref, bd_ref, cwin_ref, wnew_ref,
                           ocmp_ref, owin_ref, bias_ref, wst_ref, *, past_len, t_new):
    qbd = qbd_ref[0]
    rows = qbd.shape[0]
    cc = cc_ref[0]
    n_cmp = cc.shape[0]
    kc = (_seg_rms(cc[:, :KV_WIDTH], bd_ref[...]) * kg_ref[...]).astype(BF16)
    vc = cc[:, KV_WIDTH:].astype(BF16)

    t_row = lax.broadcasted_iota(jnp.int32, (rows, n_cmp), 0) % t_new
    n_lane = _lane_iota((rows, n_cmp))
    vis = (n_lane + 1) * CMP_BLOCK - 1 <= past_len + t_row
    (p,) = _softmax_lanes([_dot_nt(qbd, kc)], [vis])
    ocmp_ref[0] = jnp.dot(p.astype(BF16), vc, preferred_element_type=F32)

    gt = KV_HEADS * t_new
    psum = p[0:gt]
    for r in range(1, Q_PER_KV):
        psum = psum + p[r * gt:(r + 1) * gt]
    imp = psum + pltpu.roll(psum, n_cmp - 1, 1)
    lane = _lane_iota((gt, n_cmp))
    j = lane // 2
    cur = (past_len + lax.broadcasted_iota(jnp.int32, (gt, n_cmp), 0) % t_new) // SEL_BLOCK
    cand = (lane % 2 == 0) & (j < cur)
    forced = (j == 0) | (j == cur - 1)
    score = jnp.where(cand, jnp.where(forced, jnp.inf, imp), -jnp.inf)
    count = jnp.zeros((gt, n_cmp), jnp.int32)
    for c in range(0, n_cmp, 2):
        col = score[:, c:c + 1]
        count = count + jnp.where(lane > c, jnp.where(col >= score, 1, 0), jnp.where(col > score, 1, 0))
    keep = (count < N_SEL - 1) & cand
    bias_ref[0] = jnp.where(keep, 0.0, SEL_BIAS)

    cwin = cwin_ref[0]
    wb = cwin.shape[0]
    wnew = wnew_ref[0]
    pad = jnp.zeros((LANES - t_new, 2 * KV_WIDTH), F32)
    wnew_p = jnp.concatenate([wnew, pad], axis=0)
    t_w = lax.broadcasted_iota(jnp.int32, (rows, wb), 0) % t_new
    idx = _lane_iota((rows, wb))
    ok_w = (idx <= wb + t_w) & (idx > wb + t_w - WINDOW)
    t_n = lax.broadcasted_iota(jnp.int32, (rows, LANES), 0) % t_new
    idx_n = wb + _lane_iota((rows, LANES))
    ok_n = (idx_n <= wb + t_n) & (idx_n > wb + t_n - WINDOW) & (_lane_iota((rows, LANES)) < t_new)
    p_w, p_n = _softmax_lanes([_dot_nt(qbd, cwin[:, :KV_WIDTH].astype(BF16)),
                               _dot_nt(qbd, wnew_p[:, :KV_WIDTH].astype(BF16))], [ok_w, ok_n])
    owin_ref[0] = (jnp.dot(p_w.astype(BF16), cwin[:, KV_WIDTH:].astype(BF16), preferred_element_type=F32)
                   + jnp.dot(p_n.astype(BF16), wnew_p[:, KV_WIDTH:].astype(BF16), preferred_element_type=F32))
    wst_ref[0, 0:wb - t_new, :] = cwin[t_new:wb]
    wst_ref[0, wb - t_new:wb, :] = wnew


def _sample_cmp_win(qbd, cc, kc_gain, bd, cwin, wnew, past_len):
    b, rows, _ = qbd.shape
    n_cmp = cc.shape[1]
    wb = cwin.shape[1]
    t_new = wnew.shape[1]
    gt = KV_HEADS * t_new

    def per_b(shape):
        return pl.BlockSpec((1,) + shape, lambda bi: (bi, 0, 0))

    return pl.pallas_call(
        functools.partial(_sample_cmp_win_kernel, past_len=past_len, t_new=t_new),
        grid=(b,),
        in_specs=[per_b((rows, MXU_DIM)), per_b((n_cmp, 2 * KV_WIDTH)),
                  pl.BlockSpec((1, KV_WIDTH), lambda bi: (0, 0)),
                  pl.BlockSpec((MXU_DIM, MXU_DIM), lambda bi: (0, 0)),
                  per_b((wb, 2 * KV_WIDTH)), per_b((t_new, 2 * KV_WIDTH))],
        out_specs=[per_b((rows, MXU_DIM)), per_b((rows, MXU_DIM)), per_b((gt, n_cmp)), per_b((wb, 2 * KV_WIDTH))],
        out_shape=[jax.ShapeDtypeStruct((b, rows, MXU_DIM), F32), jax.ShapeDtypeStruct((b, rows, MXU_DIM), F32),
                   jax.ShapeDtypeStruct((b, gt, n_cmp), F32), jax.ShapeDtypeStruct((b, wb, 2 * KV_WIDTH), F32)],
        compiler_params=_cparams(("parallel",)),
        name="sample_cmp_win",
    )(qbd, cc, kc_gain, bd, cwin, wnew)


def _sample_sel_kernel(pt_ref, *refs, t_new):
    del pt_ref
    pages = refs[:PAGES_PER_STEP]
    (qbd_ref, bsel_ref, onehot_ref, snew_ref, ocmp_ref, owin_ref, gl_ref,
     o_ref, m_ref, l_ref, acc_ref) = refs[PAGES_PER_STEP:]
    s_idx = pl.program_id(1)
    qbd = qbd_ref[0]
    rows = qbd.shape[0]

    @pl.when(s_idx == 0)
    def _():
        m_ref[...] = jnp.full(m_ref.shape, NEG, F32)
        l_ref[...] = jnp.zeros(l_ref.shape, F32)
        acc_ref[...] = jnp.zeros(acc_ref.shape, F32)

    def update(s, v):
        m = m_ref[...]
        m_new = jnp.maximum(m, jnp.max(s, axis=1, keepdims=True))
        alpha = jnp.exp(m - m_new)
        pe = jnp.exp(s - m_new)
        l_ref[...] = alpha * l_ref[...] + jnp.sum(pe, axis=1, keepdims=True)
        acc_ref[...] = alpha * acc_ref[...] + jnp.dot(pe.astype(BF16), v, preferred_element_type=F32)
        m_ref[...] = m_new

    k = jnp.concatenate([pg[0][:, :KV_WIDTH].astype(BF16) for pg in pages], axis=0)
    v = jnp.concatenate([pg[0][:, KV_WIDTH:].astype(BF16) for pg in pages], axis=0)
    update(_dot_nt(qbd, k) + jnp.dot(bsel_ref[0, 0], onehot_ref[...], preferred_element_type=F32), v)

    @pl.when(s_idx == pl.num_programs(1) - 1)
    def _():
        snew = jnp.concatenate([snew_ref[0], jnp.zeros((LANES - t_new, 2 * KV_WIDTH), F32)], axis=0)
        t_q = lax.broadcasted_iota(jnp.int32, (rows, LANES), 0) % t_new
        t_k = _lane_iota((rows, LANES))
        s = jnp.where((t_k <= t_q) & (t_k < t_new), _dot_nt(qbd, snew[:, :KV_WIDTH].astype(BF16)), NEG)
        update(s, snew[:, KV_WIDTH:].astype(BF16))
        o_sel = acc_ref[...] / l_ref[...]
        sig = jax.nn.sigmoid(gl_ref[0])
        o_ref[0] = sig[:, 0:1] * ocmp_ref[0] + sig[:, 1:2] * o_sel + sig[:, 2:3] * owin_ref[0]


def _sample_sel(page_table, pool, qbd, bsel, onehot, snew, ocmp, owin, gl):
    b, n_pages = page_table.shape
    ns = n_pages // PAGES_PER_STEP
    rows = qbd.shape[1]
    t_new = snew.shape[1]
    keys = PAGES_PER_STEP * PAGE_SIZE

    def per_b(shape):
        return pl.BlockSpec((1,) + shape, lambda bi, s, pt: (bi,) + tuple(0 for _ in shape))

    grid_spec = pltpu.PrefetchScalarGridSpec(
        num_scalar_prefetch=1,
        grid=(b, ns),
        in_specs=_page_specs(PAGES_PER_STEP) + [
            per_b((rows, MXU_DIM)),
            pl.BlockSpec((1, 1, rows, LANES), lambda bi, s, pt: (bi, s, 0, 0)),
            pl.BlockSpec((LANES, keys), lambda bi, s, pt: (0, 0)),
            per_b((t_new, 2 * KV_WIDTH)), per_b((rows, MXU_DIM)), per_b((rows, MXU_DIM)), per_b((rows, LANES))],
        out_specs=per_b((rows, MXU_DIM)),
        scratch_shapes=[pltpu.VMEM((rows, 1), F32), pltpu.VMEM((rows, 1), F32), pltpu.VMEM((rows, MXU_DIM), F32)],
    )
    return pl.pallas_call(
        functools.partial(_sample_sel_kernel, t_new=t_new),
        grid_spec=grid_spec,
        out_shape=jax.ShapeDtypeStruct((b, rows, MXU_DIM), F32),
        compiler_params=_cparams(("parallel", "arbitrary")),
        name="sample_sel",
    )(page_table, *([pool] * PAGES_PER_STEP), qbd, bsel, onehot, snew, ocmp, owin, gl)


def _pack_weights(w_in, q_norm_g, k_norm_g, w_cmp, w_rg, w_ig, w_att_out, w_rnn_out, w_out):
    o = 0
    parts = []
    for size in (ATT_WIDTH, 6 * KV_WIDTH, 3 * N_HEADS, ATT_WIDTH, D_RNN, D_RNN, D_MODEL, D_MODEL):
        parts.append(w_in[:, o:o + size])
        o += size
    q, kv, g_nsa, g_att, x_rnn, g_rnn, gm_att, gm_rnn = parts
    g_nsa = g_nsa.reshape(D_MODEL, 3, KV_HEADS, Q_PER_KV).transpose(0, 2, 1, 3).reshape(D_MODEL, KV_HEADS, 3 * Q_PER_KV)
    g_nsa = jnp.pad(g_nsa, ((0, 0), (0, 0), (0, LANES - 3 * Q_PER_KV))).reshape(D_MODEL, KV_HEADS * LANES)
    w_pack = jnp.concatenate([q, g_att, x_rnn, g_rnn, gm_att, gm_rnn, kv, g_nsa], axis=1).astype(BF16)

    def block_diag(w):
        per = MXU_DIM // HEAD_DIM
        w4 = w.reshape(RNN_BLOCKS // per, per, HEAD_DIM, HEAD_DIM)
        eye = jnp.eye(per, dtype=w.dtype)
        return jnp.einsum('cpde,pq->cpdqe', w4, eye).reshape(RNN_BLOCKS // per, MXU_DIM, MXU_DIM).astype(BF16)

    return dict(
        w_pack=w_pack,
        q_gain=jnp.tile(q_norm_g, N_HEADS).reshape(1, ATT_WIDTH),
        kc_gain=jnp.tile(k_norm_g[0], KV_HEADS).reshape(1, KV_WIDTH),
        ks_gain=jnp.tile(k_norm_g[1], KV_HEADS).reshape(1, KV_WIDTH),
        kw_gain=jnp.tile(k_norm_g[2], KV_HEADS).reshape(1, KV_WIDTH),
        wc=jnp.broadcast_to(w_cmp[:, :, None, :], (CMP_BLOCK, 2, KV_HEADS, HEAD_DIM)).reshape(CMP_BLOCK, 2 * KV_WIDTH),
        wr_bd=block_diag(w_rg), wi_bd=block_diag(w_ig),
        wa=w_att_out.astype(BF16), wr=w_rnn_out.astype(BF16), wo=w_out.astype(BF16),
        bd=_block_diag_ones(),
    )


def _prompt_layer(x, pw, norm_g, conv_w, conv_b, b_rg, b_ig, lam):
    b, t, _ = x.shape
    x2d = x.reshape(b * t, D_MODEL)
    z = _proj(x2d, norm_g, pw['w_pack'], tm=min(1024, b * t), tn=1024)
    (qa, leaf_c, leaf_s, leaf_w, cc, kas, v0s, v1s, kaw, v0w, v1w) = _prep_prompt(
        z, b, t, pw['q_gain'], pw['ks_gain'], pw['kw_gain'], pw['wc'], pw['bd'])
    n_cmp = t // CMP_BLOCK
    cc_perm = cc.reshape(b, n_cmp // 2, 2, 2 * KV_WIDTH).transpose(0, 2, 1, 3).reshape(b, n_cmp, 2 * KV_WIDTH)
    kca, v0c, v1c = _cmp_prep(cc_perm, pw['kc_gain'], pw['bd'])
    o_att = _attn_prompt(z, qa, kca, v0c, v1c, kas, v0s, v1s, kaw, v0w, v1w, b, t)
    h_rnn, h_last, conv_tail = _rglru(z, b, t, jnp.zeros((b, 8, D_RNN), F32), jnp.zeros((b, D_RNN), F32),
                                      conv_w, conv_b, pw['wr_bd'], b_rg, pw['wi_bd'], b_ig, lam, tt=256)
    y = _out_proj(x2d, z, o_att, h_rnn, pw['wa'], pw['wr'], pw['wo'], tm=256)
    kv = lambda leaf: leaf.reshape(b, t, 2, KV_HEADS, HEAD_DIM)
    w_keep = min(WINDOW, t)
    return y.reshape(b, t, D_MODEL), (kv(leaf_c), kv(leaf_s), kv(leaf_w)[:, t - w_keep:],
                                      h_last.reshape(b, D_RNN), conv_tail[:, 8 - (CONV_W - 1):])


def _sample_layer(x, cache_cmp, cache_sel, cache_win, state_h, state_conv, page_table, pw,
                  norm_g, conv_w, conv_b, b_rg, b_ig, lam):
    b, t, _ = x.shape
    n_pages = page_table.shape[1]
    past_len = n_pages * PAGE_SIZE
    rows = N_HEADS * t
    x2d = x.reshape(b * t, D_MODEL)
    z = _proj(x2d, norm_g, pw['w_pack'], tm=b * t, tn=1024)
    qn, leaf_c, leaf_s, leaf_w = _prep_sample(z, pw['q_gain'], pw['ks_gain'], pw['kw_gain'], pw['bd'])

    q5 = qn.reshape(b, t, KV_HEADS, Q_PER_KV, HEAD_DIM).transpose(0, 3, 2, 1, 4)
    qbd = (q5[:, :, :, :, None, :] * jnp.eye(KV_HEADS, dtype=F32)[None, None, :, None, :, None])
    qbd = qbd.reshape(b, rows, KV_WIDTH).astype(BF16)

    pool_cmp = cache_cmp.reshape(cache_cmp.shape[0], PAGE_SIZE, 2 * KV_WIDTH)
    pool_sel = cache_sel.reshape(cache_sel.shape[0], PAGE_SIZE, 2 * KV_WIDTH)
    cc = _compress_pages(page_table, pool_cmp, pw['wc'])
    wb = cache_win.shape[1]
    o_cmp, o_win, bias, win_state = _sample_cmp_win(
        qbd, cc, pw['kc_gain'], pw['bd'], cache_win.reshape(b, wb, 2 * KV_WIDTH),
        leaf_w.reshape(b, t, 2 * KV_WIDTH), past_len)

    ns = n_pages // PAGES_PER_STEP
    blocks_per_step = PAGES_PER_STEP * PAGE_SIZE // SEL_BLOCK
    bsel = bias[:, :, ::2].reshape(b, KV_HEADS * t, ns, blocks_per_step).transpose(0, 2, 1, 3)
    bsel = jnp.tile(bsel, (1, 1, Q_PER_KV, 1))
    bsel = jnp.pad(bsel, ((0, 0), (0, 0), (0, 0), (0, LANES - blocks_per_step))).astype(BF16)
    key_block = jnp.arange(PAGES_PER_STEP * PAGE_SIZE) // SEL_BLOCK
    onehot = (jnp.arange(LANES)[:, None] == key_block[None, :]).astype(BF16)
    gl = z[:, C_GNSA:].reshape(b, t, KV_HEADS, LANES)[..., :3 * Q_PER_KV].reshape(b, t, KV_HEADS, 3, Q_PER_KV)
    gl = jnp.pad(gl.transpose(0, 4, 2, 1, 3).reshape(b, rows, 3), ((0, 0), (0, 0), (0, LANES - 3)))
    o_full = _sample_sel(page_table, pool_sel, qbd, bsel, onehot, leaf_s.reshape(b, t, 2 * KV_WIDTH),
                         o_cmp, o_win, gl)
    o6 = o_full.reshape(b, Q_PER_KV, KV_HEADS, t, KV_HEADS, HEAD_DIM)
    o_att = jnp.stack([o6[:, :, g, :, g, :] for g in range(KV_HEADS)], axis=2)
    o_att = o_att.transpose(0, 3, 2, 1, 4).reshape(b * t, ATT_WIDTH)

    hist8 = jnp.pad(state_conv, ((0, 0), (8 - (CONV_W - 1), 0), (0, 0)))
    h_rnn, h_last, conv_tail = _rglru(z, b, t, hist8, state_h, conv_w, conv_b,
                                      pw['wr_bd'], b_rg, pw['wi_bd'], b_ig, lam, tt=t)
    y = _out_proj(x2d, z, o_att, h_rnn, pw['wa'], pw['wr'], pw['wo'], tm=min(256, b * t))
    kv = lambda leaf: leaf.reshape(b, -1, 2, KV_HEADS, HEAD_DIM)
    return y.reshape(b, t, D_MODEL), (kv(leaf_c), kv(leaf_s), kv(win_state),
                                      h_last.reshape(b, D_RNN), conv_tail[:, 8 - (CONV_W - 1):])


def kernel(x_prompt, x_sample, cache_cmp, cache_sel, cache_win, state_h, state_conv, page_table,
           norm_g, w_in, q_norm_g, k_norm_g, w_cmp, conv_w, conv_b, w_rg, b_rg, w_ig, b_ig,
           lru_lambda, w_att_out, w_rnn_out, w_out):
    yp, ys = x_prompt, x_sample
    outs_p, outs_s = [], []
    for l in range(w_in.shape[0]):
        pw = _pack_weights(w_in[l], q_norm_g[l], k_norm_g[l], w_cmp[l], w_rg[l], w_ig[l],
                           w_att_out[l], w_rnn_out[l], w_out[l])
        yp, st_p = _prompt_layer(yp, pw, norm_g[l], conv_w[l], conv_b[l], b_rg[l], b_ig[l], lru_lambda[l])
        ys, st_s = _sample_layer(ys, cache_cmp[l], cache_sel[l], cache_win[l], state_h[l], state_conv[l],
                                 page_table, pw, norm_g[l], conv_w[l], conv_b[l], b_rg[l], b_ig[l], lru_lambda[l])
        outs_p.append(st_p)
        outs_s.append(st_s)
    cmp_p, sel_p, win_p, h_p, conv_p = [jnp.stack(a) for a in zip(*outs_p)]
    cmp_s, sel_s, win_s, h_s, conv_s = [jnp.stack(a) for a in zip(*outs_s)]
    return (yp, ys, cmp_p, sel_p, win_p, h_p, conv_p, cmp_s, sel_s, win_s, h_s, conv_s)
```

```python
import functools

import jax
import jax.numpy as jnp
from jax import lax
from jax.experimental import pallas as pl
from jax.experimental.pallas import tpu as pltpu

F32 = jnp.float32
BF16 = jnp.bfloat16

D_MODEL = 2048
N_HEADS = 16
HEAD_DIM = 64
KV_HEADS = 4
Q_PER_KV = N_HEADS // KV_HEADS
ATT_WIDTH = N_HEADS * HEAD_DIM
KV_WIDTH = KV_HEADS * HEAD_DIM
CMP_BLOCK = 32
SEL_BLOCK = 64
N_SEL = 16
WINDOW = 512
SCALE = HEAD_DIM ** -0.5
D_RNN = 1024
RNN_BLOCKS = 16
CONV_W = 4
LRU_C = 8.0
PAGE_SIZE = 128
EPS = 1e-6
NEG = -1e30
TINY = 1e-30
SEL_BIAS = -30000.0

LANES = 128
MXU_DIM = 256
VMEM_LIMIT = 56 * 1024 * 1024

C_Q = 0
C_GATT = 1024
C_XRNN = 2048
C_GRNN = 3072
C_GMATT = 4096
C_GMRNN = 6144
C_KV = 8192
C_GNSA = 9728
N_PACK = 10240

SEL_PAGES = 16
CMP_PAGES = 32


def _cparams(sem):
    return pltpu.CompilerParams(dimension_semantics=sem, vmem_limit_bytes=VMEM_LIMIT)


def _proj_kernel(x_ref, g_ref, w_ref, o_ref, xn_ref):
    @pl.when(pl.program_id(1) == 0)
    def _():
        x = x_ref[...]
        ms = jnp.mean(x * x, axis=-1, keepdims=True)
        xn_ref[...] = (x * lax.rsqrt(ms + EPS) * g_ref[...]).astype(BF16)

    o_ref[...] = jnp.dot(xn_ref[...], w_ref[...], preferred_element_type=F32)


def _proj(x2d, norm_g, w_pack, tm, tn):
    m = x2d.shape[0]
    return pl.pallas_call(
        _proj_kernel,
        grid=(m // tm, N_PACK // tn),
        in_specs=[
            pl.BlockSpec((tm, D_MODEL), lambda i, j: (i, 0)),
            pl.BlockSpec((1, D_MODEL), lambda i, j: (0, 0)),
            pl.BlockSpec((D_MODEL, tn), lambda i, j: (0, j)),
        ],
        out_specs=pl.BlockSpec((tm, tn), lambda i, j: (i, j)),
        out_shape=jax.ShapeDtypeStruct((m, N_PACK), F32),
        scratch_shapes=[pltpu.VMEM((tm, D_MODEL), BF16)],
        compiler_params=_cparams(("parallel", "arbitrary")),
        name="proj",
    )(x2d, norm_g.reshape(1, D_MODEL), w_pack)


def _lane_iota(shape):
    return lax.broadcasted_iota(jnp.int32, shape, len(shape) - 1)


def _row_iota(shape):
    return lax.broadcasted_iota(jnp.int32, shape, 0)


def _seg_rms(x, bd):
    outs = []
    for c in range(x.shape[1] // MXU_DIM):
        xc = x[:, c * MXU_DIM:(c + 1) * MXU_DIM]
        x2 = xc * xc
        hi = x2.astype(BF16)
        lo = (x2 - hi.astype(F32)).astype(BF16)
        ss = (jnp.dot(hi, bd, preferred_element_type=F32)
              + jnp.dot(lo, bd, preferred_element_type=F32))
        outs.append(xc * lax.rsqrt(ss * (1.0 / HEAD_DIM) + EPS))
    return outs[0] if len(outs) == 1 else jnp.concatenate(outs, axis=1)


def _head_lo(x, h):
    tile = x[:, (h // 2) * LANES:(h // 2 + 1) * LANES]
    if h % 2:
        tile = pltpu.roll(tile, HEAD_DIM, 1)
    return jnp.where(_lane_iota(tile.shape) < HEAD_DIM, tile, 0.0)


def _heads_t(x):
    outs = []
    for c in range(x.shape[1] // LANES):
        xt = x[:, c * LANES:(c + 1) * LANES].T
        outs += [xt[:HEAD_DIM], xt[HEAD_DIM:]]
    return outs


def _dot_nt(a, b):
    return lax.dot_general(a, b, (((1,), (1,)), ((), ())), preferred_element_type=F32)


def _block_diag_ones():
    r = jnp.arange(MXU_DIM) // HEAD_DIM
    return (r[:, None] == r[None, :]).astype(BF16)


def _prep_prompt_kernel(zq_ref, zc_ref, zs_ref, zw_ref, qg_ref, ksg_ref, kwg_ref, wc_ref, bd_ref,
                        qa_ref, lc_ref, ls_ref, lw_ref, cc_ref, kas_ref, vts_ref, kaw_ref, vtw_ref, *, te):
    i = pl.program_id(1)
    bd = bd_ref[...]
    qn = _seg_rms(zq_ref[...], bd) * qg_ref[...] * SCALE
    for h in range(N_HEADS):
        qa_ref[0, h // Q_PER_KV, h % Q_PER_KV] = _head_lo(qn, h).astype(BF16)

    zc = zc_ref[...]
    lc_ref[...] = zc
    cc_ref[0] = jnp.sum(zc.reshape(te // CMP_BLOCK, CMP_BLOCK, 2 * KV_WIDTH) * wc_ref[...][None], axis=1)

    lane = _lane_iota((te, LANES))
    own_block = (i * te + _row_iota((te, LANES))) // SEL_BLOCK
    onehot = jnp.where(lane - HEAD_DIM == own_block, 1.0, 0.0)

    for z_ref, g_ref, l_ref, ka_ref, vt_ref, with_onehot in (
            (zs_ref, ksg_ref, ls_ref, kas_ref, vts_ref, True),
            (zw_ref, kwg_ref, lw_ref, kaw_ref, vtw_ref, False)):
        z = z_ref[...]
        kn = _seg_rms(z[:, :KV_WIDTH], bd) * g_ref[...]
        v = z[:, KV_WIDTH:]
        l_ref[:, :KV_WIDTH] = kn
        l_ref[:, KV_WIDTH:] = v
        for g, vt in enumerate(_heads_t(v)):
            ka = _head_lo(kn, g)
            if with_onehot:
                ka = jnp.where(lane < HEAD_DIM, ka, onehot)
            ka_ref[0, g] = ka.astype(BF16)
            vt_ref[0, g, 0] = vt.astype(BF16)


def _prep_prompt(z, b, t, q_gain, ks_gain, kw_gain, wc, bd, te):
    nt = t // te
    m = b * t

    def zspec(width, col):
        return pl.BlockSpec((te, width), lambda bi, i, c=col // width: (bi * nt + i, c))

    def const(shape):
        return pl.BlockSpec(shape, lambda bi, i: tuple(0 for _ in shape))

    row512 = pl.BlockSpec((te, 2 * KV_WIDTH), lambda bi, i: (bi * nt + i, 0))
    k_spec = pl.BlockSpec((1, KV_HEADS, te, LANES), lambda bi, i: (bi, 0, i, 0))
    k_shape = jax.ShapeDtypeStruct((b, KV_HEADS, t, LANES), BF16)
    vt_spec = pl.BlockSpec((1, KV_HEADS, 1, HEAD_DIM, te), lambda bi, i: (bi, 0, i, 0, 0))
    vt_shape = jax.ShapeDtypeStruct((b, KV_HEADS, nt, HEAD_DIM, te), BF16)
    return pl.pallas_call(
        functools.partial(_prep_prompt_kernel, te=te),
        grid=(b, nt),
        in_specs=[zspec(ATT_WIDTH, C_Q), zspec(512, C_KV), zspec(512, C_KV + 512), zspec(512, C_KV + 1024),
                  const((1, ATT_WIDTH)), const((1, KV_WIDTH)), const((1, KV_WIDTH)),
                  const((CMP_BLOCK, 2 * KV_WIDTH)), const((MXU_DIM, MXU_DIM))],
        out_specs=[pl.BlockSpec((1, KV_HEADS, Q_PER_KV, te, LANES), lambda bi, i: (bi, 0, 0, i, 0)),
                   row512, row512, row512,
                   pl.BlockSpec((1, te // CMP_BLOCK, 2 * KV_WIDTH), lambda bi, i: (bi, i, 0)),
                   k_spec, vt_spec, k_spec, vt_spec],
        out_shape=[jax.ShapeDtypeStruct((b, KV_HEADS, Q_PER_KV, t, LANES), BF16),
                   jax.ShapeDtypeStruct((m, 2 * KV_WIDTH), F32),
                   jax.ShapeDtypeStruct((m, 2 * KV_WIDTH), F32),
                   jax.ShapeDtypeStruct((m, 2 * KV_WIDTH), F32),
                   jax.ShapeDtypeStruct((b, t // CMP_BLOCK, 2 * KV_WIDTH), F32),
                   k_shape, vt_shape, k_shape, vt_shape],
        compiler_params=_cparams(("parallel", "parallel")),
        name="prep_prompt",
    )(z, z, z, z, q_gain, ks_gain, kw_gain, wc, bd)


def _cmp_prep_kernel(cc_ref, kg_ref, bd_ref, kca_ref, vct_ref):
    cc = cc_ref[0]
    kc = _seg_rms(cc[:, :KV_WIDTH], bd_ref[...]) * kg_ref[...]
    for g, vt in enumerate(_heads_t(cc[:, KV_WIDTH:])):
        kca_ref[0, g] = _head_lo(kc, g).astype(BF16)
        vct_ref[0, g] = vt.astype(BF16)


def _cmp_prep(cc_perm, kc_gain, bd):
    b, n, _ = cc_perm.shape
    return pl.pallas_call(
        _cmp_prep_kernel,
        grid=(b,),
        in_specs=[pl.BlockSpec((1, n, 2 * KV_WIDTH), lambda bi: (bi, 0, 0)),
                  pl.BlockSpec((1, KV_WIDTH), lambda bi: (0, 0)),
                  pl.BlockSpec((MXU_DIM, MXU_DIM), lambda bi: (0, 0))],
        out_specs=[pl.BlockSpec((1, KV_HEADS, n, LANES), lambda bi: (bi, 0, 0, 0)),
                   pl.BlockSpec((1, KV_HEADS, HEAD_DIM, n), lambda bi: (bi, 0, 0, 0))],
        out_shape=[jax.ShapeDtypeStruct((b, KV_HEADS, n, LANES), BF16),
                   jax.ShapeDtypeStruct((b, KV_HEADS, HEAD_DIM, n), BF16)],
        compiler_params=_cparams(("parallel",)),
        name="cmp_prep",
    )(cc_perm, kc_gain, bd)


def _rank_select(score, n_keep):
    n, w = score.shape
    sub = lax.broadcasted_iota(jnp.int32, (8, w), 0)
    groups = [score[8 * k:8 * k + 8] for k in range(n // 8)]
    counts = [jnp.zeros((8, w), jnp.int32) for _ in groups]
    for i in range(n):
        si = jnp.broadcast_to(score[i:i + 1, :], (8, w))
        for k, blk in enumerate(groups):
            if 8 * k + 7 < i:
                inc = jnp.where(si > blk, 1, 0)
            elif 8 * k > i:
                inc = jnp.where(si >= blk, 1, 0)
            else:
                inc = jnp.where(sub > (i - 8 * k), jnp.where(si >= blk, 1, 0), jnp.where(si > blk, 1, 0))
            counts[k] = counts[k] + inc
    return jnp.concatenate(counts, axis=0) < n_keep


def _attn_prompt_kernel(qa_ref, kca_ref, vct_ref, kas_ref, vts_ref, kaw_ref, vtw_ref, gl_ref, o_ref, *, tq):
    i = pl.program_id(2)
    rows = Q_PER_KV * tq
    t0 = i * tq
    qa = qa_ref[0, 0].reshape(rows, LANES)
    kca = kca_ref[0, 0]
    n_cmp = kca.shape[0]
    n_selb = n_cmp // 2

    s = _dot_nt(kca, qa)
    n_row = _row_iota((n_cmp, tq))
    cmp_id = 2 * (n_row % n_selb) + n_row // n_selb
    vis = (cmp_id + 1) * CMP_BLOCK - 1 <= t0 + _lane_iota((n_cmp, tq))
    vis = jnp.concatenate([vis] * Q_PER_KV, axis=1)
    s = jnp.where(vis, s, NEG)
    e = jnp.where(vis, jnp.exp(s - jnp.max(s, axis=0, keepdims=True)), 0.0)
    p = e / jnp.maximum(jnp.sum(e, axis=0, keepdims=True), TINY)
    o_cmp = jnp.dot(vct_ref[0, 0], p.astype(BF16), preferred_element_type=F32)

    imp = jnp.zeros((n_selb, tq), F32)
    for r in range(Q_PER_KV):
        imp = imp + (p[:n_selb, r * tq:(r + 1) * tq] + p[n_selb:, r * tq:(r + 1) * tq])
    j_row = _row_iota((n_selb, tq))
    cur = (t0 + _lane_iota((n_selb, tq))) // SEL_BLOCK
    cand = j_row < cur
    forced = (j_row == 0) | (j_row == cur - 1)
    score = jnp.where(cand, jnp.where(forced, jnp.inf, imp), -jnp.inf)
    keep = (_rank_select(score, N_SEL - 1) & cand) | (j_row == cur)
    bias_t = jnp.where(keep, 0.0, SEL_BIAS)
    if n_selb < HEAD_DIM:
        bias_t = jnp.concatenate([bias_t, jnp.full((HEAD_DIM - n_selb, tq), SEL_BIAS, F32)], axis=0)
    bias = jnp.concatenate([jnp.zeros((HEAD_DIM, tq), F32), bias_t], axis=0).T
    bias4 = jnp.concatenate([bias] * Q_PER_KV, axis=0).astype(BF16)
    q_sel = jnp.where(_lane_iota((rows, LANES)) >= HEAD_DIM, bias4, qa)

    kk = _row_iota((tq, rows))
    rr = _lane_iota((tq, rows)) % tq

    def online_step(carry, s, vt):
        m, l, acc = carry
        m_new = jnp.maximum(m, jnp.max(s, axis=0, keepdims=True))
        alpha = jnp.exp(m - m_new)
        pe = jnp.exp(s - m_new)
        l = alpha * l + jnp.sum(pe, axis=0, keepdims=True)
        acc = acc * alpha + jnp.dot(vt, pe.astype(BF16), preferred_element_type=F32)
        return m_new, l, acc

    def finish(carry):
        _, l, acc = carry
        return acc * (1.0 / l)

    init = (jnp.full((1, rows), NEG, F32), jnp.zeros((1, rows), F32), jnp.zeros((HEAD_DIM, rows), F32))

    def k_chunk(ref, c):
        return ref[0, 0, pl.ds(pl.multiple_of(c * tq, tq), tq), :]

    def sel_body(c, carry):
        return online_step(carry, _dot_nt(k_chunk(kas_ref, c), q_sel), vts_ref[0, 0, c])

    carry = lax.fori_loop(0, i, sel_body, init)
    s = jnp.where(kk <= rr, _dot_nt(k_chunk(kas_ref, i), q_sel), NEG)
    o_sel = finish(online_step(carry, s, vts_ref[0, 0, i]))

    s = jnp.where(kk <= rr, _dot_nt(k_chunk(kaw_ref, i), qa), NEG)
    carry = online_step(init, s, vtw_ref[0, 0, i])
    for back in range(1, WINDOW // tq + 1):
        c = jnp.maximum(i - back, 0)
        ok = i >= back
        if back == WINDOW // tq:
            ok = ok & (kk > rr)
        s = jnp.where(ok, _dot_nt(k_chunk(kaw_ref, c), qa), NEG)
        carry = online_step(carry, s, vtw_ref[0, 0, c])
    o_win = finish(carry)

    sig_t = jax.nn.sigmoid(gl_ref[...]).T

    def gate(branch):
        r0 = branch * Q_PER_KV
        return jnp.concatenate([sig_t[r0 + r:r0 + r + 1, :] for r in range(Q_PER_KV)], axis=1)

    o_t = gate(0) * o_cmp + gate(1) * o_sel + gate(2) * o_win
    pairs = [jnp.concatenate([o_t[:, h * tq:(h + 1) * tq], o_t[:, (h + 1) * tq:(h + 2) * tq]], axis=0).T
             for h in range(0, Q_PER_KV, 2)]
    o_ref[...] = jnp.concatenate(pairs, axis=1)


def _attn_prompt(z, qa, kca, vct, kas, vts, kaw, vtw, b, t, tq):
    nt = t // tq
    n_cmp = kca.shape[2]
    qspec = pl.BlockSpec((1, 1, Q_PER_KV, tq, LANES), lambda bi, g, i: (bi, g, 0, i, 0))
    kspec = pl.BlockSpec((1, 1, t, LANES), lambda bi, g, i: (bi, g, 0, 0))
    vspec = pl.BlockSpec((1, 1, nt, HEAD_DIM, tq), lambda bi, g, i: (bi, g, 0, 0, 0))
    gspec = pl.BlockSpec((tq, LANES), lambda bi, g, i: (bi * nt + i, C_GNSA // LANES + g))
    return pl.pallas_call(
        functools.partial(_attn_prompt_kernel, tq=tq),
        grid=(b, KV_HEADS, nt),
        in_specs=[qspec,
                  pl.BlockSpec((1, 1, n_cmp, LANES), lambda bi, g, i: (bi, g, 0, 0)),
                  pl.BlockSpec((1, 1, HEAD_DIM, n_cmp), lambda bi, g, i: (bi, g, 0, 0)),
                  kspec, vspec, kspec, vspec, gspec],
        out_specs=pl.BlockSpec((tq, Q_PER_KV * HEAD_DIM), lambda bi, g, i: (bi * nt + i, g)),
        out_shape=jax.ShapeDtypeStruct((b * t, ATT_WIDTH), F32),
        compiler_params=_cparams(("parallel", "parallel", "arbitrary")),
        name="attn_prompt",
    )(qa, kca, vct, kas, vts, kaw, vtw, z)


def _rglru_kernel(x_ref, hist_ref, h0_ref, cw_ref, cb_ref, wr_ref, br_ref, wi_ref, bi_ref, lam_ref,
                  h_ref, hl_ref, cs_ref, carry_ref, tail_ref, a_ref, u_ref, *, tt):
    i = pl.program_id(1)

    @pl.when(i == 0)
    def _():
        carry_ref[...] = h0_ref[0]
        tail_ref[...] = hist_ref[0]

    x = x_ref[...]
    xe = jnp.concatenate([tail_ref[...], x], axis=0)
    cw = cw_ref[...]
    xc = cb_ref[...]
    for k in range(CONV_W):
        lo = 8 - (CONV_W - 1) + k
        xc = xc + xe[lo:lo + tt] * cw[k:k + 1]
    xb = xc.astype(BF16)

    def gates(w_ref, b_ref):
        parts = [jnp.dot(xb[:, c * MXU_DIM:(c + 1) * MXU_DIM], w_ref[c], preferred_element_type=F32)
                 for c in range(D_RNN // MXU_DIM)]
        return jax.nn.sigmoid(jnp.concatenate(parts, axis=1) + b_ref[...])

    r = gates(wr_ref, br_ref)
    ig = gates(wi_ref, bi_ref)
    nl = -lam_ref[...]
    softplus = jnp.maximum(nl, 0.0) + jnp.log1p(jnp.exp(-jnp.abs(nl)))
    log_a = -LRU_C * r * softplus
    a_ref[...] = jnp.exp(log_a)
    th = jnp.tanh(log_a)
    u_ref[...] = jnp.sqrt(-2.0 * th / (1.0 - th)) * (ig * xc)

    def step(t, h):
        h = a_ref[pl.ds(t, 1), :] * h + u_ref[pl.ds(t, 1), :]
        h_ref[pl.ds(t, 1), :] = h
        return h

    h = lax.fori_loop(0, tt, step, carry_ref[...], unroll=8)
    carry_ref[...] = h
    tail_ref[...] = x[tt - 8:tt]

    @pl.when(i == pl.num_programs(1) - 1)
    def _():
        hl_ref[0] = h
        cs_ref[0] = x[tt - 8:tt]


def _rglru(z, b, t, hist8, h0, conv_w, conv_b, wr_bd, b_rg, wi_bd, b_ig, lam, tt):
    nt = t // tt

    def const(shape):
        return pl.BlockSpec(shape, lambda bi, i: tuple(0 for _ in shape))

    row = lambda v: v.reshape(1, D_RNN)
    return pl.pallas_call(
        functools.partial(_rglru_kernel, tt=tt),
        grid=(b, nt),
        in_specs=[pl.BlockSpec((tt, D_RNN), lambda bi, i: (bi * nt + i, C_XRNN // D_RNN)),
                  pl.BlockSpec((1, 8, D_RNN), lambda bi, i: (bi, 0, 0)),
                  pl.BlockSpec((1, 1, D_RNN), lambda bi, i: (bi, 0, 0)),
                  const((CONV_W, D_RNN)), const((1, D_RNN)),
                  const((D_RNN // MXU_DIM, MXU_DIM, MXU_DIM)), const((1, D_RNN)),
                  const((D_RNN // MXU_DIM, MXU_DIM, MXU_DIM)), const((1, D_RNN)),
                  const((1, D_RNN))],
        out_specs=[pl.BlockSpec((tt, D_RNN), lambda bi, i: (bi * nt + i, 0)),
                   pl.BlockSpec((1, 1, D_RNN), lambda bi, i: (bi, 0, 0)),
                   pl.BlockSpec((1, 8, D_RNN), lambda bi, i: (bi, 0, 0))],
        out_shape=[jax.ShapeDtypeStruct((b * t, D_RNN), F32),
                   jax.ShapeDtypeStruct((b, 1, D_RNN), F32),
                   jax.ShapeDtypeStruct((b, 8, D_RNN), F32)],
        scratch_shapes=[pltpu.VMEM((1, D_RNN), F32), pltpu.VMEM((8, D_RNN), F32),
                        pltpu.VMEM((tt, D_RNN), F32), pltpu.VMEM((tt, D_RNN), F32)],
        compiler_params=_cparams(("parallel", "arbitrary")),
        name="rglru",
    )(z, hist8, h0.reshape(b, 1, D_RNN), conv_w, row(conv_b), wr_bd, row(b_rg), wi_bd, row(b_ig), row(lam))


def _out_kernel(x_ref, oatt_ref, gatt_ref, h_ref, grnn_ref, gma_ref, gmr_ref, wa_ref, wr_ref, wo_ref, y_ref):
    a = (oatt_ref[...] * jax.nn.silu(gatt_ref[...])).astype(BF16)
    u_att = jnp.dot(a, wa_ref[...], preferred_element_type=F32)
    r = (h_ref[...] * jax.nn.silu(grnn_ref[...])).astype(BF16)
    u_rnn = jnp.dot(r, wr_ref[...], preferred_element_type=F32)
    m = jax.nn.sigmoid(gma_ref[...]) * u_att + jax.nn.sigmoid(gmr_ref[...]) * u_rnn
    y_ref[...] = x_ref[...] + jnp.dot(m.astype(BF16), wo_ref[...], preferred_element_type=F32)


def _out_proj(x2d, z, o_att, h_rnn, wa, wr, wo, tm):
    m = x2d.shape[0]

    def zspec(width, col):
        return pl.BlockSpec((tm, width), lambda i, c=col // width: (i, c))

    def wspec(shape):
        return pl.BlockSpec(shape, lambda i: (0, 0), pipeline_mode=pl.Buffered(1))

    rows = lambda width: pl.BlockSpec((tm, width), lambda i: (i, 0))
    return pl.pallas_call(
        _out_kernel,
        grid=(m // tm,),
        in_specs=[rows(D_MODEL), rows(ATT_WIDTH), zspec(ATT_WIDTH, C_GATT), rows(D_RNN), zspec(D_RNN, C_GRNN),
                  zspec(D_MODEL, C_GMATT), zspec(D_MODEL, C_GMRNN),
                  wspec((ATT_WIDTH, D_MODEL)), wspec((D_RNN, D_MODEL)), wspec((D_MODEL, D_MODEL))],
        out_specs=rows(D_MODEL),
        out_shape=jax.ShapeDtypeStruct((m, D_MODEL), F32),
        compiler_params=_cparams(("parallel",)),
        name="out_proj",
    )(x2d, o_att, z, h_rnn, z, z, z, wa, wr, wo)


def _prep_sample_kernel(zq_ref, zc_ref, zs_ref, zw_ref, qg_ref, ksg_ref, kwg_ref, bd_ref,
                        qn_ref, lc_ref, ls_ref, lw_ref):
    bd = bd_ref[...]
    qn_ref[...] = _seg_rms(zq_ref[...], bd) * qg_ref[...] * SCALE
    lc_ref[...] = zc_ref[...]
    for z_ref, g_ref, l_ref in ((zs_ref, ksg_ref, ls_ref), (zw_ref, kwg_ref, lw_ref)):
        z = z_ref[...]
        l_ref[:, :KV_WIDTH] = _seg_rms(z[:, :KV_WIDTH], bd) * g_ref[...]
        l_ref[:, KV_WIDTH:] = z[:, KV_WIDTH:]


def _prep_sample(z, q_gain, ks_gain, kw_gain, bd):
    m = z.shape[0]

    def zspec(width, col):
        return pl.BlockSpec((m, width), lambda i, c=col // width: (0, c))

    def const(shape):
        return pl.BlockSpec(shape, lambda i: tuple(0 for _ in shape))

    full = lambda width: pl.BlockSpec((m, width), lambda i: (0, 0))
    leaf = jax.ShapeDtypeStruct((m, 2 * KV_WIDTH), F32)
    return pl.pallas_call(
        _prep_sample_kernel,
        grid=(1,),
        in_specs=[zspec(ATT_WIDTH, C_Q), zspec(512, C_KV), zspec(512, C_KV + 512), zspec(512, C_KV + 1024),
                  const((1, ATT_WIDTH)), const((1, KV_WIDTH)), const((1, KV_WIDTH)), const((MXU_DIM, MXU_DIM))],
        out_specs=[full(ATT_WIDTH), full(512), full(512), full(512)],
        out_shape=[jax.ShapeDtypeStruct((m, ATT_WIDTH), F32), leaf, leaf, leaf],
        compiler_params=_cparams(("arbitrary",)),
        name="prep_sample",
    )(z, z, z, z, q_gain, ks_gain, kw_gain, bd)


def _page_specs(n):
    return [pl.BlockSpec((1, 2 * KV_WIDTH, PAGE_SIZE),
                         lambda bi, s, pt, k=k, n=n: (pt[bi, s * n + k], 0, 0)) for k in range(n)]


def _compress_pages_kernel(pt_ref, *refs):
    del pt_ref
    pages = refs[:CMP_PAGES]
    wt_ref, seg_ref, ok_ref, ov_ref = refs[CMP_PAGES:]
    wt = wt_ref[...]
    acc_k = jnp.zeros((KV_WIDTH, LANES), F32)
    acc_v = jnp.zeros((KV_WIDTH, LANES), F32)
    for pair in range(CMP_PAGES // 2):
        pa = pages[2 * pair][0] * wt
        pb = pages[2 * pair + 1][0] * wt
        seg = seg_ref[pair]
        lhs_k = jnp.concatenate([pa[:KV_WIDTH], pb[:KV_WIDTH]], axis=1).astype(BF16)
        lhs_v = jnp.concatenate([pa[KV_WIDTH:], pb[KV_WIDTH:]], axis=1).astype(BF16)
        acc_k = acc_k + jnp.dot(lhs_k, seg, preferred_element_type=F32)
        acc_v = acc_v + jnp.dot(lhs_v, seg, preferred_element_type=F32)
    ok_ref[0] = acc_k
    ov_ref[0] = acc_v


def _compress_pages(page_table, pool_t, wt, seg):
    b, n_pages = page_table.shape
    ns = n_pages // CMP_PAGES
    out_spec = pl.BlockSpec((1, KV_WIDTH, LANES), lambda bi, s, pt: (bi, 0, s))
    grid_spec = pltpu.PrefetchScalarGridSpec(
        num_scalar_prefetch=1,
        grid=(b, ns),
        in_specs=_page_specs(CMP_PAGES) + [
            pl.BlockSpec((2 * KV_WIDTH, PAGE_SIZE), lambda bi, s, pt: (0, 0)),
            pl.BlockSpec((CMP_PAGES // 2, 2 * PAGE_SIZE, LANES), lambda bi, s, pt: (0, 0, 0))],
        out_specs=[out_spec, out_spec],
    )
    shape = jax.ShapeDtypeStruct((b, KV_WIDTH, ns * LANES), F32)
    return pl.pallas_call(
        _compress_pages_kernel,
        grid_spec=grid_spec,
        out_shape=[shape, shape],
        compiler_params=_cparams(("parallel", "arbitrary")),
        name="compress_pages",
    )(page_table, *([pool_t] * CMP_PAGES), wt, seg)


def _softmax_lanes(parts, masks):
    parts = [jnp.where(mk, s, NEG) for s, mk in zip(parts, masks)]
    mx = functools.reduce(jnp.maximum, [jnp.max(s, axis=1, keepdims=True) for s in parts])
    es = [jnp.where(mk, jnp.exp(s - mx), 0.0) for s, mk in zip(parts, masks)]
    den = jnp.maximum(sum(jnp.sum(e, axis=1, keepdims=True) for e in es), TINY)
    return [e / den for e in es]


def _sample_cmp_win_kernel(qbd_ref, ck_ref, cv_ref, kg_ref, cwin_ref, wnew_ref,
                           ocmp_ref, owin_ref, bias_ref, wst_ref, *, past_len, t_new):
    qbd = qbd_ref[0]
    rows = qbd.shape[0]
    ck = ck_ref[0]
    n_cmp = ck.shape[1]
    parts = []
    for g in range(KV_HEADS):
        xs = ck[g * HEAD_DIM:(g + 1) * HEAD_DIM]
        parts.append(xs * lax.rsqrt(jnp.mean(xs * xs, axis=0, keepdims=True) + EPS))
    kc = (jnp.concatenate(parts, axis=0) * kg_ref[...]).astype(BF16)

    t_row = _row_iota((rows, n_cmp)) % t_new
    n_lane = _lane_iota((rows, n_cmp))
    vis = (n_lane + 1) * CMP_BLOCK - 1 <= past_len + t_row
    (p,) = _softmax_lanes([jnp.dot(qbd, kc, preferred_element_type=F32)], [vis])
    ocmp_ref[0] = _dot_nt(p.astype(BF16), cv_ref[0].astype(BF16))

    gt = KV_HEADS * t_new
    psum = p[0:gt]
    for r in range(1, Q_PER_KV):
        psum = psum + p[r * gt:(r + 1) * gt]
    imp = psum + pltpu.roll(psum, n_cmp - 1, 1)
    lane = _lane_iota((gt, n_cmp))
    j = lane // 2
    cur = (past_len + _row_iota((gt, n_cmp)) % t_new) // SEL_BLOCK
    cand = (lane % 2 == 0) & (j < cur)
    forced = (j == 0) | (j == cur - 1)
    score = jnp.where(cand, jnp.where(forced, jnp.inf, imp), -jnp.inf)
    count = jnp.zeros((gt, n_cmp), jnp.int32)
    for c in range(0, n_cmp, 2):
        col = score[:, c:c + 1]
        count = count + jnp.where(lane > c, jnp.where(col >= score, 1, 0), jnp.where(col > score, 1, 0))
    keep = (count < N_SEL - 1) & cand
    bias_ref[0] = jnp.where(keep, 0.0, SEL_BIAS)

    cwin = cwin_ref[0]
    wb = cwin.shape[1]
    wnew = wnew_ref[0]
    wnew_p = jnp.concatenate([wnew, jnp.zeros((LANES - t_new, 2 * KV_WIDTH), F32)], axis=0)
    t_w = _row_iota((rows, wb)) % t_new
    idx = _lane_iota((rows, wb))
    ok_w = (idx <= wb + t_w) & (idx > wb + t_w - WINDOW)
    t_n = _row_iota((rows, LANES)) % t_new
    idx_n = wb + _lane_iota((rows, LANES))
    ok_n = (idx_n <= wb + t_n) & (idx_n > wb + t_n - WINDOW) & (_lane_iota((rows, LANES)) < t_new)
    p_w, p_n = _softmax_lanes([jnp.dot(qbd, cwin[:KV_WIDTH].astype(BF16), preferred_element_type=F32),
                               _dot_nt(qbd, wnew_p[:, :KV_WIDTH].astype(BF16))], [ok_w, ok_n])
    owin_ref[0] = (_dot_nt(p_w.astype(BF16), cwin[KV_WIDTH:].astype(BF16))
                   + jnp.dot(p_n.astype(BF16), wnew_p[:, KV_WIDTH:].astype(BF16), preferred_element_type=F32))

    keep_lanes = LANES - t_new
    new_t = jnp.concatenate([jnp.zeros((keep_lanes, 2 * KV_WIDTH), F32), wnew], axis=0).T
    rolled = [pltpu.roll(cwin[:, c * LANES:(c + 1) * LANES], keep_lanes, 1) for c in range(wb // LANES)]
    rolled.append(new_t)
    first = _lane_iota((2 * KV_WIDTH, LANES)) < keep_lanes
    for c in range(wb // LANES):
        wst_ref[0, :, c * LANES:(c + 1) * LANES] = jnp.where(first, rolled[c], rolled[c + 1])


def _sample_cmp_win(qbd, ck_t, cv_t, kc_gain_col, cwin_t, wnew, past_len):
    b, rows, _ = qbd.shape
    n_cmp = ck_t.shape[2]
    wb = cwin_t.shape[2]
    t_new = wnew.shape[1]
    gt = KV_HEADS * t_new

    def per_b(shape):
        return pl.BlockSpec((1,) + shape, lambda bi: (bi, 0, 0))

    return pl.pallas_call(
        functools.partial(_sample_cmp_win_kernel, past_len=past_len, t_new=t_new),
        grid=(b,),
        in_specs=[per_b((rows, MXU_DIM)), per_b((KV_WIDTH, n_cmp)), per_b((KV_WIDTH, n_cmp)),
                  pl.BlockSpec((KV_WIDTH, 1), lambda bi: (0, 0)),
                  per_b((2 * KV_WIDTH, wb)), per_b((t_new, 2 * KV_WIDTH))],
        out_specs=[per_b((rows, MXU_DIM)), per_b((rows, MXU_DIM)), per_b((gt, n_cmp)), per_b((2 * KV_WIDTH, wb))],
        out_shape=[jax.ShapeDtypeStruct((b, rows, MXU_DIM), F32), jax.ShapeDtypeStruct((b, rows, MXU_DIM), F32),
                   jax.ShapeDtypeStruct((b, gt, n_cmp), F32), jax.ShapeDtypeStruct((b, 2 * KV_WIDTH, wb), F32)],
        compiler_params=_cparams(("parallel",)),
        name="sample_cmp_win",
    )(qbd, ck_t, cv_t, kc_gain_col, cwin_t, wnew)


def _sample_sel_kernel(pt_ref, *refs, t_new):
    del pt_ref
    pages = refs[:SEL_PAGES]
    (qbd_ref, bsel_ref, onehot_ref, snew_ref, ocmp_ref, owin_ref, gl_ref,
     o_ref, m_ref, l_ref, acc_ref) = refs[SEL_PAGES:]
    s_idx = pl.program_id(1)
    qbd = qbd_ref[0]
    rows = qbd.shape[0]

    @pl.when(s_idx == 0)
    def _():
        m_ref[...] = jnp.full(m_ref.shape, NEG, F32)
        l_ref[...] = jnp.zeros(l_ref.shape, F32)
        acc_ref[...] = jnp.zeros(acc_ref.shape, F32)

    def update(s, pv):
        m = m_ref[...]
        m_new = jnp.maximum(m, jnp.max(s, axis=1, keepdims=True))
        alpha = jnp.exp(m - m_new)
        pe = jnp.exp(s - m_new)
        l_ref[...] = alpha * l_ref[...] + jnp.sum(pe, axis=1, keepdims=True)
        acc_ref[...] = alpha * acc_ref[...] + pv(pe.astype(BF16))
        m_ref[...] = m_new

    k_t = jnp.concatenate([pg[0, :KV_WIDTH, :].astype(BF16) for pg in pages], axis=1)
    v_t = jnp.concatenate([pg[0, KV_WIDTH:, :].astype(BF16) for pg in pages], axis=1)
    s = (jnp.dot(qbd, k_t, preferred_element_type=F32)
         + jnp.dot(bsel_ref[0, 0], onehot_ref[...], preferred_element_type=F32))
    update(s, lambda pb: _dot_nt(pb, v_t))

    @pl.when(s_idx == pl.num_programs(1) - 1)
    def _():
        snew = jnp.concatenate([snew_ref[0], jnp.zeros((LANES - t_new, 2 * KV_WIDTH), F32)], axis=0)
        t_q = _row_iota((rows, LANES)) % t_new
        t_k = _lane_iota((rows, LANES))
        s = jnp.where((t_k <= t_q) & (t_k < t_new), _dot_nt(qbd, snew[:, :KV_WIDTH].astype(BF16)), NEG)
        v_new = snew[:, KV_WIDTH:].astype(BF16)
        update(s, lambda pb: jnp.dot(pb, v_new, preferred_element_type=F32))
        o_sel = acc_ref[...] / l_ref[...]
        sig = jax.nn.sigmoid(gl_ref[0])
        o_ref[0] = sig[:, 0:1] * ocmp_ref[0] + sig[:, 1:2] * o_sel + sig[:, 2:3] * owin_ref[0]


def _sample_sel(page_table, pool_t, qbd, bsel, onehot, snew, ocmp, owin, gl):
    b, n_pages = page_table.shape
    ns = n_pages // SEL_PAGES
    rows = qbd.shape[1]
    t_new = snew.shape[1]
    keys = SEL_PAGES * PAGE_SIZE

    def per_b(shape):
        return pl.BlockSpec((1,) + shape, lambda bi, s, pt: (bi,) + tuple(0 for _ in shape))

    grid_spec = pltpu.PrefetchScalarGridSpec(
        num_scalar_prefetch=1,
        grid=(b, ns),
        in_specs=_page_specs(SEL_PAGES) + [
            per_b((rows, MXU_DIM)),
            pl.BlockSpec((1, 1, rows, LANES), lambda bi, s, pt: (bi, s, 0, 0)),
            pl.BlockSpec((LANES, keys), lambda bi, s, pt: (0, 0)),
            per_b((t_new, 2 * KV_WIDTH)), per_b((rows, MXU_DIM)), per_b((rows, MXU_DIM)), per_b((rows, LANES))],
        out_specs=per_b((rows, MXU_DIM)),
        scratch_shapes=[pltpu.VMEM((rows, 1), F32), pltpu.VMEM((rows, 1), F32), pltpu.VMEM((rows, MXU_DIM), F32)],
    )
    return pl.pallas_call(
        functools.partial(_sample_sel_kernel, t_new=t_new),
        grid_spec=grid_spec,
        out_shape=jax.ShapeDtypeStruct((b, rows, MXU_DIM), F32),
        compiler_params=_cparams(("parallel", "arbitrary")),
        name="sample_sel",
    )(page_table, *([pool_t] * SEL_PAGES), qbd, bsel, onehot, snew, ocmp, owin, gl)


def _pack_weights(w_in, q_norm_g, k_norm_g, w_cmp, w_rg, w_ig, w_att_out, w_rnn_out, w_out):
    o = 0
    parts = []
    for size in (ATT_WIDTH, 6 * KV_WIDTH, 3 * N_HEADS, ATT_WIDTH, D_RNN, D_RNN, D_MODEL, D_MODEL):
        parts.append(w_in[:, o:o + size])
        o += size
    q, kv, g_nsa, g_att, x_rnn, g_rnn, gm_att, gm_rnn = parts
    g_nsa = g_nsa.reshape(D_MODEL, 3, KV_HEADS, Q_PER_KV).transpose(0, 2, 1, 3).reshape(D_MODEL, KV_HEADS, 3 * Q_PER_KV)
    g_nsa = jnp.pad(g_nsa, ((0, 0), (0, 0), (0, LANES - 3 * Q_PER_KV))).reshape(D_MODEL, KV_HEADS * LANES)
    w_pack = jnp.concatenate([q, g_att, x_rnn, g_rnn, gm_att, gm_rnn, kv, g_nsa], axis=1).astype(BF16)

    def block_diag(w):
        per = MXU_DIM // HEAD_DIM
        w4 = w.reshape(RNN_BLOCKS // per, per, HEAD_DIM, HEAD_DIM)
        eye = jnp.eye(per, dtype=w.dtype)
        return jnp.einsum('cpde,pq->cpdqe', w4, eye).reshape(RNN_BLOCKS // per, MXU_DIM, MXU_DIM).astype(BF16)

    wt = jnp.broadcast_to(w_cmp.transpose(1, 2, 0)[:, None], (2, KV_HEADS, HEAD_DIM, CMP_BLOCK))
    wt = jnp.tile(wt, (1, 1, 1, PAGE_SIZE // CMP_BLOCK)).reshape(2 * KV_WIDTH, PAGE_SIZE)
    lane = jnp.arange(2 * PAGE_SIZE)
    col = (PAGE_SIZE // CMP_BLOCK) * (lane // PAGE_SIZE) + (lane % PAGE_SIZE) // CMP_BLOCK
    pair = jnp.arange(CMP_PAGES // 2)
    seg = (jnp.arange(LANES)[None, None, :]
           == (2 * (PAGE_SIZE // CMP_BLOCK) * pair[:, None, None] + col[None, :, None])).astype(BF16)

    return dict(
        w_pack=w_pack,
        q_gain=jnp.tile(q_norm_g, N_HEADS).reshape(1, ATT_WIDTH),
        kc_gain=jnp.tile(k_norm_g[0], KV_HEADS).reshape(1, KV_WIDTH),
        ks_gain=jnp.tile(k_norm_g[1], KV_HEADS).reshape(1, KV_WIDTH),
        kw_gain=jnp.tile(k_norm_g[2], KV_HEADS).reshape(1, KV_WIDTH),
        wc=jnp.broadcast_to(w_cmp[:, :, None, :], (CMP_BLOCK, 2, KV_HEADS, HEAD_DIM)).reshape(CMP_BLOCK, 2 * KV_WIDTH),
        wt=wt, seg=seg,
        wr_bd=block_diag(w_rg), wi_bd=block_diag(w_ig),
        wa=w_att_out.astype(BF16), wr=w_rnn_out.astype(BF16), wo=w_out.astype(BF16),
        bd=_block_diag_ones(),
    )


def _prompt_layer(x, pw, norm_g, conv_w, conv_b, b_rg, b_ig, lam):
    b, t, _ = x.shape
    tq = WINDOW // 2
    x2d = x.reshape(b * t, D_MODEL)
    z = _proj(x2d, norm_g, pw['w_pack'], tm=min(1024, b * t), tn=1024)
    qa, leaf_c, leaf_s, leaf_w, cc, kas, vts, kaw, vtw = _prep_prompt(
        z, b, t, pw['q_gain'], pw['ks_gain'], pw['kw_gain'], pw['wc'], pw['bd'], te=tq)
    n_cmp = t // CMP_BLOCK
    cc_perm = cc.reshape(b, n_cmp // 2, 2, 2 * KV_WIDTH).transpose(0, 2, 1, 3).reshape(b, n_cmp, 2 * KV_WIDTH)
    kca, vct = _cmp_prep(cc_perm, pw['kc_gain'], pw['bd'])
    o_att = _attn_prompt(z, qa, kca, vct, kas, vts, kaw, vtw, b, t, tq)
    h_rnn, h_last, conv_tail = _rglru(z, b, t, jnp.zeros((b, 8, D_RNN), F32), jnp.zeros((b, D_RNN), F32),
                                      conv_w, conv_b, pw['wr_bd'], b_rg, pw['wi_bd'], b_ig, lam, tt=256)
    y = _out_proj(x2d, z, o_att, h_rnn, pw['wa'], pw['wr'], pw['wo'], tm=256)
    kv = lambda leaf: leaf.reshape(b, t, 2, KV_HEADS, HEAD_DIM)
    w_keep = min(WINDOW, t)
    return y.reshape(b, t, D_MODEL), (kv(leaf_c), kv(leaf_s), kv(leaf_w)[:, t - w_keep:],
                                      h_last.reshape(b, D_RNN), conv_tail[:, 8 - (CONV_W - 1):])


def _feature_major(cache):
    n, tokens = cache.shape[:2]
    return cache.transpose(0, 2, 3, 4, 1).reshape(n, 2 * KV_WIDTH, tokens)


def _sample_layer(x, cache_cmp, cache_sel, cache_win, state_h, state_conv, page_table, pw,
                  norm_g, conv_w, conv_b, b_rg, b_ig, lam):
    b, t, _ = x.shape
    n_pages = page_table.shape[1]
    past_len = n_pages * PAGE_SIZE
    rows = N_HEADS * t
    x2d = x.reshape(b * t, D_MODEL)
    z = _proj(x2d, norm_g, pw['w_pack'], tm=b * t, tn=1024)
    qn, leaf_c, leaf_s, leaf_w = _prep_sample(z, pw['q_gain'], pw['ks_gain'], pw['kw_gain'], pw['bd'])

    q5 = qn.reshape(b, t, KV_HEADS, Q_PER_KV, HEAD_DIM).transpose(0, 3, 2, 1, 4)
    qbd = (q5[:, :, :, :, None, :] * jnp.eye(KV_HEADS, dtype=F32)[None, None, :, None, :, None])
    qbd = qbd.reshape(b, rows, KV_WIDTH).astype(BF16)

    ck_t, cv_t = _compress_pages(page_table, _feature_major(cache_cmp), pw['wt'], pw['seg'])
    wb = cache_win.shape[1]
    o_cmp, o_win, bias, win_state_t = _sample_cmp_win(
        qbd, ck_t, cv_t, pw['kc_gain'].reshape(KV_WIDTH, 1), _feature_major(cache_win),
        leaf_w.reshape(b, t, 2 * KV_WIDTH), past_len)
    win_state = win_state_t.reshape(b, 2, KV_HEADS, HEAD_DIM, wb).transpose(0, 4, 1, 2, 3)

    ns = n_pages // SEL_PAGES
    blocks_per_step = SEL_PAGES * PAGE_SIZE // SEL_BLOCK
    bsel = bias[:, :, ::2].reshape(b, KV_HEADS * t, ns, blocks_per_step).transpose(0, 2, 1, 3)
    bsel = jnp.tile(bsel, (1, 1, Q_PER_KV, 1))
    bsel = jnp.pad(bsel, ((0, 0), (0, 0), (0, 0), (0, LANES - blocks_per_step))).astype(BF16)
    key_block = jnp.arange(SEL_PAGES * PAGE_SIZE) // SEL_BLOCK
    onehot = (jnp.arange(LANES)[:, None] == key_block[None, :]).astype(BF16)
    gl = z[:, C_GNSA:].reshape(b, t, KV_HEADS, LANES)[..., :3 * Q_PER_KV].reshape(b, t, KV_HEADS, 3, Q_PER_KV)
    gl = jnp.pad(gl.transpose(0, 4, 2, 1, 3).reshape(b, rows, 3), ((0, 0), (0, 0), (0, LANES - 3)))
    o_full = _sample_sel(page_table, _feature_major(cache_sel), qbd, bsel, onehot,
                         leaf_s.reshape(b, t, 2 * KV_WIDTH), o_cmp, o_win, gl)
    o6 = o_full.reshape(b, Q_PER_KV, KV_HEADS, t, KV_HEADS, HEAD_DIM)
    o_att = jnp.stack([o6[:, :, g, :, g, :] for g in range(KV_HEADS)], axis=2)
    o_att = o_att.transpose(0, 3, 2, 1, 4).reshape(b * t, ATT_WIDTH)

    hist8 = jnp.pad(state_conv, ((0, 0), (8 - (CONV_W - 1), 0), (0, 0)))
    h_rnn, h_last, conv_tail = _rglru(z, b, t, hist8, state_h, conv_w, conv_b,
                                      pw['wr_bd'], b_rg, pw['wi_bd'], b_ig, lam, tt=t)
    y = _out_proj(x2d, z, o_att, h_rnn, pw['wa'], pw['wr'], pw['wo'], tm=min(256, b * t))
    kv = lambda leaf: leaf.reshape(b, -1, 2, KV_HEADS, HEAD_DIM)
    return y.reshape(b, t, D_MODEL), (kv(leaf_c), kv(leaf_s), win_state,
                                      h_last.reshape(b, D_RNN), conv_tail[:, 8 - (CONV_W - 1):])


def kernel(x_prompt, x_sample, cache_cmp, cache_sel, cache_win, state_h, state_conv, page_table,
           norm_g, w_in, q_norm_g, k_norm_g, w_cmp, conv_w, conv_b, w_rg, b_rg, w_ig, b_ig,
           lru_lambda, w_att_out, w_rnn_out, w_out):
    yp, ys = x_prompt, x_sample
    outs_p, outs_s = [], []
    for l in range(w_in.shape[0]):
        pw = _pack_weights(w_in[l], q_norm_g[l], k_norm_g[l], w_cmp[l], w_rg[l], w_ig[l],
                           w_att_out[l], w_rnn_out[l], w_out[l])
        yp, st_p = _prompt_layer(yp, pw, norm_g[l], conv_w[l], conv_b[l], b_rg[l], b_ig[l], lru_lambda[l])
        ys, st_s = _sample_layer(ys, cache_cmp[l], cache_sel[l], cache_win[l], state_h[l], state_conv[l],
                                 page_table, pw, norm_g[l], conv_w[l], conv_b[l], b_rg[l], b_ig[l], lru_lambda[l])
        outs_p.append(st_p)
        outs_s.append(st_s)
    cmp_p, sel_p, win_p, h_p, conv_p = [jnp.stack(a) for a in zip(*outs_p)]
    cmp_s, sel_s, win_s, h_s, conv_s = [jnp.stack(a) for a in zip(*outs_s)]
    return (yp, ys, cmp_p, sel_p, win_p, h_p, conv_p, cmp_s, sel_s, win_s, h_s, conv_s)
```

```python
import functools

import jax
import jax.numpy as jnp
from jax import lax
from jax.experimental import pallas as pl
from jax.experimental.pallas import tpu as pltpu

F32 = jnp.float32
BF16 = jnp.bfloat16

D_MODEL = 2048
N_HEADS = 16
HEAD_DIM = 64
KV_HEADS = 4
Q_PER_KV = N_HEADS // KV_HEADS
ATT_WIDTH = N_HEADS * HEAD_DIM
KV_WIDTH = KV_HEADS * HEAD_DIM
CMP_BLOCK = 32
SEL_BLOCK = 64
N_SEL = 16
WINDOW = 512
SCALE = HEAD_DIM ** -0.5
LOG2E = 1.4426950408889634
D_RNN = 1024
RNN_BLOCKS = 16
CONV_W = 4
LRU_C = 8.0
PAGE_SIZE = 128
EPS = 1e-6
NEG = -1e30
TINY = 1e-30
SEL_BIAS = -30000.0

LANES = 128
MXU_DIM = 256
VMEM_LIMIT = 56 * 1024 * 1024

C_Q = 0
C_GATT = 1024
C_XRNN = 2048
C_GRNN = 3072
C_GMATT = 4096
C_GMRNN = 6144
C_KV = 8192
C_GNSA = 9728
N_PACK = 10240
PROJ_TILE = 1024
_W_Q, _W_KV, _W_GNSA, _W_GATT, _W_XRNN, _W_GRNN, _W_GMATT, _W_GMRNN = 0, 1024, 2560, 2608, 3632, 4656, 5680, 7728
PROJ_ROW_STARTS = (_W_Q, _W_GATT, _W_XRNN, _W_GRNN, _W_GMATT, _W_GMATT + 1024, _W_GMRNN, _W_GMRNN + 1024,
                   _W_KV, _W_KV + 1024)

SEL_PAGES = 32
SEL_SUB = 32
CMP_PAGES = 32
GROUPS_PER_STEP = 2
VT_ROWS = 80


def _cparams(sem):
    return pltpu.CompilerParams(dimension_semantics=sem, vmem_limit_bytes=VMEM_LIMIT)


def _proj_kernel(starts_ref, x_ref, g_ref, w_ref, o_ref, xn_ref):
    del starts_ref
    @pl.when(pl.program_id(1) == 0)
    def _():
        x = x_ref[...]
        ms = jnp.mean(x * x, axis=-1, keepdims=True)
        xn_ref[...] = (x * lax.rsqrt(ms + EPS) * g_ref[...]).astype(BF16)

    o_ref[...] = lax.dot_general(xn_ref[...], w_ref[...].astype(BF16), (((1,), (1,)), ((), ())),
                                 preferred_element_type=F32)


def _proj(x2d, norm_g, w_t, tm):
    m = x2d.shape[0]
    tn = PROJ_TILE
    grid_spec = pltpu.PrefetchScalarGridSpec(
        num_scalar_prefetch=1,
        grid=(m // tm, N_PACK // tn),
        in_specs=[
            pl.BlockSpec((tm, D_MODEL), lambda i, j, st: (i, 0)),
            pl.BlockSpec((1, D_MODEL), lambda i, j, st: (0, 0)),
            pl.BlockSpec((pl.Element(tn), pl.Element(D_MODEL)), lambda i, j, st: (pl.multiple_of(st[j], 16), 0)),
        ],
        out_specs=pl.BlockSpec((tm, tn), lambda i, j, st: (i, j)),
        scratch_shapes=[pltpu.VMEM((tm, D_MODEL), BF16)],
    )
    return pl.pallas_call(
        _proj_kernel,
        grid_spec=grid_spec,
        out_shape=jax.ShapeDtypeStruct((m, N_PACK), F32),
        compiler_params=_cparams(("parallel", "arbitrary")),
        name="proj",
    )(jnp.asarray(PROJ_ROW_STARTS, jnp.int32), x2d, norm_g.reshape(1, D_MODEL), w_t)


def _lane_iota(shape):
    return lax.broadcasted_iota(jnp.int32, shape, len(shape) - 1)


def _row_iota(shape):
    return lax.broadcasted_iota(jnp.int32, shape, 0)


def _seg_rms(x, bd):
    outs = []
    for c in range(x.shape[1] // MXU_DIM):
        xc = x[:, c * MXU_DIM:(c + 1) * MXU_DIM]
        x2 = xc * xc
        hi = x2.astype(BF16)
        lo = (x2 - hi.astype(F32)).astype(BF16)
        ss = (jnp.dot(hi, bd, preferred_element_type=F32)
              + jnp.dot(lo, bd, preferred_element_type=F32))
        outs.append(xc * lax.rsqrt(ss * (1.0 / HEAD_DIM) + EPS))
    return outs[0] if len(outs) == 1 else jnp.concatenate(outs, axis=1)


def _head_lo(x, h):
    tile = x[:, (h // 2) * LANES:(h // 2 + 1) * LANES]
    if h % 2:
        tile = pltpu.roll(tile, HEAD_DIM, 1)
    return jnp.where(_lane_iota(tile.shape) < HEAD_DIM, tile, 0.0)


def _heads_t(x):
    outs = []
    for c in range(x.shape[1] // LANES):
        xt = x[:, c * LANES:(c + 1) * LANES].T
        outs += [xt[:HEAD_DIM], xt[HEAD_DIM:]]
    return outs


def _dot_nt(a, b):
    return lax.dot_general(a, b, (((1,), (1,)), ((), ())), preferred_element_type=F32)


def _block_diag_ones():
    r = jnp.arange(MXU_DIM) // HEAD_DIM
    return (r[:, None] == r[None, :]).astype(BF16)


def _prep_prompt_kernel(zq_ref, zc_ref, zs_ref, zw_ref, qg_ref, ksg_ref, kwg_ref, wc_ref, bd_ref,
                        qa_ref, lc_ref, ls_ref, lw_ref, cc_ref, kas_ref, vts_ref, kaw_ref, vtw_ref, *, te):
    i = pl.program_id(1)
    bd = bd_ref[...]
    qn = _seg_rms(zq_ref[...], bd) * qg_ref[...] * (SCALE * LOG2E)
    for h in range(N_HEADS):
        qa_ref[0, h // Q_PER_KV, h % Q_PER_KV] = _head_lo(qn, h).astype(BF16)

    zc = zc_ref[...]
    for c in range(2 * KV_WIDTH // LANES):
        lc_ref[0, c * LANES:(c + 1) * LANES, :] = zc[:, c * LANES:(c + 1) * LANES].T
    cc_ref[0] = jnp.sum(zc.reshape(te // CMP_BLOCK, CMP_BLOCK, 2 * KV_WIDTH) * wc_ref[...][None], axis=1)

    lane = _lane_iota((te, LANES))
    own_block = (i * te + _row_iota((te, LANES))) // SEL_BLOCK
    onehot = jnp.where(lane - HEAD_DIM == own_block, 1.0, 0.0)
    ones_rows = jnp.where(_row_iota((VT_ROWS - HEAD_DIM, te)) == 0, 1.0, 0.0).astype(BF16)

    for z_ref, g_ref, l_ref, ka_ref, vt_ref, with_onehot in (
            (zs_ref, ksg_ref, ls_ref, kas_ref, vts_ref, True),
            (zw_ref, kwg_ref, lw_ref, kaw_ref, vtw_ref, False)):
        z = z_ref[...]
        kn = _seg_rms(z[:, :KV_WIDTH], bd) * g_ref[...]
        v = z[:, KV_WIDTH:]
        for c in range(KV_WIDTH // LANES):
            l_ref[0, c * LANES:(c + 1) * LANES, :] = kn[:, c * LANES:(c + 1) * LANES].T
        for g, vt in enumerate(_heads_t(v)):
            l_ref[0, KV_WIDTH + g * HEAD_DIM:KV_WIDTH + (g + 1) * HEAD_DIM, :] = vt
            ka = _head_lo(kn, g)
            if with_onehot:
                ka = jnp.where(lane < HEAD_DIM, ka, onehot)
            ka_ref[0, g] = ka.astype(BF16)
            vt_ref[0, g, 0] = jnp.concatenate([vt.astype(BF16), ones_rows], axis=0)


def _prep_prompt(z, b, t, q_gain, ks_gain, kw_gain, wc, bd, te):
    nt = t // te

    def zspec(width, col):
        return pl.BlockSpec((te, width), lambda bi, i, c=col // width: (bi * nt + i, c))

    def const(shape):
        return pl.BlockSpec(shape, lambda bi, i: tuple(0 for _ in shape))

    leaf_spec = pl.BlockSpec((1, 2 * KV_WIDTH, te), lambda bi, i: (bi, 0, i))
    leaf_shape = jax.ShapeDtypeStruct((b, 2 * KV_WIDTH, t), F32)
    k_spec = pl.BlockSpec((1, KV_HEADS, te, LANES), lambda bi, i: (bi, 0, i, 0))
    k_shape = jax.ShapeDtypeStruct((b, KV_HEADS, t, LANES), BF16)
    vt_spec = pl.BlockSpec((1, KV_HEADS, 1, VT_ROWS, te), lambda bi, i: (bi, 0, i, 0, 0))
    vt_shape = jax.ShapeDtypeStruct((b, KV_HEADS, nt, VT_ROWS, te), BF16)
    return pl.pallas_call(
        functools.partial(_prep_prompt_kernel, te=te),
        grid=(b, nt),
        in_specs=[zspec(ATT_WIDTH, C_Q), zspec(512, C_KV), zspec(512, C_KV + 512), zspec(512, C_KV + 1024),
                  const((1, ATT_WIDTH)), const((1, KV_WIDTH)), const((1, KV_WIDTH)),
                  const((CMP_BLOCK, 2 * KV_WIDTH)), const((MXU_DIM, MXU_DIM))],
        out_specs=[pl.BlockSpec((1, KV_HEADS, Q_PER_KV, te, LANES), lambda bi, i: (bi, 0, 0, i, 0)),
                   leaf_spec, leaf_spec, leaf_spec,
                   pl.BlockSpec((1, te // CMP_BLOCK, 2 * KV_WIDTH), lambda bi, i: (bi, i, 0)),
                   k_spec, vt_spec, k_spec, vt_spec],
        out_shape=[jax.ShapeDtypeStruct((b, KV_HEADS, Q_PER_KV, t, LANES), BF16),
                   leaf_shape, leaf_shape, leaf_shape,
                   jax.ShapeDtypeStruct((b, t // CMP_BLOCK, 2 * KV_WIDTH), F32),
                   k_shape, vt_shape, k_shape, vt_shape],
        compiler_params=_cparams(("parallel", "parallel")),
        name="prep_prompt",
    )(z, z, z, z, q_gain, ks_gain, kw_gain, wc, bd)


def _cmp_prep_kernel(cc_ref, kg_ref, bd_ref, kca_ref, vct_ref):
    cc = cc_ref[0]
    kc = _seg_rms(cc[:, :KV_WIDTH], bd_ref[...]) * kg_ref[...]
    for g, vt in enumerate(_heads_t(cc[:, KV_WIDTH:])):
        kca_ref[0, g] = _head_lo(kc, g).astype(BF16)
        vct_ref[0, g] = vt.astype(BF16)


def _cmp_prep(cc_perm, kc_gain, bd):
    b, n, _ = cc_perm.shape
    return pl.pallas_call(
        _cmp_prep_kernel,
        grid=(b,),
        in_specs=[pl.BlockSpec((1, n, 2 * KV_WIDTH), lambda bi: (bi, 0, 0)),
                  pl.BlockSpec((1, KV_WIDTH), lambda bi: (0, 0)),
                  pl.BlockSpec((MXU_DIM, MXU_DIM), lambda bi: (0, 0))],
        out_specs=[pl.BlockSpec((1, KV_HEADS, n, LANES), lambda bi: (bi, 0, 0, 0)),
                   pl.BlockSpec((1, KV_HEADS, HEAD_DIM, n), lambda bi: (bi, 0, 0, 0))],
        out_shape=[jax.ShapeDtypeStruct((b, KV_HEADS, n, LANES), BF16),
                   jax.ShapeDtypeStruct((b, KV_HEADS, HEAD_DIM, n), BF16)],
        compiler_params=_cparams(("parallel",)),
        name="cmp_prep",
    )(cc_perm, kc_gain, bd)


def _rank_select(score, n_keep):
    n, w = score.shape
    sub = lax.broadcasted_iota(jnp.int32, (8, w), 0)
    groups = [score[8 * k:8 * k + 8] for k in range(n // 8)]
    counts = [jnp.zeros((8, w), jnp.int32) for _ in groups]
    for i in range(n):
        si = jnp.broadcast_to(score[i:i + 1, :], (8, w))
        for k, blk in enumerate(groups):
            if 8 * k + 7 < i:
                inc = jnp.where(si > blk, 1, 0)
            elif 8 * k > i:
                inc = jnp.where(si >= blk, 1, 0)
            else:
                inc = jnp.where(sub > (i - 8 * k), jnp.where(si >= blk, 1, 0), jnp.where(si > blk, 1, 0))
            counts[k] = counts[k] + inc
    return jnp.concatenate(counts, axis=0) < n_keep


def _attn_prompt_kernel(qa_ref, kca_ref, vct_ref, kas_ref, vts_ref, kaw_ref, vtw_ref, gl_ref, o_ref, sig_ref, *, tq):
    i = pl.program_id(2)
    rows = Q_PER_KV * tq
    t0 = i * tq
    n_cmp = kca_ref.shape[2]
    n_selb = n_cmp // 2
    t_q = t0 + _lane_iota((1, rows)) % tq

    def prepare(gi):
        qa = qa_ref[0, gi].reshape(rows, LANES)
        s = _dot_nt(kca_ref[0, gi], qa)
        n_row = _row_iota((n_cmp, tq))
        cmp_id = 2 * (n_row % n_selb) + n_row // n_selb
        vis = (cmp_id + 1) * CMP_BLOCK - 1 <= t0 + _lane_iota((n_cmp, tq))
        vis = jnp.concatenate([vis] * Q_PER_KV, axis=1)
        s = jnp.where(vis, s, NEG)
        e = jnp.where(vis, jnp.exp2(s - jnp.max(s, axis=0, keepdims=True)), 0.0)
        p = e / jnp.maximum(jnp.sum(e, axis=0, keepdims=True), TINY)
        o_cmp = jnp.dot(vct_ref[0, gi], p.astype(BF16), preferred_element_type=F32)

        imp = jnp.zeros((n_selb, tq), F32)
        for r in range(Q_PER_KV):
            imp = imp + (p[:n_selb, r * tq:(r + 1) * tq] + p[n_selb:, r * tq:(r + 1) * tq])
        j_row = _row_iota((n_selb, tq))
        cur = (t0 + _lane_iota((n_selb, tq))) // SEL_BLOCK
        cand = j_row < cur
        forced = (j_row == 0) | (j_row == cur - 1)
        score = jnp.where(cand, jnp.where(forced, jnp.inf, imp), -jnp.inf)
        keep = (_rank_select(score, N_SEL - 1) & cand) | (j_row == cur)
        bias_t = jnp.where(keep, 0.0, SEL_BIAS)
        if n_selb < HEAD_DIM:
            bias_t = jnp.concatenate([bias_t, jnp.full((HEAD_DIM - n_selb, tq), SEL_BIAS, F32)], axis=0)
        bias = jnp.concatenate([jnp.zeros((HEAD_DIM, tq), F32), bias_t], axis=0).T
        bias4 = jnp.concatenate([bias] * Q_PER_KV, axis=0).astype(BF16)
        q_sel = jnp.where(_lane_iota((rows, LANES)) >= HEAD_DIM, bias4, qa)
        s = jnp.where(in_window, _dot_nt(k_rows(kaw_ref, gi, c_win, n_w), qa), NEG)
        o_win = normalised(partial_softmax(s, vtw_ref, gi, c_win)[1])
        return q_sel, o_cmp, o_win

    def partial_softmax(s, vt_ref, gi, c0):
        m = jnp.max(s, axis=0, keepdims=True)
        pb = jnp.exp2(s - m).astype(BF16)
        acc = None
        for j in range(s.shape[0] // tq):
            part = jnp.dot(vt_ref[0, gi, c0 + j], pb[j * tq:(j + 1) * tq], preferred_element_type=F32)
            acc = part if acc is None else acc + part
        return m, acc

    def merge(a, b):
        m = jnp.maximum(a[0], b[0])
        return m, a[1] * jnp.exp2(a[0] - m) + b[1] * jnp.exp2(b[0] - m)

    def normalised(acc):
        return acc[:HEAD_DIM] * (1.0 / acc[HEAD_DIM:HEAD_DIM + 1])

    def k_rows(ref, gi, c0, n):
        return ref[0, gi, pl.ds(pl.multiple_of(c0 * tq, tq), n * tq), :]

    n_w = WINDOW // tq + 1
    c_win = jnp.maximum(i - (n_w - 1), 0)
    age = t_q - (c_win * tq + _row_iota((n_w * tq, 1)))
    in_window = (age >= 0) & (age < WINDOW)
    groups = [prepare(gi) for gi in range(GROUPS_PER_STEP)]

    def sel_body(pair, carries):
        return tuple(
            merge(carries[gi], partial_softmax(_dot_nt(k_rows(kas_ref, gi, 2 * pair, 2), groups[gi][0]),
                                               vts_ref, gi, 2 * pair))
            for gi in range(GROUPS_PER_STEP))

    init = (jnp.full((1, rows), NEG, F32), jnp.zeros((VT_ROWS, rows), F32))
    carries = lax.fori_loop(0, i // 2, sel_body, (init,) * GROUPS_PER_STEP)
    c_sel = jnp.maximum(i - 1, 0)
    k_off = _row_iota((2 * tq, 1))
    done = (c_sel + k_off // tq < i) & (i % 2 == 0)
    k_pos_sel = jnp.where(done, t0 + 2 * tq, c_sel * tq + k_off)

    sig_ref[...] = jax.nn.sigmoid(gl_ref[...]).T
    for gi, (q_sel, o_cmp, o_win) in enumerate(groups):
        s = jnp.where(k_pos_sel <= t_q, _dot_nt(k_rows(kas_ref, gi, c_sel, 2), q_sel), NEG)
        o_sel = normalised(merge(carries[gi], partial_softmax(s, vts_ref, gi, c_sel))[1])
        group = pl.program_id(1) * GROUPS_PER_STEP + gi

        def gate(branch):
            r0 = branch * N_HEADS + group * Q_PER_KV
            return jnp.concatenate([sig_ref[pl.ds(r0 + r, 1), :] for r in range(Q_PER_KV)], axis=1)

        o_t = gate(0) * o_cmp + gate(1) * o_sel + gate(2) * o_win
        for h in range(0, Q_PER_KV, 2):
            pair_t = jnp.concatenate([o_t[:, h * tq:(h + 1) * tq], o_t[:, (h + 1) * tq:(h + 2) * tq]], axis=0)
            lane0 = (gi * Q_PER_KV + h) * HEAD_DIM
            o_ref[:, lane0:lane0 + LANES] = pair_t.T


def _attn_prompt(z, qa, kca, vct, kas, vts, kaw, vtw, b, t, tq):
    nt = t // tq
    n_cmp = kca.shape[2]
    gps = GROUPS_PER_STEP
    qspec = pl.BlockSpec((1, gps, Q_PER_KV, tq, LANES), lambda bi, g, i: (bi, g, 0, i, 0))
    kspec = pl.BlockSpec((1, gps, t, LANES), lambda bi, g, i: (bi, g, 0, 0))
    vspec = pl.BlockSpec((1, gps, nt, VT_ROWS, tq), lambda bi, g, i: (bi, g, 0, 0, 0))
    gspec = pl.BlockSpec((tq, LANES), lambda bi, g, i: (bi * nt + i, C_GNSA // LANES))
    return pl.pallas_call(
        functools.partial(_attn_prompt_kernel, tq=tq),
        grid=(b, KV_HEADS // gps, nt),
        in_specs=[qspec,
                  pl.BlockSpec((1, gps, n_cmp, LANES), lambda bi, g, i: (bi, g, 0, 0)),
                  pl.BlockSpec((1, gps, HEAD_DIM, n_cmp), lambda bi, g, i: (bi, g, 0, 0)),
                  kspec, vspec, kspec, vspec, gspec],
        out_specs=pl.BlockSpec((tq, gps * Q_PER_KV * HEAD_DIM), lambda bi, g, i: (bi * nt + i, g)),
        out_shape=jax.ShapeDtypeStruct((b * t, ATT_WIDTH), F32),
        scratch_shapes=[pltpu.VMEM((LANES, tq), F32)],
        compiler_params=_cparams(("parallel", "parallel", "arbitrary")),
        name="attn_prompt",
    )(qa, kca, vct, kas, vts, kaw, vtw, z)


def _rglru_kernel(x_ref, hist_ref, h0_ref, cw_ref, cb_ref, wr_ref, br_ref, wi_ref, bi_ref, lam_ref,
                  h_ref, hl_ref, cs_ref, carry_ref, tail_ref, a_ref, u_ref, *, tt, nb):
    i = pl.program_id(1)

    @pl.when(i == 0)
    def _():
        carry_ref[...] = h0_ref[...]
        tail_ref[...] = hist_ref[...]

    cw = cw_ref[...]
    nl = -lam_ref[...]
    softplus = jnp.maximum(nl, 0.0) + jnp.log1p(jnp.exp(-jnp.abs(nl)))
    for k in range(nb):
        x = x_ref[k]
        xe = jnp.concatenate([tail_ref[k], x], axis=0)
        xc = cb_ref[...]
        for j in range(CONV_W):
            lo = 8 - (CONV_W - 1) + j
            xc = xc + xe[lo:lo + tt] * cw[j:j + 1]
        xb = xc.astype(BF16)

        def gates(w_ref, b_ref):
            parts = [jnp.dot(xb[:, c * MXU_DIM:(c + 1) * MXU_DIM], w_ref[c], preferred_element_type=F32)
                     for c in range(D_RNN // MXU_DIM)]
            return jax.nn.sigmoid(jnp.concatenate(parts, axis=1) + b_ref[...])

        r = gates(wr_ref, br_ref)
        ig = gates(wi_ref, bi_ref)
        log_a = -LRU_C * r * softplus
        a_ref[k] = jnp.exp(log_a)
        th = jnp.tanh(log_a)
        u_ref[k] = jnp.sqrt(-2.0 * th / (1.0 - th)) * (ig * xc)
        tail_ref[k] = x[tt - 8:tt]

    def step(t, hs):
        out = []
        for k in range(nb):
            h = a_ref[k, pl.ds(t, 1), :] * hs[k] + u_ref[k, pl.ds(t, 1), :]
            h_ref[k, pl.ds(t, 1), :] = h
            out.append(h)
        return tuple(out)

    hs = lax.fori_loop(0, tt, step, tuple(carry_ref[k] for k in range(nb)), unroll=8)
    for k in range(nb):
        carry_ref[k] = hs[k]

    @pl.when(i == pl.num_programs(1) - 1)
    def _():
        hl_ref[...] = carry_ref[...]
        cs_ref[...] = tail_ref[...]


def _rglru(z, b, t, hist8, h0, conv_w, conv_b, wr_bd, b_rg, wi_bd, b_ig, lam, tt, nb):
    nt = t // tt

    def const(shape):
        return pl.BlockSpec(shape, lambda gb, i: tuple(0 for _ in shape))

    def per_seq(rows):
        return pl.BlockSpec((nb, rows, D_RNN), lambda gb, i: (gb, 0, 0))

    row = lambda v: v.reshape(1, D_RNN)
    h, h_last, conv_tail = pl.pallas_call(
        functools.partial(_rglru_kernel, tt=tt, nb=nb),
        grid=(b // nb, nt),
        in_specs=[pl.BlockSpec((nb, tt, D_RNN), lambda gb, i: (gb, i, C_XRNN // D_RNN)),
                  per_seq(8), per_seq(1),
                  const((CONV_W, D_RNN)), const((1, D_RNN)),
                  const((D_RNN // MXU_DIM, MXU_DIM, MXU_DIM)), const((1, D_RNN)),
                  const((D_RNN // MXU_DIM, MXU_DIM, MXU_DIM)), const((1, D_RNN)),
                  const((1, D_RNN))],
        out_specs=[pl.BlockSpec((nb, tt, D_RNN), lambda gb, i: (gb, i, 0)), per_seq(1), per_seq(8)],
        out_shape=[jax.ShapeDtypeStruct((b, t, D_RNN), F32),
                   jax.ShapeDtypeStruct((b, 1, D_RNN), F32),
                   jax.ShapeDtypeStruct((b, 8, D_RNN), F32)],
        scratch_shapes=[pltpu.VMEM((nb, 1, D_RNN), F32), pltpu.VMEM((nb, 8, D_RNN), F32),
                        pltpu.VMEM((nb, tt, D_RNN), F32), pltpu.VMEM((nb, tt, D_RNN), F32)],
        compiler_params=_cparams(("parallel", "arbitrary")),
        name="rglru",
    )(z.reshape(b, t, N_PACK), hist8, h0.reshape(b, 1, D_RNN), conv_w, row(conv_b),
      wr_bd, row(b_rg), wi_bd, row(b_ig), row(lam))
    return h.reshape(b * t, D_RNN), h_last, conv_tail


def _out_kernel(x_ref, oatt_ref, gatt_ref, h_ref, grnn_ref, gma_ref, gmr_ref, wa_ref, wr_ref, wo_ref, y_ref):
    a = (oatt_ref[...] * jax.nn.silu(gatt_ref[...])).astype(BF16)
    u_att = jnp.dot(a, wa_ref[...], preferred_element_type=F32)
    r = (h_ref[...] * jax.nn.silu(grnn_ref[...])).astype(BF16)
    u_rnn = jnp.dot(r, wr_ref[...], preferred_element_type=F32)
    m = jax.nn.sigmoid(gma_ref[...]) * u_att + jax.nn.sigmoid(gmr_ref[...]) * u_rnn
    y_ref[...] = x_ref[...] + jnp.dot(m.astype(BF16), wo_ref[...], preferred_element_type=F32)


def _out_proj(x2d, z, o_att, h_rnn, wa, wr, wo, tm):
    m = x2d.shape[0]

    def zspec(width, col):
        return pl.BlockSpec((tm, width), lambda i, c=col // width: (i, c))

    def wspec(shape):
        return pl.BlockSpec(shape, lambda i: (0, 0), pipeline_mode=pl.Buffered(1))

    rows = lambda width: pl.BlockSpec((tm, width), lambda i: (i, 0))
    return pl.pallas_call(
        _out_kernel,
        grid=(m // tm,),
        in_specs=[rows(D_MODEL), rows(ATT_WIDTH), zspec(ATT_WIDTH, C_GATT), rows(D_RNN), zspec(D_RNN, C_GRNN),
                  zspec(D_MODEL, C_GMATT), zspec(D_MODEL, C_GMRNN),
                  wspec((ATT_WIDTH, D_MODEL)), wspec((D_RNN, D_MODEL)), wspec((D_MODEL, D_MODEL))],
        out_specs=rows(D_MODEL),
        out_shape=jax.ShapeDtypeStruct((m, D_MODEL), F32),
        compiler_params=_cparams(("parallel",)),
        name="out_proj",
    )(x2d, o_att, z, h_rnn, z, z, z, wa, wr, wo)


def _prep_sample_kernel(zq_ref, zc_ref, zs_ref, zw_ref, qg_ref, ksg_ref, kwg_ref, bd_ref,
                        qn_ref, lc_ref, ls_ref, lw_ref):
    bd = bd_ref[...]
    qn_ref[...] = _seg_rms(zq_ref[...], bd) * qg_ref[...] * SCALE
    lc_ref[...] = zc_ref[...]
    for z_ref, g_ref, l_ref in ((zs_ref, ksg_ref, ls_ref), (zw_ref, kwg_ref, lw_ref)):
        z = z_ref[...]
        l_ref[:, :KV_WIDTH] = _seg_rms(z[:, :KV_WIDTH], bd) * g_ref[...]
        l_ref[:, KV_WIDTH:] = z[:, KV_WIDTH:]


def _prep_sample(z, q_gain, ks_gain, kw_gain, bd):
    m = z.shape[0]

    def zspec(width, col):
        return pl.BlockSpec((m, width), lambda i, c=col // width: (0, c))

    def const(shape):
        return pl.BlockSpec(shape, lambda i: tuple(0 for _ in shape))

    full = lambda width: pl.BlockSpec((m, width), lambda i: (0, 0))
    leaf = jax.ShapeDtypeStruct((m, 2 * KV_WIDTH), F32)
    return pl.pallas_call(
        _prep_sample_kernel,
        grid=(1,),
        in_specs=[zspec(ATT_WIDTH, C_Q), zspec(512, C_KV), zspec(512, C_KV + 512), zspec(512, C_KV + 1024),
                  const((1, ATT_WIDTH)), const((1, KV_WIDTH)), const((1, KV_WIDTH)), const((MXU_DIM, MXU_DIM))],
        out_specs=[full(ATT_WIDTH), full(512), full(512), full(512)],
        out_shape=[jax.ShapeDtypeStruct((m, ATT_WIDTH), F32), leaf, leaf, leaf],
        compiler_params=_cparams(("arbitrary",)),
        name="prep_sample",
    )(z, z, z, z, q_gain, ks_gain, kw_gain, bd)


def _page_specs(n):
    return [pl.BlockSpec((1, 2 * KV_WIDTH, PAGE_SIZE),
                         lambda bi, s, pt, k=k, n=n: (pt[bi, s * n + k], 0, 0)) for k in range(n)]


def _compress_pages_kernel(pt_ref, *refs):
    del pt_ref
    pages = refs[:CMP_PAGES]
    wt_ref, seg_ref, ok_ref, ov_ref = refs[CMP_PAGES:]
    wt = wt_ref[...]
    acc_k = jnp.zeros((KV_WIDTH, LANES), F32)
    acc_v = jnp.zeros((KV_WIDTH, LANES), F32)
    for pair in range(CMP_PAGES // 2):
        pa = pages[2 * pair][0] * wt
        pb = pages[2 * pair + 1][0] * wt
        seg = seg_ref[pair]
        lhs_k = jnp.concatenate([pa[:KV_WIDTH], pb[:KV_WIDTH]], axis=1).astype(BF16)
        lhs_v = jnp.concatenate([pa[KV_WIDTH:], pb[KV_WIDTH:]], axis=1).astype(BF16)
        acc_k = acc_k + jnp.dot(lhs_k, seg, preferred_element_type=F32)
        acc_v = acc_v + jnp.dot(lhs_v, seg, preferred_element_type=F32)
    ok_ref[0] = acc_k
    ov_ref[0] = acc_v


def _compress_pages(page_table, pool_t, wt, seg):
    b, n_pages = page_table.shape
    ns = n_pages // CMP_PAGES
    out_spec = pl.BlockSpec((1, KV_WIDTH, LANES), lambda bi, s, pt: (bi, 0, s))
    grid_spec = pltpu.PrefetchScalarGridSpec(
        num_scalar_prefetch=1,
        grid=(b, ns),
        in_specs=_page_specs(CMP_PAGES) + [
            pl.BlockSpec((2 * KV_WIDTH, PAGE_SIZE), lambda bi, s, pt: (0, 0)),
            pl.BlockSpec((CMP_PAGES // 2, 2 * PAGE_SIZE, LANES), lambda bi, s, pt: (0, 0, 0))],
        out_specs=[out_spec, out_spec],
    )
    shape = jax.ShapeDtypeStruct((b, KV_WIDTH, ns * LANES), F32)
    return pl.pallas_call(
        _compress_pages_kernel,
        grid_spec=grid_spec,
        out_shape=[shape, shape],
        compiler_params=_cparams(("parallel", "arbitrary")),
        name="compress_pages",
    )(page_table, *([pool_t] * CMP_PAGES), wt, seg)


def _softmax_lanes(parts, masks):
    parts = [jnp.where(mk, s, NEG) for s, mk in zip(parts, masks)]
    mx = functools.reduce(jnp.maximum, [jnp.max(s, axis=1, keepdims=True) for s in parts])
    es = [jnp.where(mk, jnp.exp(s - mx), 0.0) for s, mk in zip(parts, masks)]
    den = jnp.maximum(sum(jnp.sum(e, axis=1, keepdims=True) for e in es), TINY)
    return [e / den for e in es]


def _sample_cmp_win_kernel(qbd_ref, ck_ref, cv_ref, kg_ref, cwin_ref, wnew_ref,
                           ocmp_ref, owin_ref, bias_ref, wst_ref, *, past_len, t_new):
    qbd = qbd_ref[0]
    rows = qbd.shape[0]
    ck = ck_ref[0]
    n_cmp = ck.shape[1]
    parts = []
    for g in range(KV_HEADS):
        xs = ck[g * HEAD_DIM:(g + 1) * HEAD_DIM]
        parts.append(xs * lax.rsqrt(jnp.mean(xs * xs, axis=0, keepdims=True) + EPS))
    kc = (jnp.concatenate(parts, axis=0) * kg_ref[...]).astype(BF16)

    t_row = _row_iota((rows, n_cmp)) % t_new
    n_lane = _lane_iota((rows, n_cmp))
    cmp_id = LANES * (n_lane // LANES) + 2 * (n_lane % (LANES // 2)) + (n_lane % LANES) // (LANES // 2)
    vis = (cmp_id + 1) * CMP_BLOCK - 1 <= past_len + t_row
    (p,) = _softmax_lanes([jnp.dot(qbd, kc, preferred_element_type=F32)], [vis])
    ocmp_ref[0] = _dot_nt(p.astype(BF16), cv_ref[0].astype(BF16))

    gt = KV_HEADS * t_new
    psum = p[0:gt]
    for r in range(1, Q_PER_KV):
        psum = psum + p[r * gt:(r + 1) * gt]
    halves = []
    for c in range(n_cmp // LANES):
        tile = psum[:, c * LANES:(c + 1) * LANES]
        halves.append(tile + pltpu.roll(tile, LANES // 2, 1))
    if len(halves) == 1:
        imp = halves[0][:, :LANES // 2]
    else:
        low = _lane_iota((gt, LANES)) < LANES // 2
        imp = jnp.concatenate([jnp.where(low, halves[c], halves[c + 1]) for c in range(0, len(halves), 2)], axis=1)
    n_blk = n_cmp // 2
    j = _lane_iota((gt, n_blk))
    cur = (past_len + _row_iota((gt, n_blk)) % t_new) // SEL_BLOCK
    cand = j < cur
    forced = (j == 0) | (j == cur - 1)
    score = jnp.where(cand, jnp.where(forced, jnp.inf, imp), -jnp.inf)
    count = jnp.zeros((gt, n_blk), jnp.int32)
    for c in range(n_blk):
        col = score[:, c:c + 1]
        count = count + jnp.where(j > c, jnp.where(col >= score, 1, 0), jnp.where(col > score, 1, 0))
    keep = (count < N_SEL - 1) & cand
    bias_ref[0] = jnp.where(keep, 0.0, SEL_BIAS)

    cwin = cwin_ref[0]
    wb = cwin.shape[1]
    wnew = wnew_ref[0]
    wnew_p = jnp.concatenate([wnew, jnp.zeros((LANES - t_new, 2 * KV_WIDTH), F32)], axis=0)
    t_w = _row_iota((rows, wb)) % t_new
    idx = _lane_iota((rows, wb))
    ok_w = (idx <= wb + t_w) & (idx > wb + t_w - WINDOW)
    t_n = _row_iota((rows, LANES)) % t_new
    idx_n = wb + _lane_iota((rows, LANES))
    ok_n = (idx_n <= wb + t_n) & (idx_n > wb + t_n - WINDOW) & (_lane_iota((rows, LANES)) < t_new)
    p_w, p_n = _softmax_lanes([jnp.dot(qbd, cwin[:KV_WIDTH].astype(BF16), preferred_element_type=F32),
                               _dot_nt(qbd, wnew_p[:, :KV_WIDTH].astype(BF16))], [ok_w, ok_n])
    owin_ref[0] = (_dot_nt(p_w.astype(BF16), cwin[KV_WIDTH:].astype(BF16))
                   + jnp.dot(p_n.astype(BF16), wnew_p[:, KV_WIDTH:].astype(BF16), preferred_element_type=F32))

    keep_lanes = LANES - t_new
    new_t = jnp.concatenate([jnp.zeros((keep_lanes, 2 * KV_WIDTH), F32), wnew], axis=0).T
    rolled = [pltpu.roll(cwin[:, c * LANES:(c + 1) * LANES], keep_lanes, 1) for c in range(wb // LANES)]
    rolled.append(new_t)
    first = _lane_iota((2 * KV_WIDTH, LANES)) < keep_lanes
    for c in range(wb // LANES):
        wst_ref[0, :, c * LANES:(c + 1) * LANES] = jnp.where(first, rolled[c], rolled[c + 1])


def _sample_cmp_win(qbd, ck_t, cv_t, kc_gain_col, cwin_t, wnew, past_len):
    b, rows, _ = qbd.shape
    n_cmp = ck_t.shape[2]
    wb = cwin_t.shape[2]
    t_new = wnew.shape[1]
    gt = KV_HEADS * t_new

    def per_b(shape):
        return pl.BlockSpec((1,) + shape, lambda bi: (bi, 0, 0))

    return pl.pallas_call(
        functools.partial(_sample_cmp_win_kernel, past_len=past_len, t_new=t_new),
        grid=(b,),
        in_specs=[per_b((rows, MXU_DIM)), per_b((KV_WIDTH, n_cmp)), per_b((KV_WIDTH, n_cmp)),
                  pl.BlockSpec((KV_WIDTH, 1), lambda bi: (0, 0)),
                  per_b((2 * KV_WIDTH, wb)), per_b((t_new, 2 * KV_WIDTH))],
        out_specs=[per_b((rows, MXU_DIM)), per_b((rows, MXU_DIM)), per_b((gt, n_cmp // 2)), per_b((2 * KV_WIDTH, wb))],
        out_shape=[jax.ShapeDtypeStruct((b, rows, MXU_DIM), F32), jax.ShapeDtypeStruct((b, rows, MXU_DIM), F32),
                   jax.ShapeDtypeStruct((b, gt, n_cmp // 2), F32), jax.ShapeDtypeStruct((b, 2 * KV_WIDTH, wb), F32)],
        compiler_params=_cparams(("parallel",)),
        name="sample_cmp_win",
    )(qbd, ck_t, cv_t, kc_gain_col, cwin_t, wnew)


def _sample_sel_kernel(pt_ref, *refs, t_new):
    del pt_ref
    pages = refs[:SEL_PAGES]
    (qbd_ref, bsel_ref, onehot_ref, snew_ref, ocmp_ref, owin_ref, gl_ref,
     o_ref, m_ref, l_ref, acc_ref) = refs[SEL_PAGES:]
    s_idx = pl.program_id(1)
    qbd = qbd_ref[0]
    rows = qbd.shape[0]

    @pl.when(s_idx == 0)
    def _():
        m_ref[...] = jnp.full(m_ref.shape, NEG, F32)
        l_ref[...] = jnp.zeros(l_ref.shape, F32)
        acc_ref[...] = jnp.zeros(acc_ref.shape, F32)

    def partial_softmax(s, pv):
        m = jnp.max(s, axis=1, keepdims=True)
        pe = jnp.exp(s - m)
        return m, jnp.sum(pe, axis=1, keepdims=True), pv(pe.astype(BF16))

    def update(parts):
        m_old = m_ref[...]
        m_new = functools.reduce(jnp.maximum, [p[0] for p in parts], m_old)
        scale = jnp.exp(m_old - m_new)
        l = scale * l_ref[...]
        acc = scale * acc_ref[...]
        for m, l_part, acc_part in parts:
            c = jnp.exp(m - m_new)
            l = l + c * l_part
            acc = acc + c * acc_part
        m_ref[...] = m_new
        l_ref[...] = l
        acc_ref[...] = acc

    bsel = bsel_ref[0, 0]
    parts = []
    for g in range(SEL_PAGES // SEL_SUB):
        pgs = pages[g * SEL_SUB:(g + 1) * SEL_SUB]
        keys = SEL_SUB * PAGE_SIZE
        k_t = jnp.concatenate([pg[0, :KV_WIDTH, :].astype(BF16) for pg in pgs], axis=1)
        v_t = jnp.concatenate([pg[0, KV_WIDTH:, :].astype(BF16) for pg in pgs], axis=1)
        s = (jnp.dot(qbd, k_t, preferred_element_type=F32)
             + jnp.dot(bsel, onehot_ref[:, g * keys:(g + 1) * keys], preferred_element_type=F32))
        parts.append(partial_softmax(s, lambda pb, v_t=v_t: _dot_nt(pb, v_t)))
    update(parts)

    @pl.when(s_idx == pl.num_programs(1) - 1)
    def _():
        snew = jnp.concatenate([snew_ref[0], jnp.zeros((LANES - t_new, 2 * KV_WIDTH), F32)], axis=0)
        t_q = _row_iota((rows, LANES)) % t_new
        t_k = _lane_iota((rows, LANES))
        s = jnp.where((t_k <= t_q) & (t_k < t_new), _dot_nt(qbd, snew[:, :KV_WIDTH].astype(BF16)), NEG)
        v_new = snew[:, KV_WIDTH:].astype(BF16)
        update([partial_softmax(s, lambda pb: jnp.dot(pb, v_new, preferred_element_type=F32))])
        o_sel = acc_ref[...] / l_ref[...]
        sig = jax.nn.sigmoid(gl_ref[0])
        o_ref[0] = sig[:, 0:1] * ocmp_ref[0] + sig[:, 1:2] * o_sel + sig[:, 2:3] * owin_ref[0]


def _sample_sel(page_table, pool_t, qbd, bsel, onehot, snew, ocmp, owin, gl):
    b, n_pages = page_table.shape
    ns = n_pages // SEL_PAGES
    rows = qbd.shape[1]
    t_new = snew.shape[1]
    keys = SEL_PAGES * PAGE_SIZE

    def per_b(shape):
        return pl.BlockSpec((1,) + shape, lambda bi, s, pt: (bi,) + tuple(0 for _ in shape))

    grid_spec = pltpu.PrefetchScalarGridSpec(
        num_scalar_prefetch=1,
        grid=(b, ns),
        in_specs=_page_specs(SEL_PAGES) + [
            per_b((rows, MXU_DIM)),
            pl.BlockSpec((1, 1, rows, LANES), lambda bi, s, pt: (bi, s, 0, 0)),
            pl.BlockSpec((LANES, keys), lambda bi, s, pt: (0, 0)),
            per_b((t_new, 2 * KV_WIDTH)), per_b((rows, MXU_DIM)), per_b((rows, MXU_DIM)), per_b((rows, LANES))],
        out_specs=per_b((rows, MXU_DIM)),
        scratch_shapes=[pltpu.VMEM((rows, 1), F32), pltpu.VMEM((rows, 1), F32), pltpu.VMEM((rows, MXU_DIM), F32)],
    )
    return pl.pallas_call(
        functools.partial(_sample_sel_kernel, t_new=t_new),
        grid_spec=grid_spec,
        out_shape=jax.ShapeDtypeStruct((b, rows, MXU_DIM), F32),
        compiler_params=_cparams(("parallel", "arbitrary")),
        name="sample_sel",
    )(page_table, *([pool_t] * SEL_PAGES), qbd, bsel, onehot, snew, ocmp, owin, gl)


def _pack_weights(w_in, q_norm_g, k_norm_g, w_cmp, w_rg, w_ig, w_att_out, w_rnn_out, w_out):
    def block_diag(w):
        per = MXU_DIM // HEAD_DIM
        w4 = w.reshape(RNN_BLOCKS // per, per, HEAD_DIM, HEAD_DIM)
        eye = jnp.eye(per, dtype=w.dtype)
        return jnp.einsum('cpde,pq->cpdqe', w4, eye).reshape(RNN_BLOCKS // per, MXU_DIM, MXU_DIM).astype(BF16)

    wt = jnp.broadcast_to(w_cmp.transpose(1, 2, 0)[:, None], (2, KV_HEADS, HEAD_DIM, CMP_BLOCK))
    wt = jnp.tile(wt, (1, 1, 1, PAGE_SIZE // CMP_BLOCK)).reshape(2 * KV_WIDTH, PAGE_SIZE)
    lane = jnp.arange(2 * PAGE_SIZE)
    pair = jnp.arange(CMP_PAGES // 2)
    n_local = ((PAGE_SIZE // CMP_BLOCK) * (2 * pair[:, None] + lane[None, :] // PAGE_SIZE)
               + (lane[None, :] % PAGE_SIZE) // CMP_BLOCK)
    col = (n_local % 2) * (LANES // 2) + n_local // 2
    seg = (jnp.arange(LANES)[None, None, :] == col[:, :, None]).astype(BF16)

    return dict(
        w_t=w_in.T,
        q_gain=jnp.tile(q_norm_g, N_HEADS).reshape(1, ATT_WIDTH),
        kc_gain=jnp.tile(k_norm_g[0], KV_HEADS).reshape(1, KV_WIDTH),
        ks_gain=jnp.tile(k_norm_g[1], KV_HEADS).reshape(1, KV_WIDTH),
        kw_gain=jnp.tile(k_norm_g[2], KV_HEADS).reshape(1, KV_WIDTH),
        wc=jnp.broadcast_to(w_cmp[:, :, None, :], (CMP_BLOCK, 2, KV_HEADS, HEAD_DIM)).reshape(CMP_BLOCK, 2 * KV_WIDTH),
        wt=wt, seg=seg,
        wr_bd=block_diag(w_rg), wi_bd=block_diag(w_ig),
        wa=w_att_out.astype(BF16), wr=w_rnn_out.astype(BF16), wo=w_out.astype(BF16),
        bd=_block_diag_ones(),
    )


def _prompt_layer(x, pw, norm_g, conv_w, conv_b, b_rg, b_ig, lam):
    b, t, _ = x.shape
    tq = WINDOW // 2
    x2d = x.reshape(b * t, D_MODEL)
    z = _proj(x2d, norm_g, pw['w_t'], tm=min(1024, b * t))
    qa, leaf_c, leaf_s, leaf_w, cc, kas, vts, kaw, vtw = _prep_prompt(
        z, b, t, pw['q_gain'], pw['ks_gain'], pw['kw_gain'], pw['wc'], pw['bd'], te=tq)
    n_cmp = t // CMP_BLOCK
    cc_perm = cc.reshape(b, n_cmp // 2, 2, 2 * KV_WIDTH).transpose(0, 2, 1, 3).reshape(b, n_cmp, 2 * KV_WIDTH)
    kca, vct = _cmp_prep(cc_perm, pw['kc_gain'], pw['bd'])
    o_att = _attn_prompt(z, qa, kca, vct, kas, vts, kaw, vtw, b, t, tq)
    h_rnn, h_last, conv_tail = _rglru(z, b, t, jnp.zeros((b, 8, D_RNN), F32), jnp.zeros((b, D_RNN), F32),
                                      conv_w, conv_b, pw['wr_bd'], b_rg, pw['wi_bd'], b_ig, lam, tt=256, nb=b)
    y = _out_proj(x2d, z, o_att, h_rnn, pw['wa'], pw['wr'], pw['wo'], tm=256)
    def kv(leaf_t):
        tokens = leaf_t.shape[2]
        return leaf_t.reshape(b, 2, KV_HEADS, HEAD_DIM, tokens).transpose(0, 4, 1, 2, 3)

    w_keep = min(WINDOW, t)
    return y.reshape(b, t, D_MODEL), (kv(leaf_c), kv(leaf_s), kv(leaf_w[:, :, t - w_keep:]),
                                      h_last.reshape(b, D_RNN), conv_tail[:, 8 - (CONV_W - 1):])


def _feature_major(cache):
    n, tokens = cache.shape[:2]
    return cache.transpose(0, 2, 3, 4, 1).reshape(n, 2 * KV_WIDTH, tokens)


def _sample_layer(x, cache_cmp, cache_sel, cache_win, state_h, state_conv, page_table, pw,
                  norm_g, conv_w, conv_b, b_rg, b_ig, lam):
    b, t, _ = x.shape
    n_pages = page_table.shape[1]
    past_len = n_pages * PAGE_SIZE
    rows = N_HEADS * t
    x2d = x.reshape(b * t, D_MODEL)
    z = _proj(x2d, norm_g, pw['w_t'], tm=b * t)
    qn, leaf_c, leaf_s, leaf_w = _prep_sample(z, pw['q_gain'], pw['ks_gain'], pw['kw_gain'], pw['bd'])

    q5 = qn.reshape(b, t, KV_HEADS, Q_PER_KV, HEAD_DIM).transpose(0, 3, 2, 1, 4)
    qbd = (q5[:, :, :, :, None, :] * jnp.eye(KV_HEADS, dtype=F32)[None, None, :, None, :, None])
    qbd = qbd.reshape(b, rows, KV_WIDTH).astype(BF16)

    ck_t, cv_t = _compress_pages(page_table, _feature_major(cache_cmp), pw['wt'], pw['seg'])
    wb = cache_win.shape[1]
    o_cmp, o_win, bias, win_state_t = _sample_cmp_win(
        qbd, ck_t, cv_t, pw['kc_gain'].reshape(KV_WIDTH, 1), _feature_major(cache_win),
        leaf_w.reshape(b, t, 2 * KV_WIDTH), past_len)
    win_state = win_state_t.reshape(b, 2, KV_HEADS, HEAD_DIM, wb).transpose(0, 4, 1, 2, 3)

    ns = n_pages // SEL_PAGES
    blocks_per_step = SEL_PAGES * PAGE_SIZE // SEL_BLOCK
    bsel = bias.reshape(b, KV_HEADS * t, ns, blocks_per_step).transpose(0, 2, 1, 3)
    bsel = jnp.tile(bsel, (1, 1, Q_PER_KV, 1))
    bsel = jnp.pad(bsel, ((0, 0), (0, 0), (0, 0), (0, LANES - blocks_per_step))).astype(BF16)
    key_block = jnp.arange(SEL_PAGES * PAGE_SIZE) // SEL_BLOCK
    onehot = (jnp.arange(LANES)[:, None] == key_block[None, :]).astype(BF16)
    gl = z[:, C_GNSA:C_GNSA + 3 * N_HEADS].reshape(b, t, 3, KV_HEADS, Q_PER_KV)
    gl = jnp.pad(gl.transpose(0, 4, 3, 1, 2).reshape(b, rows, 3), ((0, 0), (0, 0), (0, LANES - 3)))
    o_full = _sample_sel(page_table, _feature_major(cache_sel), qbd, bsel, onehot,
                         leaf_s.reshape(b, t, 2 * KV_WIDTH), o_cmp, o_win, gl)
    o6 = o_full.reshape(b, Q_PER_KV, KV_HEADS, t, KV_HEADS, HEAD_DIM)
    o_att = jnp.stack([o6[:, :, g, :, g, :] for g in range(KV_HEADS)], axis=2)
    o_att = o_att.transpose(0, 3, 2, 1, 4).reshape(b * t, ATT_WIDTH)

    hist8 = jnp.pad(state_conv, ((0, 0), (8 - (CONV_W - 1), 0), (0, 0)))
    h_rnn, h_last, conv_tail = _rglru(z, b, t, hist8, state_h, conv_w, conv_b,
                                      pw['wr_bd'], b_rg, pw['wi_bd'], b_ig, lam, tt=t,
                                      nb=8 if b % 8 == 0 else b)
    y = _out_proj(x2d, z, o_att, h_rnn, pw['wa'], pw['wr'], pw['wo'], tm=min(256, b * t))
    kv = lambda leaf: leaf.reshape(b, -1, 2, KV_HEADS, HEAD_DIM)
    return y.reshape(b, t, D_MODEL), (kv(leaf_c), kv(leaf_s), win_state,
                                      h_last.reshape(b, D_RNN), conv_tail[:, 8 - (CONV_W - 1):])


def kernel(x_prompt, x_sample, cache_cmp, cache_sel, cache_win, state_h, state_conv, page_table,
           norm_g, w_in, q_norm_g, k_norm_g, w_cmp, conv_w, conv_b, w_rg, b_rg, w_ig, b_ig,
           lru_lambda, w_att_out, w_rnn_out, w_out):
    yp, ys = x_prompt, x_sample
    outs_p, outs_s = [], []
    for l in range(w_in.shape[0]):
        pw = _pack_weights(w_in[l], q_norm_g[l], k_norm_g[l], w_cmp[l], w_rg[l], w_ig[l],
                           w_att_out[l], w_rnn_out[l], w_out[l])
        yp, st_p = _prompt_layer(yp, pw, norm_g[l], conv_w[l], conv_b[l], b_rg[l], b_ig[l], lru_lambda[l])
        ys, st_s = _sample_layer(ys, cache_cmp[l], cache_sel[l], cache_win[l], state_h[l], state_conv[l],
                                 page_table, pw, norm_g[l], conv_w[l], conv_b[l], b_rg[l], b_ig[l], lru_lambda[l])
        outs_p.append(st_p)
        outs_s.append(st_s)
    cmp_p, sel_p, win_p, h_p, conv_p = [jnp.stack(a) for a in zip(*outs_p)]
    cmp_s, sel_s, win_s, h_s, conv_s = [jnp.stack(a) for a in zip(*outs_s)]
    return (yp, ys, cmp_p, sel_p, win_p, h_p, conv_p, cmp_s, sel_s, win_s, h_s, conv_s)
```

```python
import functools

import jax
import jax.numpy as jnp
from jax import lax
from jax.experimental import pallas as pl
from jax.experimental.pallas import tpu as pltpu

F32 = jnp.float32
BF16 = jnp.bfloat16

D_MODEL = 2048
N_HEADS = 16
HEAD_DIM = 64
KV_HEADS = 4
Q_PER_KV = N_HEADS // KV_HEADS
ATT_WIDTH = N_HEADS * HEAD_DIM
KV_WIDTH = KV_HEADS * HEAD_DIM
CMP_BLOCK = 32
SEL_BLOCK = 64
N_SEL = 16
WINDOW = 512
SCALE = HEAD_DIM ** -0.5
LOG2E = 1.4426950408889634
D_RNN = 1024
RNN_BLOCKS = 16
CONV_W = 4
LRU_C = 8.0
PAGE_SIZE = 128
EPS = 1e-6
NEG = -1e30
TINY = 1e-30
SEL_BIAS = -30000.0

LANES = 128
MXU_DIM = 256
VMEM_LIMIT = 56 * 1024 * 1024

C_Q = 0
C_GATT = 1024
C_XRNN = 2048
C_GRNN = 3072
C_GMATT = 4096
C_GMRNN = 6144
C_KV = 8192
C_GNSA = 9728
N_PACK = 10240
PROJ_TILE = 1024
_W_Q, _W_KV, _W_GNSA, _W_GATT, _W_XRNN, _W_GRNN, _W_GMATT, _W_GMRNN = 0, 1024, 2560, 2608, 3632, 4656, 5680, 7728
PROJ_ROW_STARTS = (_W_Q, _W_GATT, _W_XRNN, _W_GRNN, _W_GMATT, _W_GMATT + 1024, _W_GMRNN, _W_GMRNN + 1024,
                   _W_KV, _W_KV + 1024)

SEL_PAGES = 64
SEL_SUB = 64
CMP_PAGES = 64
CMP_TILE_PAGES = 32
GROUPS_PER_STEP = 2
VT_ROWS = 80


def _cparams(sem):
    return pltpu.CompilerParams(dimension_semantics=sem, vmem_limit_bytes=VMEM_LIMIT)


def _proj_kernel(starts_ref, x_ref, g_ref, w_ref, o_ref, xn_ref):
    del starts_ref
    @pl.when(pl.program_id(1) == 0)
    def _():
        x = x_ref[...]
        ms = jnp.mean(x * x, axis=-1, keepdims=True)
        xn_ref[...] = (x * lax.rsqrt(ms + EPS) * g_ref[...]).astype(BF16)

    o_ref[...] = lax.dot_general(xn_ref[...], w_ref[...].astype(BF16), (((1,), (1,)), ((), ())),
                                 preferred_element_type=F32)


def _proj(x2d, norm_g, w_t, tm):
    m = x2d.shape[0]
    tn = PROJ_TILE
    grid_spec = pltpu.PrefetchScalarGridSpec(
        num_scalar_prefetch=1,
        grid=(m // tm, N_PACK // tn),
        in_specs=[
            pl.BlockSpec((tm, D_MODEL), lambda i, j, st: (i, 0)),
            pl.BlockSpec((1, D_MODEL), lambda i, j, st: (0, 0)),
            pl.BlockSpec((pl.Element(tn), pl.Element(D_MODEL)), lambda i, j, st: (pl.multiple_of(st[j], 16), 0)),
        ],
        out_specs=pl.BlockSpec((tm, tn), lambda i, j, st: (i, j)),
        scratch_shapes=[pltpu.VMEM((tm, D_MODEL), BF16)],
    )
    return pl.pallas_call(
        _proj_kernel,
        grid_spec=grid_spec,
        out_shape=jax.ShapeDtypeStruct((m, N_PACK), F32),
        compiler_params=_cparams(("parallel", "arbitrary")),
        name="proj",
    )(jnp.asarray(PROJ_ROW_STARTS, jnp.int32), x2d, norm_g.reshape(1, D_MODEL), w_t)


def _lane_iota(shape):
    return lax.broadcasted_iota(jnp.int32, shape, len(shape) - 1)


def _row_iota(shape):
    return lax.broadcasted_iota(jnp.int32, shape, 0)


def _seg_rms(x, bd):
    outs = []
    for c in range(x.shape[1] // MXU_DIM):
        xc = x[:, c * MXU_DIM:(c + 1) * MXU_DIM]
        x2 = xc * xc
        hi = x2.astype(BF16)
        lo = (x2 - hi.astype(F32)).astype(BF16)
        ss = (jnp.dot(hi, bd, preferred_element_type=F32)
              + jnp.dot(lo, bd, preferred_element_type=F32))
        outs.append(xc * lax.rsqrt(ss * (1.0 / HEAD_DIM) + EPS))
    return outs[0] if len(outs) == 1 else jnp.concatenate(outs, axis=1)


def _head_lo(x, h):
    tile = x[:, (h // 2) * LANES:(h // 2 + 1) * LANES]
    if h % 2:
        tile = pltpu.roll(tile, HEAD_DIM, 1)
    return jnp.where(_lane_iota(tile.shape) < HEAD_DIM, tile, 0.0)


def _heads_t(x):
    outs = []
    for c in range(x.shape[1] // LANES):
        xt = x[:, c * LANES:(c + 1) * LANES].T
        outs += [xt[:HEAD_DIM], xt[HEAD_DIM:]]
    return outs


def _dot_nt(a, b):
    return lax.dot_general(a, b, (((1,), (1,)), ((), ())), preferred_element_type=F32)


def _block_diag_ones():
    r = jnp.arange(MXU_DIM) // HEAD_DIM
    return (r[:, None] == r[None, :]).astype(BF16)


def _prep_prompt_kernel(zq_ref, zc_ref, zs_ref, zw_ref, qg_ref, ksg_ref, kwg_ref, wc_ref, bd_ref,
                        qa_ref, lc_ref, ls_ref, lw_ref, cc_ref, kas_ref, vts_ref, kaw_ref, vtw_ref, *, te):
    i = pl.program_id(1)
    bd = bd_ref[...]
    qn = _seg_rms(zq_ref[...], bd) * qg_ref[...] * (SCALE * LOG2E)
    for h in range(N_HEADS):
        qa_ref[0, h // Q_PER_KV, h % Q_PER_KV] = _head_lo(qn, h).astype(BF16)

    zc = zc_ref[...]
    for c in range(2 * KV_WIDTH // LANES):
        lc_ref[0, c * LANES:(c + 1) * LANES, :] = zc[:, c * LANES:(c + 1) * LANES].T
    cc_ref[0] = jnp.sum(zc.reshape(te // CMP_BLOCK, CMP_BLOCK, 2 * KV_WIDTH) * wc_ref[...][None], axis=1)

    lane = _lane_iota((te, LANES))
    own_block = (i * te + _row_iota((te, LANES))) // SEL_BLOCK
    onehot = jnp.where(lane - HEAD_DIM == own_block, 1.0, 0.0)
    ones_rows = jnp.where(_row_iota((VT_ROWS - HEAD_DIM, te)) == 0, 1.0, 0.0).astype(BF16)

    for z_ref, g_ref, l_ref, ka_ref, vt_ref, with_onehot in (
            (zs_ref, ksg_ref, ls_ref, kas_ref, vts_ref, True),
            (zw_ref, kwg_ref, lw_ref, kaw_ref, vtw_ref, False)):
        z = z_ref[...]
        kn = _seg_rms(z[:, :KV_WIDTH], bd) * g_ref[...]
        v = z[:, KV_WIDTH:]
        for c in range(KV_WIDTH // LANES):
            l_ref[0, c * LANES:(c + 1) * LANES, :] = kn[:, c * LANES:(c + 1) * LANES].T
        for g, vt in enumerate(_heads_t(v)):
            l_ref[0, KV_WIDTH + g * HEAD_DIM:KV_WIDTH + (g + 1) * HEAD_DIM, :] = vt
            ka = _head_lo(kn, g)
            if with_onehot:
                ka = jnp.where(lane < HEAD_DIM, ka, onehot)
            ka_ref[0, g] = ka.astype(BF16)
            vt_ref[0, g, 0] = jnp.concatenate([vt.astype(BF16), ones_rows], axis=0)


def _prep_prompt(z, b, t, q_gain, ks_gain, kw_gain, wc, bd, te):
    nt = t // te

    def zspec(width, col):
        return pl.BlockSpec((te, width), lambda bi, i, c=col // width: (bi * nt + i, c))

    def const(shape):
        return pl.BlockSpec(shape, lambda bi, i: tuple(0 for _ in shape))

    leaf_spec = pl.BlockSpec((1, 2 * KV_WIDTH, te), lambda bi, i: (bi, 0, i))
    leaf_shape = jax.ShapeDtypeStruct((b, 2 * KV_WIDTH, t), F32)
    k_spec = pl.BlockSpec((1, KV_HEADS, te, LANES), lambda bi, i: (bi, 0, i, 0))
    k_shape = jax.ShapeDtypeStruct((b, KV_HEADS, t, LANES), BF16)
    vt_spec = pl.BlockSpec((1, KV_HEADS, 1, VT_ROWS, te), lambda bi, i: (bi, 0, i, 0, 0))
    vt_shape = jax.ShapeDtypeStruct((b, KV_HEADS, nt, VT_ROWS, te), BF16)
    return pl.pallas_call(
        functools.partial(_prep_prompt_kernel, te=te),
        grid=(b, nt),
        in_specs=[zspec(ATT_WIDTH, C_Q), zspec(512, C_KV), zspec(512, C_KV + 512), zspec(512, C_KV + 1024),
                  const((1, ATT_WIDTH)), const((1, KV_WIDTH)), const((1, KV_WIDTH)),
                  const((CMP_BLOCK, 2 * KV_WIDTH)), const((MXU_DIM, MXU_DIM))],
        out_specs=[pl.BlockSpec((1, KV_HEADS, Q_PER_KV, te, LANES), lambda bi, i: (bi, 0, 0, i, 0)),
                   leaf_spec, leaf_spec, leaf_spec,
                   pl.BlockSpec((1, te // CMP_BLOCK, 2 * KV_WIDTH), lambda bi, i: (bi, i, 0)),
                   k_spec, vt_spec, k_spec, vt_spec],
        out_shape=[jax.ShapeDtypeStruct((b, KV_HEADS, Q_PER_KV, t, LANES), BF16),
                   leaf_shape, leaf_shape, leaf_shape,
                   jax.ShapeDtypeStruct((b, t // CMP_BLOCK, 2 * KV_WIDTH), F32),
                   k_shape, vt_shape, k_shape, vt_shape],
        compiler_params=_cparams(("parallel", "parallel")),
        name="prep_prompt",
    )(z, z, z, z, q_gain, ks_gain, kw_gain, wc, bd)


def _cmp_prep_kernel(cc_ref, kg_ref, bd_ref, kca_ref, vct_ref):
    cc = cc_ref[0]
    kc = _seg_rms(cc[:, :KV_WIDTH], bd_ref[...]) * kg_ref[...]
    for g, vt in enumerate(_heads_t(cc[:, KV_WIDTH:])):
        kca_ref[0, g] = _head_lo(kc, g).astype(BF16)
        vct_ref[0, g] = vt.astype(BF16)


def _cmp_prep(cc_perm, kc_gain, bd):
    b, n, _ = cc_perm.shape
    return pl.pallas_call(
        _cmp_prep_kernel,
        grid=(b,),
        in_specs=[pl.BlockSpec((1, n, 2 * KV_WIDTH), lambda bi: (bi, 0, 0)),
                  pl.BlockSpec((1, KV_WIDTH), lambda bi: (0, 0)),
                  pl.BlockSpec((MXU_DIM, MXU_DIM), lambda bi: (0, 0))],
        out_specs=[pl.BlockSpec((1, KV_HEADS, n, LANES), lambda bi: (bi, 0, 0, 0)),
                   pl.BlockSpec((1, KV_HEADS, HEAD_DIM, n), lambda bi: (bi, 0, 0, 0))],
        out_shape=[jax.ShapeDtypeStruct((b, KV_HEADS, n, LANES), BF16),
                   jax.ShapeDtypeStruct((b, KV_HEADS, HEAD_DIM, n), BF16)],
        compiler_params=_cparams(("parallel",)),
        name="cmp_prep",
    )(cc_perm, kc_gain, bd)


def _rank_select(score, n_keep):
    n, w = score.shape
    sub = lax.broadcasted_iota(jnp.int32, (8, w), 0)
    groups = [score[8 * k:8 * k + 8] for k in range(n // 8)]
    counts = [jnp.zeros((8, w), jnp.int32) for _ in groups]
    for i in range(n):
        si = jnp.broadcast_to(score[i:i + 1, :], (8, w))
        for k, blk in enumerate(groups):
            if 8 * k + 7 < i:
                inc = jnp.where(si > blk, 1, 0)
            elif 8 * k > i:
                inc = jnp.where(si >= blk, 1, 0)
            else:
                inc = jnp.where(sub > (i - 8 * k), jnp.where(si >= blk, 1, 0), jnp.where(si > blk, 1, 0))
            counts[k] = counts[k] + inc
    return jnp.concatenate(counts, axis=0) < n_keep


def _attn_prompt_kernel(qa_ref, kca_ref, vct_ref, kas_ref, vts_ref, kaw_ref, vtw_ref, gl_ref, o_ref, sig_ref, *, tq):
    i = pl.program_id(2)
    rows = Q_PER_KV * tq
    t0 = i * tq
    n_cmp = kca_ref.shape[2]
    n_selb = n_cmp // 2
    t_q = t0 + _lane_iota((1, rows)) % tq

    def prepare(gi):
        qa = qa_ref[0, gi].reshape(rows, LANES)
        s = _dot_nt(kca_ref[0, gi], qa)
        n_row = _row_iota((n_cmp, tq))
        cmp_id = 2 * (n_row % n_selb) + n_row // n_selb
        vis = (cmp_id + 1) * CMP_BLOCK - 1 <= t0 + _lane_iota((n_cmp, tq))
        vis = jnp.concatenate([vis] * Q_PER_KV, axis=1)
        s = jnp.where(vis, s, NEG)
        e = jnp.where(vis, jnp.exp2(s - jnp.max(s, axis=0, keepdims=True)), 0.0)
        p = e / jnp.maximum(jnp.sum(e, axis=0, keepdims=True), TINY)
        o_cmp = jnp.dot(vct_ref[0, gi], p.astype(BF16), preferred_element_type=F32)

        imp = jnp.zeros((n_selb, tq), F32)
        for r in range(Q_PER_KV):
            imp = imp + (p[:n_selb, r * tq:(r + 1) * tq] + p[n_selb:, r * tq:(r + 1) * tq])
        j_row = _row_iota((n_selb, tq))
        cur = (t0 + _lane_iota((n_selb, tq))) // SEL_BLOCK
        cand = j_row < cur
        forced = (j_row == 0) | (j_row == cur - 1)
        score = jnp.where(cand, jnp.where(forced, jnp.inf, imp), -jnp.inf)
        keep = (_rank_select(score, N_SEL - 1) & cand) | (j_row == cur)
        bias_t = jnp.where(keep, 0.0, SEL_BIAS)
        if n_selb < HEAD_DIM:
            bias_t = jnp.concatenate([bias_t, jnp.full((HEAD_DIM - n_selb, tq), SEL_BIAS, F32)], axis=0)
        bias = jnp.concatenate([jnp.zeros((HEAD_DIM, tq), F32), bias_t], axis=0).T
        bias4 = jnp.concatenate([bias] * Q_PER_KV, axis=0).astype(BF16)
        q_sel = jnp.where(_lane_iota((rows, LANES)) >= HEAD_DIM, bias4, qa)
        s = jnp.where(in_window, _dot_nt(k_rows(kaw_ref, gi, c_win, n_w), qa), NEG)
        o_win = normalised(partial_softmax(s, vtw_ref, gi, c_win)[1])
        return q_sel, o_cmp, o_win

    def partial_softmax(s, vt_ref, gi, c0):
        m = jnp.max(s, axis=0, keepdims=True)
        pb = jnp.exp2(s - m).astype(BF16)
        acc = None
        for j in range(s.shape[0] // tq):
            part = jnp.dot(vt_ref[0, gi, c0 + j], pb[j * tq:(j + 1) * tq], preferred_element_type=F32)
            acc = part if acc is None else acc + part
        return m, acc

    def merge(a, b):
        m = jnp.maximum(a[0], b[0])
        return m, a[1] * jnp.exp2(a[0] - m) + b[1] * jnp.exp2(b[0] - m)

    def normalised(acc):
        return acc[:HEAD_DIM] * (1.0 / acc[HEAD_DIM:HEAD_DIM + 1])

    def k_rows(ref, gi, c0, n):
        return ref[0, gi, pl.ds(pl.multiple_of(c0 * tq, tq), n * tq), :]

    n_w = WINDOW // tq + 1
    c_win = jnp.maximum(i - (n_w - 1), 0)
    age = t_q - (c_win * tq + _row_iota((n_w * tq, 1)))
    in_window = (age >= 0) & (age < WINDOW)
    groups = [prepare(gi) for gi in range(GROUPS_PER_STEP)]

    def sel_body(pair, carries):
        return tuple(
            merge(carries[gi], partial_softmax(_dot_nt(k_rows(kas_ref, gi, 2 * pair, 2), groups[gi][0]),
                                               vts_ref, gi, 2 * pair))
            for gi in range(GROUPS_PER_STEP))

    init = (jnp.full((1, rows), NEG, F32), jnp.zeros((VT_ROWS, rows), F32))
    carries = lax.fori_loop(0, i // 2, sel_body, (init,) * GROUPS_PER_STEP)
    c_sel = jnp.maximum(i - 1, 0)
    k_off = _row_iota((2 * tq, 1))
    done = (c_sel + k_off // tq < i) & (i % 2 == 0)
    k_pos_sel = jnp.where(done, t0 + 2 * tq, c_sel * tq + k_off)

    sig_ref[...] = jax.nn.sigmoid(gl_ref[...]).T
    for gi, (q_sel, o_cmp, o_win) in enumerate(groups):
        s = jnp.where(k_pos_sel <= t_q, _dot_nt(k_rows(kas_ref, gi, c_sel, 2), q_sel), NEG)
        o_sel = normalised(merge(carries[gi], partial_softmax(s, vts_ref, gi, c_sel))[1])
        group = pl.program_id(1) * GROUPS_PER_STEP + gi

        def gate(branch):
            r0 = branch * N_HEADS + group * Q_PER_KV
            return jnp.concatenate([sig_ref[pl.ds(r0 + r, 1), :] for r in range(Q_PER_KV)], axis=1)

        o_t = gate(0) * o_cmp + gate(1) * o_sel + gate(2) * o_win
        for h in range(0, Q_PER_KV, 2):
            pair_t = jnp.concatenate([o_t[:, h * tq:(h + 1) * tq], o_t[:, (h + 1) * tq:(h + 2) * tq]], axis=0)
            lane0 = (gi * Q_PER_KV + h) * HEAD_DIM
            o_ref[:, lane0:lane0 + LANES] = pair_t.T


def _attn_prompt(z, qa, kca, vct, kas, vts, kaw, vtw, b, t, tq):
    nt = t // tq
    n_cmp = kca.shape[2]
    gps = GROUPS_PER_STEP
    qspec = pl.BlockSpec((1, gps, Q_PER_KV, tq, LANES), lambda bi, g, i: (bi, g, 0, i, 0))
    kspec = pl.BlockSpec((1, gps, t, LANES), lambda bi, g, i: (bi, g, 0, 0))
    vspec = pl.BlockSpec((1, gps, nt, VT_ROWS, tq), lambda bi, g, i: (bi, g, 0, 0, 0))
    gspec = pl.BlockSpec((tq, LANES), lambda bi, g, i: (bi * nt + i, C_GNSA // LANES))
    return pl.pallas_call(
        functools.partial(_attn_prompt_kernel, tq=tq),
        grid=(b, KV_HEADS // gps, nt),
        in_specs=[qspec,
                  pl.BlockSpec((1, gps, n_cmp, LANES), lambda bi, g, i: (bi, g, 0, 0)),
                  pl.BlockSpec((1, gps, HEAD_DIM, n_cmp), lambda bi, g, i: (bi, g, 0, 0)),
                  kspec, vspec, kspec, vspec, gspec],
        out_specs=pl.BlockSpec((tq, gps * Q_PER_KV * HEAD_DIM), lambda bi, g, i: (bi * nt + i, g)),
        out_shape=jax.ShapeDtypeStruct((b * t, ATT_WIDTH), F32),
        scratch_shapes=[pltpu.VMEM((LANES, tq), F32)],
        compiler_params=_cparams(("parallel", "parallel", "arbitrary")),
        name="attn_prompt",
    )(qa, kca, vct, kas, vts, kaw, vtw, z)


def _rglru_kernel(x_ref, hist_ref, h0_ref, cw_ref, cb_ref, wr_ref, br_ref, wi_ref, bi_ref, lam_ref,
                  h_ref, hl_ref, cs_ref, carry_ref, tail_ref, a_ref, u_ref, *, tt, nb):
    i = pl.program_id(1)

    @pl.when(i == 0)
    def _():
        carry_ref[...] = h0_ref[...]
        tail_ref[...] = hist_ref[...]

    cw = cw_ref[...]
    nl = -lam_ref[...]
    softplus = jnp.maximum(nl, 0.0) + jnp.log1p(jnp.exp(-jnp.abs(nl)))
    for k in range(nb):
        x = x_ref[k]
        xe = jnp.concatenate([tail_ref[k], x], axis=0)
        xc = cb_ref[...]
        for j in range(CONV_W):
            lo = 8 - (CONV_W - 1) + j
            xc = xc + xe[lo:lo + tt] * cw[j:j + 1]
        xb = xc.astype(BF16)

        def gates(w_ref, b_ref):
            parts = [jnp.dot(xb[:, c * MXU_DIM:(c + 1) * MXU_DIM], w_ref[c], preferred_element_type=F32)
                     for c in range(D_RNN // MXU_DIM)]
            return jax.nn.sigmoid(jnp.concatenate(parts, axis=1) + b_ref[...])

        r = gates(wr_ref, br_ref)
        ig = gates(wi_ref, bi_ref)
        log_a = -LRU_C * r * softplus
        a_ref[k] = jnp.exp(log_a)
        th = jnp.tanh(log_a)
        u_ref[k] = jnp.sqrt(-2.0 * th / (1.0 - th)) * (ig * xc)
        tail_ref[k] = x[tt - 8:tt]

    def step(t, hs):
        out = []
        for k in range(nb):
            h = a_ref[k, pl.ds(t, 1), :] * hs[k] + u_ref[k, pl.ds(t, 1), :]
            h_ref[k, pl.ds(t, 1), :] = h
            out.append(h)
        return tuple(out)

    hs = lax.fori_loop(0, tt, step, tuple(carry_ref[k] for k in range(nb)), unroll=8)
    for k in range(nb):
        carry_ref[k] = hs[k]

    @pl.when(i == pl.num_programs(1) - 1)
    def _():
        hl_ref[...] = carry_ref[...]
        cs_ref[...] = tail_ref[...]


def _rglru(z, b, t, hist8, h0, conv_w, conv_b, wr_bd, b_rg, wi_bd, b_ig, lam, tt, nb):
    nt = t // tt

    def const(shape):
        return pl.BlockSpec(shape, lambda gb, i: tuple(0 for _ in shape))

    def per_seq(rows):
        return pl.BlockSpec((nb, rows, D_RNN), lambda gb, i: (gb, 0, 0))

    row = lambda v: v.reshape(1, D_RNN)
    h, h_last, conv_tail = pl.pallas_call(
        functools.partial(_rglru_kernel, tt=tt, nb=nb),
        grid=(b // nb, nt),
        in_specs=[pl.BlockSpec((nb, tt, D_RNN), lambda gb, i: (gb, i, C_XRNN // D_RNN)),
                  per_seq(8), per_seq(1),
                  const((CONV_W, D_RNN)), const((1, D_RNN)),
                  const((D_RNN // MXU_DIM, MXU_DIM, MXU_DIM)), const((1, D_RNN)),
                  const((D_RNN // MXU_DIM, MXU_DIM, MXU_DIM)), const((1, D_RNN)),
                  const((1, D_RNN))],
        out_specs=[pl.BlockSpec((nb, tt, D_RNN), lambda gb, i: (gb, i, 0)), per_seq(1), per_seq(8)],
        out_shape=[jax.ShapeDtypeStruct((b, t, D_RNN), F32),
                   jax.ShapeDtypeStruct((b, 1, D_RNN), F32),
                   jax.ShapeDtypeStruct((b, 8, D_RNN), F32)],
        scratch_shapes=[pltpu.VMEM((nb, 1, D_RNN), F32), pltpu.VMEM((nb, 8, D_RNN), F32),
                        pltpu.VMEM((nb, tt, D_RNN), F32), pltpu.VMEM((nb, tt, D_RNN), F32)],
        compiler_params=_cparams(("parallel", "arbitrary")),
        name="rglru",
    )(z.reshape(b, t, N_PACK), hist8, h0.reshape(b, 1, D_RNN), conv_w, row(conv_b),
      wr_bd, row(b_rg), wi_bd, row(b_ig), row(lam))
    return h.reshape(b * t, D_RNN), h_last, conv_tail


def _out_kernel(x_ref, oatt_ref, gatt_ref, h_ref, grnn_ref, gma_ref, gmr_ref, wa_ref, wr_ref, wo_ref, y_ref):
    a = (oatt_ref[...] * jax.nn.silu(gatt_ref[...])).astype(BF16)
    u_att = jnp.dot(a, wa_ref[...], preferred_element_type=F32)
    r = (h_ref[...] * jax.nn.silu(grnn_ref[...])).astype(BF16)
    u_rnn = jnp.dot(r, wr_ref[...], preferred_element_type=F32)
    m = jax.nn.sigmoid(gma_ref[...]) * u_att + jax.nn.sigmoid(gmr_ref[...]) * u_rnn
    y_ref[...] = x_ref[...] + jnp.dot(m.astype(BF16), wo_ref[...], preferred_element_type=F32)


def _out_proj(x2d, z, o_att, h_rnn, wa, wr, wo, tm):
    m = x2d.shape[0]

    def zspec(width, col):
        return pl.BlockSpec((tm, width), lambda i, c=col // width: (i, c))

    def wspec(shape):
        return pl.BlockSpec(shape, lambda i: (0, 0), pipeline_mode=pl.Buffered(1))

    rows = lambda width: pl.BlockSpec((tm, width), lambda i: (i, 0))
    return pl.pallas_call(
        _out_kernel,
        grid=(m // tm,),
        in_specs=[rows(D_MODEL), rows(ATT_WIDTH), zspec(ATT_WIDTH, C_GATT), rows(D_RNN), zspec(D_RNN, C_GRNN),
                  zspec(D_MODEL, C_GMATT), zspec(D_MODEL, C_GMRNN),
                  wspec((ATT_WIDTH, D_MODEL)), wspec((D_RNN, D_MODEL)), wspec((D_MODEL, D_MODEL))],
        out_specs=rows(D_MODEL),
        out_shape=jax.ShapeDtypeStruct((m, D_MODEL), F32),
        compiler_params=_cparams(("parallel",)),
        name="out_proj",
    )(x2d, o_att, z, h_rnn, z, z, z, wa, wr, wo)


def _prep_sample_kernel(zq_ref, zc_ref, zs_ref, zw_ref, qg_ref, ksg_ref, kwg_ref, bd_ref,
                        qn_ref, lc_ref, ls_ref, lw_ref):
    bd = bd_ref[...]
    qn_ref[...] = _seg_rms(zq_ref[...], bd) * qg_ref[...] * SCALE
    lc_ref[...] = zc_ref[...]
    for z_ref, g_ref, l_ref in ((zs_ref, ksg_ref, ls_ref), (zw_ref, kwg_ref, lw_ref)):
        z = z_ref[...]
        l_ref[:, :KV_WIDTH] = _seg_rms(z[:, :KV_WIDTH], bd) * g_ref[...]
        l_ref[:, KV_WIDTH:] = z[:, KV_WIDTH:]


def _prep_sample(z, q_gain, ks_gain, kw_gain, bd):
    m = z.shape[0]

    def zspec(width, col):
        return pl.BlockSpec((m, width), lambda i, c=col // width: (0, c))

    def const(shape):
        return pl.BlockSpec(shape, lambda i: tuple(0 for _ in shape))

    full = lambda width: pl.BlockSpec((m, width), lambda i: (0, 0))
    leaf = jax.ShapeDtypeStruct((m, 2 * KV_WIDTH), F32)
    return pl.pallas_call(
        _prep_sample_kernel,
        grid=(1,),
        in_specs=[zspec(ATT_WIDTH, C_Q), zspec(512, C_KV), zspec(512, C_KV + 512), zspec(512, C_KV + 1024),
                  const((1, ATT_WIDTH)), const((1, KV_WIDTH)), const((1, KV_WIDTH)), const((MXU_DIM, MXU_DIM))],
        out_specs=[full(ATT_WIDTH), full(512), full(512), full(512)],
        out_shape=[jax.ShapeDtypeStruct((m, ATT_WIDTH), F32), leaf, leaf, leaf],
        compiler_params=_cparams(("arbitrary",)),
        name="prep_sample",
    )(z, z, z, z, q_gain, ks_gain, kw_gain, bd)


def _page_specs(n):
    return [pl.BlockSpec((1, 2 * KV_WIDTH, PAGE_SIZE),
                         lambda bi, s, pt, k=k, n=n: (pt[bi, s * n + k], 0, 0)) for k in range(n)]


def _compress_pages_kernel(pt_ref, *refs):
    del pt_ref
    pages = refs[:CMP_PAGES]
    wt_ref, seg_ref, ok_ref, ov_ref = refs[CMP_PAGES:]
    wt = wt_ref[...]
    for tile in range(CMP_PAGES // CMP_TILE_PAGES):
        acc_k = jnp.zeros((KV_WIDTH, LANES), F32)
        acc_v = jnp.zeros((KV_WIDTH, LANES), F32)
        for pair in range(CMP_TILE_PAGES // 2):
            first = tile * CMP_TILE_PAGES + 2 * pair
            pa = pages[first][0] * wt
            pb = pages[first + 1][0] * wt
            seg = seg_ref[pair]
            lhs_k = jnp.concatenate([pa[:KV_WIDTH], pb[:KV_WIDTH]], axis=1).astype(BF16)
            lhs_v = jnp.concatenate([pa[KV_WIDTH:], pb[KV_WIDTH:]], axis=1).astype(BF16)
            acc_k = acc_k + jnp.dot(lhs_k, seg, preferred_element_type=F32)
            acc_v = acc_v + jnp.dot(lhs_v, seg, preferred_element_type=F32)
        ok_ref[0, :, tile * LANES:(tile + 1) * LANES] = acc_k
        ov_ref[0, :, tile * LANES:(tile + 1) * LANES] = acc_v


def _compress_pages(page_table, pool_t, wt, seg):
    b, n_pages = page_table.shape
    ns = n_pages // CMP_PAGES
    cols = CMP_PAGES // CMP_TILE_PAGES * LANES
    out_spec = pl.BlockSpec((1, KV_WIDTH, cols), lambda bi, s, pt: (bi, 0, s))
    grid_spec = pltpu.PrefetchScalarGridSpec(
        num_scalar_prefetch=1,
        grid=(b, ns),
        in_specs=_page_specs(CMP_PAGES) + [
            pl.BlockSpec((2 * KV_WIDTH, PAGE_SIZE), lambda bi, s, pt: (0, 0)),
            pl.BlockSpec((CMP_TILE_PAGES // 2, 2 * PAGE_SIZE, LANES), lambda bi, s, pt: (0, 0, 0))],
        out_specs=[out_spec, out_spec],
    )
    shape = jax.ShapeDtypeStruct((b, KV_WIDTH, ns * cols), F32)
    return pl.pallas_call(
        _compress_pages_kernel,
        grid_spec=grid_spec,
        out_shape=[shape, shape],
        compiler_params=_cparams(("parallel", "arbitrary")),
        name="compress_pages",
    )(page_table, *([pool_t] * CMP_PAGES), wt, seg)


def _softmax_lanes(parts, masks):
    parts = [jnp.where(mk, s, NEG) for s, mk in zip(parts, masks)]
    mx = functools.reduce(jnp.maximum, [jnp.max(s, axis=1, keepdims=True) for s in parts])
    es = [jnp.where(mk, jnp.exp(s - mx), 0.0) for s, mk in zip(parts, masks)]
    den = jnp.maximum(sum(jnp.sum(e, axis=1, keepdims=True) for e in es), TINY)
    return [e / den for e in es]


def _sample_cmp_win_kernel(qbd_ref, ck_ref, cv_ref, kg_ref, cwin_ref, wnew_ref,
                           ocmp_ref, owin_ref, bias_ref, wst_ref, *, past_len, t_new):
    qbd = qbd_ref[0]
    rows = qbd.shape[0]
    ck = ck_ref[0]
    n_cmp = ck.shape[1]
    parts = []
    for g in range(KV_HEADS):
        xs = ck[g * HEAD_DIM:(g + 1) * HEAD_DIM]
        parts.append(xs * lax.rsqrt(jnp.mean(xs * xs, axis=0, keepdims=True) + EPS))
    kc = (jnp.concatenate(parts, axis=0) * kg_ref[...]).astype(BF16)

    t_row = _row_iota((rows, n_cmp)) % t_new
    n_lane = _lane_iota((rows, n_cmp))
    cmp_id = LANES * (n_lane // LANES) + 2 * (n_lane % (LANES // 2)) + (n_lane % LANES) // (LANES // 2)
    vis = (cmp_id + 1) * CMP_BLOCK - 1 <= past_len + t_row
    (p,) = _softmax_lanes([jnp.dot(qbd, kc, preferred_element_type=F32)], [vis])
    ocmp_ref[0] = _dot_nt(p.astype(BF16), cv_ref[0].astype(BF16))

    gt = KV_HEADS * t_new
    psum = p[0:gt]
    for r in range(1, Q_PER_KV):
        psum = psum + p[r * gt:(r + 1) * gt]
    halves = []
    for c in range(n_cmp // LANES):
        tile = psum[:, c * LANES:(c + 1) * LANES]
        halves.append(tile + pltpu.roll(tile, LANES // 2, 1))
    if len(halves) == 1:
        imp = halves[0][:, :LANES // 2]
    else:
        low = _lane_iota((gt, LANES)) < LANES // 2
        imp = jnp.concatenate([jnp.where(low, halves[c], halves[c + 1]) for c in range(0, len(halves), 2)], axis=1)
    n_blk = n_cmp // 2
    j = _lane_iota((gt, n_blk))
    cur = (past_len + _row_iota((gt, n_blk)) % t_new) // SEL_BLOCK
    cand = j < cur
    forced = (j == 0) | (j == cur - 1)
    score = jnp.where(cand, jnp.where(forced, jnp.inf, imp), -jnp.inf)
    width = min(LANES, n_blk)
    tiles = [score[:, k * width:(k + 1) * width] for k in range(n_blk // width)]
    counts = [jnp.zeros((gt, width), jnp.int32) for _ in tiles]
    lane_w = _lane_iota((gt, width))
    for c in range(n_blk):
        col = score[:, c:c + 1]
        for k, tile in enumerate(tiles):
            if k < c // width:
                inc = jnp.where(col > tile, 1, 0)
            elif k > c // width:
                inc = jnp.where(col >= tile, 1, 0)
            else:
                inc = jnp.where(lane_w > c % width, jnp.where(col >= tile, 1, 0), jnp.where(col > tile, 1, 0))
            counts[k] = counts[k] + inc
    count = counts[0] if len(counts) == 1 else jnp.concatenate(counts, axis=1)
    keep = (count < N_SEL - 1) & cand
    bias_ref[0] = jnp.where(keep, 0.0, SEL_BIAS)

    cwin = cwin_ref[0]
    wb = cwin.shape[1]
    wnew = wnew_ref[0]
    wnew_p = jnp.concatenate([wnew, jnp.zeros((LANES - t_new, 2 * KV_WIDTH), F32)], axis=0)
    t_w = _row_iota((rows, wb)) % t_new
    idx = _lane_iota((rows, wb))
    ok_w = (idx <= wb + t_w) & (idx > wb + t_w - WINDOW)
    t_n = _row_iota((rows, LANES)) % t_new
    idx_n = wb + _lane_iota((rows, LANES))
    ok_n = (idx_n <= wb + t_n) & (idx_n > wb + t_n - WINDOW) & (_lane_iota((rows, LANES)) < t_new)
    p_w, p_n = _softmax_lanes([jnp.dot(qbd, cwin[:KV_WIDTH].astype(BF16), preferred_element_type=F32),
                               _dot_nt(qbd, wnew_p[:, :KV_WIDTH].astype(BF16))], [ok_w, ok_n])
    owin_ref[0] = (_dot_nt(p_w.astype(BF16), cwin[KV_WIDTH:].astype(BF16))
                   + jnp.dot(p_n.astype(BF16), wnew_p[:, KV_WIDTH:].astype(BF16), preferred_element_type=F32))

    keep_lanes = LANES - t_new
    new_t = jnp.concatenate([jnp.zeros((keep_lanes, 2 * KV_WIDTH), F32), wnew], axis=0).T
    rolled = [pltpu.roll(cwin[:, c * LANES:(c + 1) * LANES], keep_lanes, 1) for c in range(wb // LANES)]
    rolled.append(new_t)
    first = _lane_iota((2 * KV_WIDTH, LANES)) < keep_lanes
    for c in range(wb // LANES):
        wst_ref[0, :, c * LANES:(c + 1) * LANES] = jnp.where(first, rolled[c], rolled[c + 1])


def _sample_cmp_win(qbd, ck_t, cv_t, kc_gain_col, cwin_t, wnew, past_len):
    b, rows, _ = qbd.shape
    n_cmp = ck_t.shape[2]
    wb = cwin_t.shape[2]
    t_new = wnew.shape[1]
    gt = KV_HEADS * t_new

    def per_b(shape):
        return pl.BlockSpec((1,) + shape, lambda bi: (bi, 0, 0))

    return pl.pallas_call(
        functools.partial(_sample_cmp_win_kernel, past_len=past_len, t_new=t_new),
        grid=(b,),
        in_specs=[per_b((rows, MXU_DIM)), per_b((KV_WIDTH, n_cmp)), per_b((KV_WIDTH, n_cmp)),
                  pl.BlockSpec((KV_WIDTH, 1), lambda bi: (0, 0)),
                  per_b((2 * KV_WIDTH, wb)), per_b((t_new, 2 * KV_WIDTH))],
        out_specs=[per_b((rows, MXU_DIM)), per_b((rows, MXU_DIM)), per_b((gt, n_cmp // 2)), per_b((2 * KV_WIDTH, wb))],
        out_shape=[jax.ShapeDtypeStruct((b, rows, MXU_DIM), F32), jax.ShapeDtypeStruct((b, rows, MXU_DIM), F32),
                   jax.ShapeDtypeStruct((b, gt, n_cmp // 2), F32), jax.ShapeDtypeStruct((b, 2 * KV_WIDTH, wb), F32)],
        compiler_params=_cparams(("parallel",)),
        name="sample_cmp_win",
    )(qbd, ck_t, cv_t, kc_gain_col, cwin_t, wnew)


def _sample_sel_kernel(pt_ref, *refs, t_new):
    del pt_ref
    pages = refs[:SEL_PAGES]
    (qbd_ref, bsel_ref, onehot_ref, snew_ref, ocmp_ref, owin_ref, gl_ref,
     o_ref, m_ref, l_ref, acc_ref) = refs[SEL_PAGES:]
    s_idx = pl.program_id(1)
    qbd = qbd_ref[0]
    rows = qbd.shape[0]

    @pl.when(s_idx == 0)
    def _():
        m_ref[...] = jnp.full(m_ref.shape, NEG, F32)
        l_ref[...] = jnp.zeros(l_ref.shape, F32)
        acc_ref[...] = jnp.zeros(acc_ref.shape, F32)

    def partial_softmax(s, pv):
        m = jnp.max(s, axis=1, keepdims=True)
        pe = jnp.exp(s - m)
        return m, jnp.sum(pe, axis=1, keepdims=True), pv(pe.astype(BF16))

    def update(parts):
        m_old = m_ref[...]
        m_new = functools.reduce(jnp.maximum, [p[0] for p in parts], m_old)
        scale = jnp.exp(m_old - m_new)
        l = scale * l_ref[...]
        acc = scale * acc_ref[...]
        for m, l_part, acc_part in parts:
            c = jnp.exp(m - m_new)
            l = l + c * l_part
            acc = acc + c * acc_part
        m_ref[...] = m_new
        l_ref[...] = l
        acc_ref[...] = acc

    bsel = bsel_ref[0, 0]
    parts = []
    for g in range(SEL_PAGES // SEL_SUB):
        pgs = pages[g * SEL_SUB:(g + 1) * SEL_SUB]
        keys = SEL_SUB * PAGE_SIZE
        k_t = jnp.concatenate([pg[0, :KV_WIDTH, :].astype(BF16) for pg in pgs], axis=1)
        v_t = jnp.concatenate([pg[0, KV_WIDTH:, :].astype(BF16) for pg in pgs], axis=1)
        s = (jnp.dot(qbd, k_t, preferred_element_type=F32)
             + jnp.dot(bsel, onehot_ref[:, g * keys:(g + 1) * keys], preferred_element_type=F32))
        parts.append(partial_softmax(s, lambda pb, v_t=v_t: _dot_nt(pb, v_t)))
    update(parts)

    @pl.when(s_idx == pl.num_programs(1) - 1)
    def _():
        snew = jnp.concatenate([snew_ref[0], jnp.zeros((LANES - t_new, 2 * KV_WIDTH), F32)], axis=0)
        t_q = _row_iota((rows, LANES)) % t_new
        t_k = _lane_iota((rows, LANES))
        s = jnp.where((t_k <= t_q) & (t_k < t_new), _dot_nt(qbd, snew[:, :KV_WIDTH].astype(BF16)), NEG)
        v_new = snew[:, KV_WIDTH:].astype(BF16)
        update([partial_softmax(s, lambda pb: jnp.dot(pb, v_new, preferred_element_type=F32))])
        o_sel = acc_ref[...] / l_ref[...]
        sig = jax.nn.sigmoid(gl_ref[0])
        o_ref[0] = sig[:, 0:1] * ocmp_ref[0] + sig[:, 1:2] * o_sel + sig[:, 2:3] * owin_ref[0]


def _sample_sel(page_table, pool_t, qbd, bsel, onehot, snew, ocmp, owin, gl):
    b, n_pages = page_table.shape
    ns = n_pages // SEL_PAGES
    rows = qbd.shape[1]
    t_new = snew.shape[1]
    keys = SEL_PAGES * PAGE_SIZE

    def per_b(shape):
        return pl.BlockSpec((1,) + shape, lambda bi, s, pt: (bi,) + tuple(0 for _ in shape))

    grid_spec = pltpu.PrefetchScalarGridSpec(
        num_scalar_prefetch=1,
        grid=(b, ns),
        in_specs=_page_specs(SEL_PAGES) + [
            per_b((rows, MXU_DIM)),
            pl.BlockSpec((1, 1, rows, LANES), lambda bi, s, pt: (bi, s, 0, 0)),
            pl.BlockSpec((LANES, keys), lambda bi, s, pt: (0, 0)),
            per_b((t_new, 2 * KV_WIDTH)), per_b((rows, MXU_DIM)), per_b((rows, MXU_DIM)), per_b((rows, LANES))],
        out_specs=per_b((rows, MXU_DIM)),
        scratch_shapes=[pltpu.VMEM((rows, 1), F32), pltpu.VMEM((rows, 1), F32), pltpu.VMEM((rows, MXU_DIM), F32)],
    )
    return pl.pallas_call(
        functools.partial(_sample_sel_kernel, t_new=t_new),
        grid_spec=grid_spec,
        out_shape=jax.ShapeDtypeStruct((b, rows, MXU_DIM), F32),
        compiler_params=_cparams(("parallel", "arbitrary")),
        name="sample_sel",
    )(page_table, *([pool_t] * SEL_PAGES), qbd, bsel, onehot, snew, ocmp, owin, gl)


def _pack_weights(w_in, q_norm_g, k_norm_g, w_cmp, w_rg, w_ig, w_att_out, w_rnn_out, w_out):
    def block_diag(w):
        per = MXU_DIM // HEAD_DIM
        w4 = w.reshape(RNN_BLOCKS // per, per, HEAD_DIM, HEAD_DIM)
        eye = jnp.eye(per, dtype=w.dtype)
        return jnp.einsum('cpde,pq->cpdqe', w4, eye).reshape(RNN_BLOCKS // per, MXU_DIM, MXU_DIM).astype(BF16)

    wt = jnp.broadcast_to(w_cmp.transpose(1, 2, 0)[:, None], (2, KV_HEADS, HEAD_DIM, CMP_BLOCK))
    wt = jnp.tile(wt, (1, 1, 1, PAGE_SIZE // CMP_BLOCK)).reshape(2 * KV_WIDTH, PAGE_SIZE)
    lane = jnp.arange(2 * PAGE_SIZE)
    pair = jnp.arange(CMP_TILE_PAGES // 2)
    n_local = ((PAGE_SIZE // CMP_BLOCK) * (2 * pair[:, None] + lane[None, :] // PAGE_SIZE)
               + (lane[None, :] % PAGE_SIZE) // CMP_BLOCK)
    col = (n_local % 2) * (LANES // 2) + n_local // 2
    seg = (jnp.arange(LANES)[None, None, :] == col[:, :, None]).astype(BF16)

    return dict(
        w_t=w_in.T,
        q_gain=jnp.tile(q_norm_g, N_HEADS).reshape(1, ATT_WIDTH),
        kc_gain=jnp.tile(k_norm_g[0], KV_HEADS).reshape(1, KV_WIDTH),
        ks_gain=jnp.tile(k_norm_g[1], KV_HEADS).reshape(1, KV_WIDTH),
        kw_gain=jnp.tile(k_norm_g[2], KV_HEADS).reshape(1, KV_WIDTH),
        wc=jnp.broadcast_to(w_cmp[:, :, None, :], (CMP_BLOCK, 2, KV_HEADS, HEAD_DIM)).reshape(CMP_BLOCK, 2 * KV_WIDTH),
        wt=wt, seg=seg,
        wr_bd=block_diag(w_rg), wi_bd=block_diag(w_ig),
        wa=w_att_out.astype(BF16), wr=w_rnn_out.astype(BF16), wo=w_out.astype(BF16),
        bd=_block_diag_ones(),
    )


def _prompt_layer(x, pw, norm_g, conv_w, conv_b, b_rg, b_ig, lam):
    b, t, _ = x.shape
    tq = WINDOW // 2
    x2d = x.reshape(b * t, D_MODEL)
    z = _proj(x2d, norm_g, pw['w_t'], tm=min(1024, b * t))
    qa, leaf_c, leaf_s, leaf_w, cc, kas, vts, kaw, vtw = _prep_prompt(
        z, b, t, pw['q_gain'], pw['ks_gain'], pw['kw_gain'], pw['wc'], pw['bd'], te=tq)
    n_cmp = t // CMP_BLOCK
    cc_perm = cc.reshape(b, n_cmp // 2, 2, 2 * KV_WIDTH).transpose(0, 2, 1, 3).reshape(b, n_cmp, 2 * KV_WIDTH)
    kca, vct = _cmp_prep(cc_perm, pw['kc_gain'], pw['bd'])
    o_att = _attn_prompt(z, qa, kca, vct, kas, vts, kaw, vtw, b, t, tq)
    h_rnn, h_last, conv_tail = _rglru(z, b, t, jnp.zeros((b, 8, D_RNN), F32), jnp.zeros((b, D_RNN), F32),
                                      conv_w, conv_b, pw['wr_bd'], b_rg, pw['wi_bd'], b_ig, lam, tt=256, nb=b)
    y = _out_proj(x2d, z, o_att, h_rnn, pw['wa'], pw['wr'], pw['wo'], tm=256)
    def kv(leaf_t):
        tokens = leaf_t.shape[2]
        return leaf_t.reshape(b, 2, KV_HEADS, HEAD_DIM, tokens).transpose(0, 4, 1, 2, 3)

    w_keep = min(WINDOW, t)
    return y.reshape(b, t, D_MODEL), (kv(leaf_c), kv(leaf_s), kv(leaf_w[:, :, t - w_keep:]),
                                      h_last.reshape(b, D_RNN), conv_tail[:, 8 - (CONV_W - 1):])


def _feature_major(cache):
    n, tokens = cache.shape[:2]
    return cache.transpose(0, 2, 3, 4, 1).reshape(n, 2 * KV_WIDTH, tokens)


def _sample_layer(x, cache_cmp, cache_sel, cache_win, state_h, state_conv, page_table, pw,
                  norm_g, conv_w, conv_b, b_rg, b_ig, lam):
    b, t, _ = x.shape
    n_pages = page_table.shape[1]
    past_len = n_pages * PAGE_SIZE
    rows = N_HEADS * t
    x2d = x.reshape(b * t, D_MODEL)
    z = _proj(x2d, norm_g, pw['w_t'], tm=b * t)
    qn, leaf_c, leaf_s, leaf_w = _prep_sample(z, pw['q_gain'], pw['ks_gain'], pw['kw_gain'], pw['bd'])

    q5 = qn.reshape(b, t, KV_HEADS, Q_PER_KV, HEAD_DIM).transpose(0, 3, 2, 1, 4)
    qbd = (q5[:, :, :, :, None, :] * jnp.eye(KV_HEADS, dtype=F32)[None, None, :, None, :, None])
    qbd = qbd.reshape(b, rows, KV_WIDTH).astype(BF16)

    ck_t, cv_t = _compress_pages(page_table, _feature_major(cache_cmp), pw['wt'], pw['seg'])
    wb = cache_win.shape[1]
    o_cmp, o_win, bias, win_state_t = _sample_cmp_win(
        qbd, ck_t, cv_t, pw['kc_gain'].reshape(KV_WIDTH, 1), _feature_major(cache_win),
        leaf_w.reshape(b, t, 2 * KV_WIDTH), past_len)
    win_state = win_state_t.reshape(b, 2, KV_HEADS, HEAD_DIM, wb).transpose(0, 4, 1, 2, 3)

    ns = n_pages // SEL_PAGES
    blocks_per_step = SEL_PAGES * PAGE_SIZE // SEL_BLOCK
    bsel = bias.reshape(b, KV_HEADS * t, ns, blocks_per_step).transpose(0, 2, 1, 3)
    bsel = jnp.tile(bsel, (1, 1, Q_PER_KV, 1))
    bsel = jnp.pad(bsel, ((0, 0), (0, 0), (0, 0), (0, LANES - blocks_per_step))).astype(BF16)
    key_block = jnp.arange(SEL_PAGES * PAGE_SIZE) // SEL_BLOCK
    onehot = (jnp.arange(LANES)[:, None] == key_block[None, :]).astype(BF16)
    gl = z[:, C_GNSA:C_GNSA + 3 * N_HEADS].reshape(b, t, 3, KV_HEADS, Q_PER_KV)
    gl = jnp.pad(gl.transpose(0, 4, 3, 1, 2).reshape(b, rows, 3), ((0, 0), (0, 0), (0, LANES - 3)))
    o_full = _sample_sel(page_table, _feature_major(cache_sel), qbd, bsel, onehot,
                         leaf_s.reshape(b, t, 2 * KV_WIDTH), o_cmp, o_win, gl)
    o6 = o_full.reshape(b, Q_PER_KV, KV_HEADS, t, KV_HEADS, HEAD_DIM)
    o_att = jnp.stack([o6[:, :, g, :, g, :] for g in range(KV_HEADS)], axis=2)
    o_att = o_att.transpose(0, 3, 2, 1, 4).reshape(b * t, ATT_WIDTH)

    hist8 = jnp.pad(state_conv, ((0, 0), (8 - (CONV_W - 1), 0), (0, 0)))
    h_rnn, h_last, conv_tail = _rglru(z, b, t, hist8, state_h, conv_w, conv_b,
                                      pw['wr_bd'], b_rg, pw['wi_bd'], b_ig, lam, tt=t,
                                      nb=8 if b % 8 == 0 else b)
    y = _out_proj(x2d, z, o_att, h_rnn, pw['wa'], pw['wr'], pw['wo'], tm=min(256, b * t))
    kv = lambda leaf: leaf.reshape(b, -1, 2, KV_HEADS, HEAD_DIM)
    return y.reshape(b, t, D_MODEL), (kv(leaf_c), kv(leaf_s), win_state,
                                      h_last.reshape(b, D_RNN), conv_tail[:, 8 - (CONV_W - 1):])


def kernel(x_prompt, x_sample, cache_cmp, cache_sel, cache_win, state_h, state_conv, page_table,
           norm_g, w_in, q_norm_g, k_norm_g, w_cmp, conv_w, conv_b, w_rg, b_rg, w_ig, b_ig,
           lru_lambda, w_att_out, w_rnn_out, w_out):
    yp, ys = x_prompt, x_sample
    outs_p, outs_s = [], []
    for l in range(w_in.shape[0]):
        pw = _pack_weights(w_in[l], q_norm_g[l], k_norm_g[l], w_cmp[l], w_rg[l], w_ig[l],
                           w_att_out[l], w_rnn_out[l], w_out[l])
        yp, st_p = _prompt_layer(yp, pw, norm_g[l], conv_w[l], conv_b[l], b_rg[l], b_ig[l], lru_lambda[l])
        ys, st_s = _sample_layer(ys, cache_cmp[l], cache_sel[l], cache_win[l], state_h[l], state_conv[l],
                                 page_table, pw, norm_g[l], conv_w[l], conv_b[l], b_rg[l], b_ig[l], lru_lambda[l])
        outs_p.append(st_p)
        outs_s.append(st_s)
    cmp_p, sel_p, win_p, h_p, conv_p = [jnp.stack(a) for a in zip(*outs_p)]
    cmp_s, sel_s, win_s, h_s, conv_s = [jnp.stack(a) for a in zip(*outs_s)]
    return (yp, ys, cmp_p, sel_p, win_p, h_p, conv_p, cmp_s, sel_s, win_s, h_s, conv_s)
```

```python
import functools

import jax
import jax.numpy as jnp
from jax import lax
from jax.experimental import pallas as pl
from jax.experimental.pallas import tpu as pltpu

F32 = jnp.float32
BF16 = jnp.bfloat16

D_MODEL = 2048
N_HEADS = 16
HEAD_DIM = 64
KV_HEADS = 4
Q_PER_KV = N_HEADS // KV_HEADS
ATT_WIDTH = N_HEADS * HEAD_DIM
KV_WIDTH = KV_HEADS * HEAD_DIM
CMP_BLOCK = 32
SEL_BLOCK = 64
N_SEL = 16
WINDOW = 512
SCALE = HEAD_DIM ** -0.5
LOG2E = 1.4426950408889634
D_RNN = 1024
RNN_BLOCKS = 16
CONV_W = 4
LRU_C = 8.0
PAGE_SIZE = 128
EPS = 1e-6
NEG = -1e30
TINY = 1e-30
SEL_BIAS = -1e30

LANES = 128
MXU_DIM = 256
VMEM_LIMIT = 56 * 1024 * 1024

C_Q = 0
C_GATT = 1024
C_XRNN = 2048
C_GRNN = 3072
C_GMATT = 4096
C_GMRNN = 6144
C_KV = 8192
C_GNSA = 9728
N_PACK = 10240
PROJ_TILE = 1024
_W_Q, _W_KV, _W_GNSA, _W_GATT, _W_XRNN, _W_GRNN, _W_GMATT, _W_GMRNN = 0, 1024, 2560, 2608, 3632, 4656, 5680, 7728
PROJ_ROW_STARTS = (_W_Q, _W_GATT, _W_XRNN, _W_GRNN, _W_GMATT, _W_GMATT + 1024, _W_GMRNN, _W_GMRNN + 1024,
                   _W_KV, _W_KV + 1024)

SEL_PAGES = 64
SEL_SUB = 64
CMP_PAGES = 64
CMP_TILE_PAGES = 32
GROUPS_PER_STEP = 2
VT_ROWS = 80


def _cparams(sem):
    return pltpu.CompilerParams(dimension_semantics=sem, vmem_limit_bytes=VMEM_LIMIT)


def _proj_kernel(starts_ref, x_ref, g_ref, w_ref, o_ref, xn_ref):
    del starts_ref
    @pl.when(pl.program_id(1) == 0)
    def _():
        x = x_ref[...]
        ms = jnp.mean(x * x, axis=-1, keepdims=True)
        xn_ref[...] = (x * lax.rsqrt(ms + EPS) * g_ref[...]).astype(BF16)

    o_ref[...] = lax.dot_general(xn_ref[...], w_ref[...].astype(BF16), (((1,), (1,)), ((), ())),
                                 preferred_element_type=F32)


def _proj(x2d, norm_g, w_t, tm):
    m = x2d.shape[0]
    tn = PROJ_TILE
    grid_spec = pltpu.PrefetchScalarGridSpec(
        num_scalar_prefetch=1,
        grid=(m // tm, N_PACK // tn),
        in_specs=[
            pl.BlockSpec((tm, D_MODEL), lambda i, j, st: (i, 0)),
            pl.BlockSpec((1, D_MODEL), lambda i, j, st: (0, 0)),
            pl.BlockSpec((pl.Element(tn), pl.Element(D_MODEL)), lambda i, j, st: (pl.multiple_of(st[j], 16), 0)),
        ],
        out_specs=pl.BlockSpec((tm, tn), lambda i, j, st: (i, j)),
        scratch_shapes=[pltpu.VMEM((tm, D_MODEL), BF16)],
    )
    return pl.pallas_call(
        _proj_kernel,
        grid_spec=grid_spec,
        out_shape=jax.ShapeDtypeStruct((m, N_PACK), F32),
        compiler_params=_cparams(("parallel", "arbitrary")),
        name="proj",
    )(jnp.asarray(PROJ_ROW_STARTS, jnp.int32), x2d, norm_g.reshape(1, D_MODEL), w_t)


def _lane_iota(shape):
    return lax.broadcasted_iota(jnp.int32, shape, len(shape) - 1)


def _row_iota(shape):
    return lax.broadcasted_iota(jnp.int32, shape, 0)


def _seg_rms(x, bd):
    outs = []
    for c in range(x.shape[1] // MXU_DIM):
        xc = x[:, c * MXU_DIM:(c + 1) * MXU_DIM]
        x2 = xc * xc
        hi = x2.astype(BF16)
        lo = (x2 - hi.astype(F32)).astype(BF16)
        ss = (jnp.dot(hi, bd, preferred_element_type=F32)
              + jnp.dot(lo, bd, preferred_element_type=F32))
        outs.append(xc * lax.rsqrt(ss * (1.0 / HEAD_DIM) + EPS))
    return outs[0] if len(outs) == 1 else jnp.concatenate(outs, axis=1)


def _head_lo(x, h):
    tile = x[:, (h // 2) * LANES:(h // 2 + 1) * LANES]
    if h % 2:
        tile = pltpu.roll(tile, HEAD_DIM, 1)
    return jnp.where(_lane_iota(tile.shape) < HEAD_DIM, tile, 0.0)


def _heads_t(x):
    outs = []
    for c in range(x.shape[1] // LANES):
        xt = x[:, c * LANES:(c + 1) * LANES].T
        outs += [xt[:HEAD_DIM], xt[HEAD_DIM:]]
    return outs


def _dot_nt(a, b):
    return lax.dot_general(a, b, (((1,), (1,)), ((), ())), preferred_element_type=F32)


def _block_diag_ones():
    r = jnp.arange(MXU_DIM) // HEAD_DIM
    return (r[:, None] == r[None, :]).astype(BF16)


def _prep_prompt_kernel(zq_ref, zc_ref, zs_ref, zw_ref, qg_ref, ksg_ref, kwg_ref, wc_ref, bd_ref,
                        qa_ref, lc_ref, ls_ref, lw_ref, cc_ref, kas_ref, vts_ref, kaw_ref, vtw_ref, *, te):
    i = pl.program_id(1)
    bd = bd_ref[...]
    qn = _seg_rms(zq_ref[...], bd) * qg_ref[...] * (SCALE * LOG2E)
    for h in range(N_HEADS):
        qa_ref[0, h // Q_PER_KV, h % Q_PER_KV] = _head_lo(qn, h).astype(BF16)

    zc = zc_ref[...]
    for c in range(2 * KV_WIDTH // LANES):
        lc_ref[0, c * LANES:(c + 1) * LANES, :] = zc[:, c * LANES:(c + 1) * LANES].T
    cc_ref[0] = jnp.sum(zc.reshape(te // CMP_BLOCK, CMP_BLOCK, 2 * KV_WIDTH) * wc_ref[...][None], axis=1)

    lane = _lane_iota((te, LANES))
    own_block = (i * te + _row_iota((te, LANES))) // SEL_BLOCK
    onehot = jnp.where(lane - HEAD_DIM == own_block, 1.0, 0.0)
    ones_rows = jnp.where(_row_iota((VT_ROWS - HEAD_DIM, te)) == 0, 1.0, 0.0).astype(BF16)

    for z_ref, g_ref, l_ref, ka_ref, vt_ref, with_onehot in (
            (zs_ref, ksg_ref, ls_ref, kas_ref, vts_ref, True),
            (zw_ref, kwg_ref, lw_ref, kaw_ref, vtw_ref, False)):
        z = z_ref[...]
        kn = _seg_rms(z[:, :KV_WIDTH], bd) * g_ref[...]
        v = z[:, KV_WIDTH:]
        for c in range(KV_WIDTH // LANES):
            l_ref[0, c * LANES:(c + 1) * LANES, :] = kn[:, c * LANES:(c + 1) * LANES].T
        for g, vt in enumerate(_heads_t(v)):
            l_ref[0, KV_WIDTH + g * HEAD_DIM:KV_WIDTH + (g + 1) * HEAD_DIM, :] = vt
            ka = _head_lo(kn, g)
            if with_onehot:
                ka = jnp.where(lane < HEAD_DIM, ka, onehot)
            ka_ref[0, g] = ka.astype(BF16)
            vt_ref[0, g, 0] = jnp.concatenate([vt.astype(BF16), ones_rows], axis=0)


def _prep_prompt(z, b, t, q_gain, ks_gain, kw_gain, wc, bd, te):
    nt = t // te

    def zspec(width, col):
        return pl.BlockSpec((te, width), lambda bi, i, c=col // width: (bi * nt + i, c))

    def const(shape):
        return pl.BlockSpec(shape, lambda bi, i: tuple(0 for _ in shape))

    leaf_spec = pl.BlockSpec((1, 2 * KV_WIDTH, te), lambda bi, i: (bi, 0, i))
    leaf_shape = jax.ShapeDtypeStruct((b, 2 * KV_WIDTH, t), F32)
    k_spec = pl.BlockSpec((1, KV_HEADS, te, LANES), lambda bi, i: (bi, 0, i, 0))
    k_shape = jax.ShapeDtypeStruct((b, KV_HEADS, t, LANES), BF16)
    vt_spec = pl.BlockSpec((1, KV_HEADS, 1, VT_ROWS, te), lambda bi, i: (bi, 0, i, 0, 0))
    vt_shape = jax.ShapeDtypeStruct((b, KV_HEADS, nt, VT_ROWS, te), BF16)
    return pl.pallas_call(
        functools.partial(_prep_prompt_kernel, te=te),
        grid=(b, nt),
        in_specs=[zspec(ATT_WIDTH, C_Q), zspec(512, C_KV), zspec(512, C_KV + 512), zspec(512, C_KV + 1024),
                  const((1, ATT_WIDTH)), const((1, KV_WIDTH)), const((1, KV_WIDTH)),
                  const((CMP_BLOCK, 2 * KV_WIDTH)), const((MXU_DIM, MXU_DIM))],
        out_specs=[pl.BlockSpec((1, KV_HEADS, Q_PER_KV, te, LANES), lambda bi, i: (bi, 0, 0, i, 0)),
                   leaf_spec, leaf_spec, leaf_spec,
                   pl.BlockSpec((1, te // CMP_BLOCK, 2 * KV_WIDTH), lambda bi, i: (bi, i, 0)),
                   k_spec, vt_spec, k_spec, vt_spec],
        out_shape=[jax.ShapeDtypeStruct((b, KV_HEADS, Q_PER_KV, t, LANES), BF16),
                   leaf_shape, leaf_shape, leaf_shape,
                   jax.ShapeDtypeStruct((b, t // CMP_BLOCK, 2 * KV_WIDTH), F32),
                   k_shape, vt_shape, k_shape, vt_shape],
        compiler_params=_cparams(("parallel", "parallel")),
        name="prep_prompt",
    )(z, z, z, z, q_gain, ks_gain, kw_gain, wc, bd)


def _cmp_prep_kernel(cc_ref, kg_ref, bd_ref, kca_ref, vct_ref):
    cc = cc_ref[0]
    kc = _seg_rms(cc[:, :KV_WIDTH], bd_ref[...]) * kg_ref[...]
    for g, vt in enumerate(_heads_t(cc[:, KV_WIDTH:])):
        kca_ref[0, g] = _head_lo(kc, g).astype(BF16)
        vct_ref[0, g] = vt.astype(BF16)


def _cmp_prep(cc_perm, kc_gain, bd):
    b, n, _ = cc_perm.shape
    return pl.pallas_call(
        _cmp_prep_kernel,
        grid=(b,),
        in_specs=[pl.BlockSpec((1, n, 2 * KV_WIDTH), lambda bi: (bi, 0, 0)),
                  pl.BlockSpec((1, KV_WIDTH), lambda bi: (0, 0)),
                  pl.BlockSpec((MXU_DIM, MXU_DIM), lambda bi: (0, 0))],
        out_specs=[pl.BlockSpec((1, KV_HEADS, n, LANES), lambda bi: (bi, 0, 0, 0)),
                   pl.BlockSpec((1, KV_HEADS, HEAD_DIM, n), lambda bi: (bi, 0, 0, 0))],
        out_shape=[jax.ShapeDtypeStruct((b, KV_HEADS, n, LANES), BF16),
                   jax.ShapeDtypeStruct((b, KV_HEADS, HEAD_DIM, n), BF16)],
        compiler_params=_cparams(("parallel",)),
        name="cmp_prep",
    )(cc_perm, kc_gain, bd)


def _rank_select(score, n_keep):
    n, w = score.shape
    sub = lax.broadcasted_iota(jnp.int32, (8, w), 0)
    groups = [score[8 * k:8 * k + 8] for k in range(n // 8)]
    counts = [jnp.zeros((8, w), jnp.int32) for _ in groups]
    for i in range(n):
        si = jnp.broadcast_to(score[i:i + 1, :], (8, w))
        for k, blk in enumerate(groups):
            if 8 * k + 7 < i:
                inc = jnp.where(si > blk, 1, 0)
            elif 8 * k > i:
                inc = jnp.where(si >= blk, 1, 0)
            else:
                inc = jnp.where(sub > (i - 8 * k), jnp.where(si >= blk, 1, 0), jnp.where(si > blk, 1, 0))
            counts[k] = counts[k] + inc
    return jnp.concatenate(counts, axis=0) < n_keep


def _attn_prompt_kernel(qa_ref, kca_ref, vct_ref, kas_ref, vts_ref, kaw_ref, vtw_ref, gl_ref, o_ref, sig_ref, *, tq):
    i = pl.program_id(2)
    rows = Q_PER_KV * tq
    t0 = i * tq
    n_cmp = kca_ref.shape[2]
    n_selb = n_cmp // 2
    t_q = t0 + _lane_iota((1, rows)) % tq

    def prepare(gi):
        qa = qa_ref[0, gi].reshape(rows, LANES)
        s = _dot_nt(kca_ref[0, gi], qa)
        n_row = _row_iota((n_cmp, tq))
        cmp_id = 2 * (n_row % n_selb) + n_row // n_selb
        vis = (cmp_id + 1) * CMP_BLOCK - 1 <= t0 + _lane_iota((n_cmp, tq))
        vis = jnp.concatenate([vis] * Q_PER_KV, axis=1)
        s = jnp.where(vis, s, NEG)
        e = jnp.where(vis, jnp.exp2(s - jnp.max(s, axis=0, keepdims=True)), 0.0)
        p = e / jnp.maximum(jnp.sum(e, axis=0, keepdims=True), TINY)
        o_cmp = jnp.dot(vct_ref[0, gi], p.astype(BF16), preferred_element_type=F32)

        imp = jnp.zeros((n_selb, tq), F32)
        for r in range(Q_PER_KV):
            imp = imp + (p[:n_selb, r * tq:(r + 1) * tq] + p[n_selb:, r * tq:(r + 1) * tq])
        j_row = _row_iota((n_selb, tq))
        cur = (t0 + _lane_iota((n_selb, tq))) // SEL_BLOCK
        cand = j_row < cur
        forced = (j_row == 0) | (j_row == cur - 1)
        score = jnp.where(cand, jnp.where(forced, jnp.inf, imp), -jnp.inf)
        keep = (_rank_select(score, N_SEL - 1) & cand) | (j_row == cur)
        bias_t = jnp.where(keep, 0.0, SEL_BIAS)
        if n_selb < HEAD_DIM:
            bias_t = jnp.concatenate([bias_t, jnp.full((HEAD_DIM - n_selb, tq), SEL_BIAS, F32)], axis=0)
        bias = jnp.concatenate([jnp.zeros((HEAD_DIM, tq), F32), bias_t], axis=0).T
        bias4 = jnp.concatenate([bias] * Q_PER_KV, axis=0).astype(BF16)
        q_sel = jnp.where(_lane_iota((rows, LANES)) >= HEAD_DIM, bias4, qa)
        s = jnp.where(in_window, _dot_nt(k_rows(kaw_ref, gi, c_win, n_w), qa), NEG)
        o_win = normalised(partial_softmax(s, vtw_ref, gi, c_win)[1])
        return q_sel, o_cmp, o_win

    def partial_softmax(s, vt_ref, gi, c0):
        m = jnp.max(s, axis=0, keepdims=True)
        pb = jnp.exp2(s - m).astype(BF16)
        acc = None
        for j in range(s.shape[0] // tq):
            part = jnp.dot(vt_ref[0, gi, c0 + j], pb[j * tq:(j + 1) * tq], preferred_element_type=F32)
            acc = part if acc is None else acc + part
        return m, acc

    def merge(a, b):
        m = jnp.maximum(a[0], b[0])
        return m, a[1] * jnp.exp2(a[0] - m) + b[1] * jnp.exp2(b[0] - m)

    def normalised(acc):
        return acc[:HEAD_DIM] * (1.0 / acc[HEAD_DIM:HEAD_DIM + 1])

    def k_rows(ref, gi, c0, n):
        return ref[0, gi, pl.ds(pl.multiple_of(c0 * tq, tq), n * tq), :]

    n_w = WINDOW // tq + 1
    c_win = jnp.maximum(i - (n_w - 1), 0)
    age = t_q - (c_win * tq + _row_iota((n_w * tq, 1)))
    in_window = (age >= 0) & (age < WINDOW)
    groups = [prepare(gi) for gi in range(GROUPS_PER_STEP)]

    def sel_body(pair, carries):
        return tuple(
            merge(carries[gi], partial_softmax(_dot_nt(k_rows(kas_ref, gi, 2 * pair, 2), groups[gi][0]),
                                               vts_ref, gi, 2 * pair))
            for gi in range(GROUPS_PER_STEP))

    init = (jnp.full((1, rows), NEG, F32), jnp.zeros((VT_ROWS, rows), F32))
    carries = lax.fori_loop(0, i // 2, sel_body, (init,) * GROUPS_PER_STEP)
    c_sel = jnp.maximum(i - 1, 0)
    k_off = _row_iota((2 * tq, 1))
    done = (c_sel + k_off // tq < i) & (i % 2 == 0)
    k_pos_sel = jnp.where(done, t0 + 2 * tq, c_sel * tq + k_off)

    sig_ref[...] = jax.nn.sigmoid(gl_ref[...]).T
    for gi, (q_sel, o_cmp, o_win) in enumerate(groups):
        s = jnp.where(k_pos_sel <= t_q, _dot_nt(k_rows(kas_ref, gi, c_sel, 2), q_sel), NEG)
        o_sel = normalised(merge(carries[gi], partial_softmax(s, vts_ref, gi, c_sel))[1])
        group = pl.program_id(1) * GROUPS_PER_STEP + gi

        def gate(branch):
            r0 = branch * N_HEADS + group * Q_PER_KV
            return jnp.concatenate([sig_ref[pl.ds(r0 + r, 1), :] for r in range(Q_PER_KV)], axis=1)

        o_t = gate(0) * o_cmp + gate(1) * o_sel + gate(2) * o_win
        for h in range(0, Q_PER_KV, 2):
            pair_t = jnp.concatenate([o_t[:, h * tq:(h + 1) * tq], o_t[:, (h + 1) * tq:(h + 2) * tq]], axis=0)
            lane0 = (gi * Q_PER_KV + h) * HEAD_DIM
            o_ref[:, lane0:lane0 + LANES] = pair_t.T


def _attn_prompt(z, qa, kca, vct, kas, vts, kaw, vtw, b, t, tq):
    nt = t // tq
    n_cmp = kca.shape[2]
    gps = GROUPS_PER_STEP
    qspec = pl.BlockSpec((1, gps, Q_PER_KV, tq, LANES), lambda bi, g, i: (bi, g, 0, i, 0))
    kspec = pl.BlockSpec((1, gps, t, LANES), lambda bi, g, i: (bi, g, 0, 0))
    vspec = pl.BlockSpec((1, gps, nt, VT_ROWS, tq), lambda bi, g, i: (bi, g, 0, 0, 0))
    gspec = pl.BlockSpec((tq, LANES), lambda bi, g, i: (bi * nt + i, C_GNSA // LANES))
    return pl.pallas_call(
        functools.partial(_attn_prompt_kernel, tq=tq),
        grid=(b, KV_HEADS // gps, nt),
        in_specs=[qspec,
                  pl.BlockSpec((1, gps, n_cmp, LANES), lambda bi, g, i: (bi, g, 0, 0)),
                  pl.BlockSpec((1, gps, HEAD_DIM, n_cmp), lambda bi, g, i: (bi, g, 0, 0)),
                  kspec, vspec, kspec, vspec, gspec],
        out_specs=pl.BlockSpec((tq, gps * Q_PER_KV * HEAD_DIM), lambda bi, g, i: (bi * nt + i, g)),
        out_shape=jax.ShapeDtypeStruct((b * t, ATT_WIDTH), F32),
        scratch_shapes=[pltpu.VMEM((LANES, tq), F32)],
        compiler_params=_cparams(("parallel", "parallel", "arbitrary")),
        name="attn_prompt",
    )(qa, kca, vct, kas, vts, kaw, vtw, z)


def _rglru_kernel(x_ref, hist_ref, h0_ref, cw_ref, cb_ref, wr_ref, br_ref, wi_ref, bi_ref, lam_ref,
                  h_ref, hl_ref, cs_ref, carry_ref, tail_ref, a_ref, u_ref, xe_ref, *, tt, nb):
    i = pl.program_id(1)

    @pl.when(i == 0)
    def _():
        carry_ref[...] = h0_ref[...]
        tail_ref[...] = hist_ref[...]

    cw = cw_ref[...]
    nl = -lam_ref[...]
    softplus = jnp.maximum(nl, 0.0) + jnp.log1p(jnp.exp(-jnp.abs(nl)))
    for k in range(nb):
        x = x_ref[k]
        xe_ref[k, 0:8, :] = tail_ref[k]
        xe_ref[k, 8:8 + tt, :] = x
        xc = cb_ref[...]
        for j in range(CONV_W):
            lo = 8 - (CONV_W - 1) + j
            xc = xc + (x if lo == 8 else xe_ref[k, lo:lo + tt, :]) * cw[j:j + 1]
        xb = xc.astype(BF16)

        def gates(w_ref, b_ref):
            parts = [jnp.dot(xb[:, c * MXU_DIM:(c + 1) * MXU_DIM], w_ref[c], preferred_element_type=F32)
                     for c in range(D_RNN // MXU_DIM)]
            z = jnp.concatenate(parts, axis=1) + b_ref[...]
            return 0.5 * jnp.tanh(0.5 * z) + 0.5

        r = gates(wr_ref, br_ref)
        ig = gates(wi_ref, bi_ref)
        log_a = -LRU_C * r * softplus
        a_ref[k] = jnp.exp(log_a)
        th = jnp.tanh(log_a)
        gap = -2.0 * th / (1.0 - th)
        root = jnp.where(gap > 0.0, gap * lax.rsqrt(gap), 0.0)
        u_ref[k] = root * (ig * xc)
        tail_ref[k] = x[tt - 8:tt]

    def step(t, hs):
        out = []
        for k in range(nb):
            h = a_ref[k, pl.ds(t, 1), :] * hs[k] + u_ref[k, pl.ds(t, 1), :]
            h_ref[k, pl.ds(t, 1), :] = h
            out.append(h)
        return tuple(out)

    hs = lax.fori_loop(0, tt, step, tuple(carry_ref[k] for k in range(nb)), unroll=8)
    for k in range(nb):
        carry_ref[k] = hs[k]

    @pl.when(i == pl.num_programs(1) - 1)
    def _():
        hl_ref[...] = carry_ref[...]
        cs_ref[...] = tail_ref[...]


def _rglru(z, b, t, hist8, h0, conv_w, conv_b, wr_bd, b_rg, wi_bd, b_ig, lam, tt, nb):
    nt = t // tt

    def const(shape):
        return pl.BlockSpec(shape, lambda gb, i: tuple(0 for _ in shape))

    def per_seq(rows):
        return pl.BlockSpec((nb, rows, D_RNN), lambda gb, i: (gb, 0, 0))

    row = lambda v: v.reshape(1, D_RNN)
    h, h_last, conv_tail = pl.pallas_call(
        functools.partial(_rglru_kernel, tt=tt, nb=nb),
        grid=(b // nb, nt),
        in_specs=[pl.BlockSpec((nb, tt, D_RNN), lambda gb, i: (gb, i, C_XRNN // D_RNN)),
                  per_seq(8), per_seq(1),
                  const((CONV_W, D_RNN)), const((1, D_RNN)),
                  const((D_RNN // MXU_DIM, MXU_DIM, MXU_DIM)), const((1, D_RNN)),
                  const((D_RNN // MXU_DIM, MXU_DIM, MXU_DIM)), const((1, D_RNN)),
                  const((1, D_RNN))],
        out_specs=[pl.BlockSpec((nb, tt, D_RNN), lambda gb, i: (gb, i, 0)), per_seq(1), per_seq(8)],
        out_shape=[jax.ShapeDtypeStruct((b, t, D_RNN), F32),
                   jax.ShapeDtypeStruct((b, 1, D_RNN), F32),
                   jax.ShapeDtypeStruct((b, 8, D_RNN), F32)],
        scratch_shapes=[pltpu.VMEM((nb, 1, D_RNN), F32), pltpu.VMEM((nb, 8, D_RNN), F32),
                        pltpu.VMEM((nb, tt, D_RNN), F32), pltpu.VMEM((nb, tt, D_RNN), F32),
                        pltpu.VMEM((nb, tt + 8, D_RNN), F32)],
        compiler_params=_cparams(("parallel", "arbitrary")),
        name="rglru",
    )(z.reshape(b, t, N_PACK), hist8, h0.reshape(b, 1, D_RNN), conv_w, row(conv_b),
      wr_bd, row(b_rg), wi_bd, row(b_ig), row(lam))
    return h.reshape(b * t, D_RNN), h_last, conv_tail


def _out_kernel(x_ref, oatt_ref, gatt_ref, h_ref, grnn_ref, gma_ref, gmr_ref, wa_ref, wr_ref, wo_ref, y_ref):
    a = (oatt_ref[...] * jax.nn.silu(gatt_ref[...])).astype(BF16)
    u_att = jnp.dot(a, wa_ref[...], preferred_element_type=F32)
    r = (h_ref[...] * jax.nn.silu(grnn_ref[...])).astype(BF16)
    u_rnn = jnp.dot(r, wr_ref[...], preferred_element_type=F32)
    m = jax.nn.sigmoid(gma_ref[...]) * u_att + jax.nn.sigmoid(gmr_ref[...]) * u_rnn
    y_ref[...] = x_ref[...] + jnp.dot(m.astype(BF16), wo_ref[...], preferred_element_type=F32)


def _out_proj(x2d, z, o_att, h_rnn, wa, wr, wo, tm):
    m = x2d.shape[0]

    def zspec(width, col):
        return pl.BlockSpec((tm, width), lambda i, c=col // width: (i, c))

    def wspec(shape):
        return pl.BlockSpec(shape, lambda i: (0, 0), pipeline_mode=pl.Buffered(1))

    rows = lambda width: pl.BlockSpec((tm, width), lambda i: (i, 0))
    return pl.pallas_call(
        _out_kernel,
        grid=(m // tm,),
        in_specs=[rows(D_MODEL), rows(ATT_WIDTH), zspec(ATT_WIDTH, C_GATT), rows(D_RNN), zspec(D_RNN, C_GRNN),
                  zspec(D_MODEL, C_GMATT), zspec(D_MODEL, C_GMRNN),
                  wspec((ATT_WIDTH, D_MODEL)), wspec((D_RNN, D_MODEL)), wspec((D_MODEL, D_MODEL))],
        out_specs=rows(D_MODEL),
        out_shape=jax.ShapeDtypeStruct((m, D_MODEL), F32),
        compiler_params=_cparams(("parallel",)),
        name="out_proj",
    )(x2d, o_att, z, h_rnn, z, z, z, wa, wr, wo)


def _prep_sample_kernel(zq_ref, zc_ref, zs_ref, zw_ref, qg_ref, ksg_ref, kwg_ref, bd_ref,
                        qn_ref, lc_ref, ls_ref, lw_ref):
    bd = bd_ref[...]
    qn_ref[...] = _seg_rms(zq_ref[...], bd) * qg_ref[...] * SCALE
    lc_ref[...] = zc_ref[...]
    for z_ref, g_ref, l_ref in ((zs_ref, ksg_ref, ls_ref), (zw_ref, kwg_ref, lw_ref)):
        z = z_ref[...]
        l_ref[:, :KV_WIDTH] = _seg_rms(z[:, :KV_WIDTH], bd) * g_ref[...]
        l_ref[:, KV_WIDTH:] = z[:, KV_WIDTH:]


def _prep_sample(z, q_gain, ks_gain, kw_gain, bd):
    m = z.shape[0]

    def zspec(width, col):
        return pl.BlockSpec((m, width), lambda i, c=col // width: (0, c))

    def const(shape):
        return pl.BlockSpec(shape, lambda i: tuple(0 for _ in shape))

    full = lambda width: pl.BlockSpec((m, width), lambda i: (0, 0))
    leaf = jax.ShapeDtypeStruct((m, 2 * KV_WIDTH), F32)
    return pl.pallas_call(
        _prep_sample_kernel,
        grid=(1,),
        in_specs=[zspec(ATT_WIDTH, C_Q), zspec(512, C_KV), zspec(512, C_KV + 512), zspec(512, C_KV + 1024),
                  const((1, ATT_WIDTH)), const((1, KV_WIDTH)), const((1, KV_WIDTH)), const((MXU_DIM, MXU_DIM))],
        out_specs=[full(ATT_WIDTH), full(512), full(512), full(512)],
        out_shape=[jax.ShapeDtypeStruct((m, ATT_WIDTH), F32), leaf, leaf, leaf],
        compiler_params=_cparams(("arbitrary",)),
        name="prep_sample",
    )(z, z, z, z, q_gain, ks_gain, kw_gain, bd)


def _page_specs(n):
    return [pl.BlockSpec((1, 2 * KV_WIDTH, PAGE_SIZE),
                         lambda bi, s, pt, k=k, n=n: (pt[bi, s * n + k], 0, 0)) for k in range(n)]


def _compress_pages_kernel(pt_ref, *refs):
    del pt_ref
    pages = refs[:CMP_PAGES]
    wt_ref, seg_ref, ok_ref, ov_ref = refs[CMP_PAGES:]
    wt = wt_ref[...]
    for tile in range(CMP_PAGES // CMP_TILE_PAGES):
        acc_k = jnp.zeros((KV_WIDTH, LANES), F32)
        acc_v = jnp.zeros((KV_WIDTH, LANES), F32)
        for pair in range(CMP_TILE_PAGES // 2):
            first = tile * CMP_TILE_PAGES + 2 * pair
            pa = pages[first][0] * wt
            pb = pages[first + 1][0] * wt
            seg = seg_ref[pair]
            lhs_k = jnp.concatenate([pa[:KV_WIDTH], pb[:KV_WIDTH]], axis=1).astype(BF16)
            lhs_v = jnp.concatenate([pa[KV_WIDTH:], pb[KV_WIDTH:]], axis=1).astype(BF16)
            acc_k = acc_k + jnp.dot(lhs_k, seg, preferred_element_type=F32)
            acc_v = acc_v + jnp.dot(lhs_v, seg, preferred_element_type=F32)
        ok_ref[0, :, tile * LANES:(tile + 1) * LANES] = acc_k
        ov_ref[0, :, tile * LANES:(tile + 1) * LANES] = acc_v


def _compress_pages(page_table, pool_t, wt, seg):
    b, n_pages = page_table.shape
    ns = n_pages // CMP_PAGES
    cols = CMP_PAGES // CMP_TILE_PAGES * LANES
    out_spec = pl.BlockSpec((1, KV_WIDTH, cols), lambda bi, s, pt: (bi, 0, s))
    grid_spec = pltpu.PrefetchScalarGridSpec(
        num_scalar_prefetch=1,
        grid=(b, ns),
        in_specs=_page_specs(CMP_PAGES) + [
            pl.BlockSpec((2 * KV_WIDTH, PAGE_SIZE), lambda bi, s, pt: (0, 0)),
            pl.BlockSpec((CMP_TILE_PAGES // 2, 2 * PAGE_SIZE, LANES), lambda bi, s, pt: (0, 0, 0))],
        out_specs=[out_spec, out_spec],
    )
    shape = jax.ShapeDtypeStruct((b, KV_WIDTH, ns * cols), F32)
    return pl.pallas_call(
        _compress_pages_kernel,
        grid_spec=grid_spec,
        out_shape=[shape, shape],
        compiler_params=_cparams(("parallel", "arbitrary")),
        name="compress_pages",
    )(page_table, *([pool_t] * CMP_PAGES), wt, seg)


def _softmax_lanes(parts, masks):
    parts = [jnp.where(mk, s, NEG) for s, mk in zip(parts, masks)]
    mx = functools.reduce(jnp.maximum, [jnp.max(s, axis=1, keepdims=True) for s in parts])
    es = [jnp.where(mk, jnp.exp(s - mx), 0.0) for s, mk in zip(parts, masks)]
    den = jnp.maximum(sum(jnp.sum(e, axis=1, keepdims=True) for e in es), TINY)
    return [e / den for e in es]


def _sample_cmp_win_kernel(qbd_ref, ck_ref, cv_ref, kg_ref, cwin_ref, wnew_ref,
                           ocmp_ref, owin_ref, bias_ref, wst_ref, *, past_len, t_new):
    qbd = qbd_ref[0]
    rows = qbd.shape[0]
    ck = ck_ref[0]
    n_cmp = ck.shape[1]
    parts = []
    for g in range(KV_HEADS):
        xs = ck[g * HEAD_DIM:(g + 1) * HEAD_DIM]
        parts.append(xs * lax.rsqrt(jnp.mean(xs * xs, axis=0, keepdims=True) + EPS))
    kc = (jnp.concatenate(parts, axis=0) * kg_ref[...]).astype(BF16)

    t_row = _row_iota((rows, n_cmp)) % t_new
    n_lane = _lane_iota((rows, n_cmp))
    cmp_id = LANES * (n_lane // LANES) + 2 * (n_lane % (LANES // 2)) + (n_lane % LANES) // (LANES // 2)
    vis = (cmp_id + 1) * CMP_BLOCK - 1 <= past_len + t_row
    (p,) = _softmax_lanes([jnp.dot(qbd, kc, preferred_element_type=F32)], [vis])
    ocmp_ref[0] = _dot_nt(p.astype(BF16), cv_ref[0].astype(BF16))

    gt = KV_HEADS * t_new
    psum = p[0:gt]
    for r in range(1, Q_PER_KV):
        psum = psum + p[r * gt:(r + 1) * gt]
    halves = []
    for c in range(n_cmp // LANES):
        tile = psum[:, c * LANES:(c + 1) * LANES]
        halves.append(tile + pltpu.roll(tile, LANES // 2, 1))
    if len(halves) == 1:
        imp = halves[0][:, :LANES // 2]
    else:
        low = _lane_iota((gt, LANES)) < LANES // 2
        imp = jnp.concatenate([jnp.where(low, halves[c], halves[c + 1]) for c in range(0, len(halves), 2)], axis=1)
    n_blk = n_cmp // 2
    j = _lane_iota((gt, n_blk))
    cur = (past_len + _row_iota((gt, n_blk)) % t_new) // SEL_BLOCK
    cand = j < cur
    forced = (j == 0) | (j == cur - 1)
    score = jnp.where(cand, jnp.where(forced, jnp.inf, imp), -jnp.inf)
    width = min(LANES, n_blk)
    tiles = [score[:, k * width:(k + 1) * width] for k in range(n_blk // width)]
    counts = [jnp.zeros((gt, width), jnp.int32) for _ in tiles]
    lane_w = _lane_iota((gt, width))
    for c in range(n_blk):
        col = score[:, c:c + 1]
        for k, tile in enumerate(tiles):
            if k < c // width:
                inc = jnp.where(col > tile, 1, 0)
            elif k > c // width:
                inc = jnp.where(col >= tile, 1, 0)
            else:
                inc = jnp.where(lane_w > c % width, jnp.where(col >= tile, 1, 0), jnp.where(col > tile, 1, 0))
            counts[k] = counts[k] + inc
    count = counts[0] if len(counts) == 1 else jnp.concatenate(counts, axis=1)
    keep = (count < N_SEL - 1) & cand
    bias_ref[0] = jnp.where(keep, 0.0, SEL_BIAS)

    cwin = cwin_ref[0]
    wb = cwin.shape[1]
    wnew = wnew_ref[0]
    wnew_p = jnp.concatenate([wnew, jnp.zeros((LANES - t_new, 2 * KV_WIDTH), F32)], axis=0)
    t_w = _row_iota((rows, wb)) % t_new
    idx = _lane_iota((rows, wb))
    ok_w = (idx <= wb + t_w) & (idx > wb + t_w - WINDOW)
    t_n = _row_iota((rows, LANES)) % t_new
    idx_n = wb + _lane_iota((rows, LANES))
    ok_n = (idx_n <= wb + t_n) & (idx_n > wb + t_n - WINDOW) & (_lane_iota((rows, LANES)) < t_new)
    p_w, p_n = _softmax_lanes([jnp.dot(qbd, cwin[:KV_WIDTH].astype(BF16), preferred_element_type=F32),
                               _dot_nt(qbd, wnew_p[:, :KV_WIDTH].astype(BF16))], [ok_w, ok_n])
    owin_ref[0] = (_dot_nt(p_w.astype(BF16), cwin[KV_WIDTH:].astype(BF16))
                   + jnp.dot(p_n.astype(BF16), wnew_p[:, KV_WIDTH:].astype(BF16), preferred_element_type=F32))

    keep_lanes = LANES - t_new
    new_t = jnp.concatenate([jnp.zeros((keep_lanes, 2 * KV_WIDTH), F32), wnew], axis=0).T
    rolled = [pltpu.roll(cwin[:, c * LANES:(c + 1) * LANES], keep_lanes, 1) for c in range(wb // LANES)]
    rolled.append(new_t)
    first = _lane_iota((2 * KV_WIDTH, LANES)) < keep_lanes
    for c in range(wb // LANES):
        wst_ref[0, :, c * LANES:(c + 1) * LANES] = jnp.where(first, rolled[c], rolled[c + 1])


def _sample_cmp_win(qbd, ck_t, cv_t, kc_gain_col, cwin_t, wnew, past_len):
    b, rows, _ = qbd.shape
    n_cmp = ck_t.shape[2]
    wb = cwin_t.shape[2]
    t_new = wnew.shape[1]
    gt = KV_HEADS * t_new

    def per_b(shape):
        return pl.BlockSpec((1,) + shape, lambda bi: (bi, 0, 0))

    return pl.pallas_call(
        functools.partial(_sample_cmp_win_kernel, past_len=past_len, t_new=t_new),
        grid=(b,),
        in_specs=[per_b((rows, MXU_DIM)), per_b((KV_WIDTH, n_cmp)), per_b((KV_WIDTH, n_cmp)),
                  pl.BlockSpec((KV_WIDTH, 1), lambda bi: (0, 0)),
                  per_b((2 * KV_WIDTH, wb)), per_b((t_new, 2 * KV_WIDTH))],
        out_specs=[per_b((rows, MXU_DIM)), per_b((rows, MXU_DIM)), per_b((gt, n_cmp // 2)), per_b((2 * KV_WIDTH, wb))],
        out_shape=[jax.ShapeDtypeStruct((b, rows, MXU_DIM), F32), jax.ShapeDtypeStruct((b, rows, MXU_DIM), F32),
                   jax.ShapeDtypeStruct((b, gt, n_cmp // 2), F32), jax.ShapeDtypeStruct((b, 2 * KV_WIDTH, wb), F32)],
        compiler_params=_cparams(("parallel",)),
        name="sample_cmp_win",
    )(qbd, ck_t, cv_t, kc_gain_col, cwin_t, wnew)


def _sample_sel_kernel(pt_ref, *refs, t_new):
    del pt_ref
    pages = refs[:SEL_PAGES]
    (qbd_ref, bsel_ref, onehot_ref, snew_ref, ocmp_ref, owin_ref, gl_ref,
     o_ref, m_ref, l_ref, acc_ref) = refs[SEL_PAGES:]
    s_idx = pl.program_id(1)
    qbd = qbd_ref[0]
    rows = qbd.shape[0]

    @pl.when(s_idx == 0)
    def _():
        m_ref[...] = jnp.full(m_ref.shape, NEG, F32)
        l_ref[...] = jnp.zeros(l_ref.shape, F32)
        acc_ref[...] = jnp.zeros(acc_ref.shape, F32)

    def partial_softmax(s, pv):
        m = jnp.max(s, axis=1, keepdims=True)
        pe = jnp.exp(s - m)
        return m, jnp.sum(pe, axis=1, keepdims=True), pv(pe.astype(BF16))

    def update(parts):
        m_old = m_ref[...]
        m_new = functools.reduce(jnp.maximum, [p[0] for p in parts], m_old)
        scale = jnp.exp(m_old - m_new)
        l = scale * l_ref[...]
        acc = scale * acc_ref[...]
        for m, l_part, acc_part in parts:
            c = jnp.exp(m - m_new)
            l = l + c * l_part
            acc = acc + c * acc_part
        m_ref[...] = m_new
        l_ref[...] = l
        acc_ref[...] = acc

    bsel = bsel_ref[0, 0]
    parts = []
    for g in range(SEL_PAGES // SEL_SUB):
        pgs = pages[g * SEL_SUB:(g + 1) * SEL_SUB]
        keys = SEL_SUB * PAGE_SIZE
        k_t = jnp.concatenate([pg[0, :KV_WIDTH, :].astype(BF16) for pg in pgs], axis=1)
        v_t = jnp.concatenate([pg[0, KV_WIDTH:, :].astype(BF16) for pg in pgs], axis=1)
        s = (jnp.dot(qbd, k_t, preferred_element_type=F32)
             + jnp.dot(bsel, onehot_ref[:, g * keys:(g + 1) * keys], preferred_element_type=F32))
        parts.append(partial_softmax(s, lambda pb, v_t=v_t: _dot_nt(pb, v_t)))
    update(parts)

    @pl.when(s_idx == pl.num_programs(1) - 1)
    def _():
        snew = jnp.concatenate([snew_ref[0], jnp.zeros((LANES - t_new, 2 * KV_WIDTH), F32)], axis=0)
        t_q = _row_iota((rows, LANES)) % t_new
        t_k = _lane_iota((rows, LANES))
        s = jnp.where((t_k <= t_q) & (t_k < t_new), _dot_nt(qbd, snew[:, :KV_WIDTH].astype(BF16)), NEG)
        v_new = snew[:, KV_WIDTH:].astype(BF16)
        update([partial_softmax(s, lambda pb: jnp.dot(pb, v_new, preferred_element_type=F32))])
        o_sel = acc_ref[...] / l_ref[...]
        sig = jax.nn.sigmoid(gl_ref[0])
        o_ref[0] = sig[:, 0:1] * ocmp_ref[0] + sig[:, 1:2] * o_sel + sig[:, 2:3] * owin_ref[0]


def _sample_sel(page_table, pool_t, qbd, bsel, onehot, snew, ocmp, owin, gl):
    b, n_pages = page_table.shape
    ns = n_pages // SEL_PAGES
    rows = qbd.shape[1]
    t_new = snew.shape[1]
    keys = SEL_PAGES * PAGE_SIZE

    def per_b(shape):
        return pl.BlockSpec((1,) + shape, lambda bi, s, pt: (bi,) + tuple(0 for _ in shape))

    grid_spec = pltpu.PrefetchScalarGridSpec(
        num_scalar_prefetch=1,
        grid=(b, ns),
        in_specs=_page_specs(SEL_PAGES) + [
            per_b((rows, MXU_DIM)),
            pl.BlockSpec((1, 1, rows, LANES), lambda bi, s, pt: (bi, s, 0, 0)),
            pl.BlockSpec((LANES, keys), lambda bi, s, pt: (0, 0)),
            per_b((t_new, 2 * KV_WIDTH)), per_b((rows, MXU_DIM)), per_b((rows, MXU_DIM)), per_b((rows, LANES))],
        out_specs=per_b((rows, MXU_DIM)),
        scratch_shapes=[pltpu.VMEM((rows, 1), F32), pltpu.VMEM((rows, 1), F32), pltpu.VMEM((rows, MXU_DIM), F32)],
    )
    return pl.pallas_call(
        functools.partial(_sample_sel_kernel, t_new=t_new),
        grid_spec=grid_spec,
        out_shape=jax.ShapeDtypeStruct((b, rows, MXU_DIM), F32),
        compiler_params=_cparams(("parallel", "arbitrary")),
        name="sample_sel",
    )(page_table, *([pool_t] * SEL_PAGES), qbd, bsel, onehot, snew, ocmp, owin, gl)


def _pack_weights(w_in, q_norm_g, k_norm_g, w_cmp, w_rg, w_ig, w_att_out, w_rnn_out, w_out):
    def block_diag(w):
        per = MXU_DIM // HEAD_DIM
        w4 = w.reshape(RNN_BLOCKS // per, per, HEAD_DIM, HEAD_DIM)
        eye = jnp.eye(per, dtype=w.dtype)
        return jnp.einsum('cpde,pq->cpdqe', w4, eye).reshape(RNN_BLOCKS // per, MXU_DIM, MXU_DIM).astype(BF16)

    wt = jnp.broadcast_to(w_cmp.transpose(1, 2, 0)[:, None], (2, KV_HEADS, HEAD_DIM, CMP_BLOCK))
    wt = jnp.tile(wt, (1, 1, 1, PAGE_SIZE // CMP_BLOCK)).reshape(2 * KV_WIDTH, PAGE_SIZE)
    lane = jnp.arange(2 * PAGE_SIZE)
    pair = jnp.arange(CMP_TILE_PAGES // 2)
    n_local = ((PAGE_SIZE // CMP_BLOCK) * (2 * pair[:, None] + lane[None, :] // PAGE_SIZE)
               + (lane[None, :] % PAGE_SIZE) // CMP_BLOCK)
    col = (n_local % 2) * (LANES // 2) + n_local // 2
    seg = (jnp.arange(LANES)[None, None, :] == col[:, :, None]).astype(BF16)

    return dict(
        w_t=w_in.T,
        q_gain=jnp.tile(q_norm_g, N_HEADS).reshape(1, ATT_WIDTH),
        kc_gain=jnp.tile(k_norm_g[0], KV_HEADS).reshape(1, KV_WIDTH),
        ks_gain=jnp.tile(k_norm_g[1], KV_HEADS).reshape(1, KV_WIDTH),
        kw_gain=jnp.tile(k_norm_g[2], KV_HEADS).reshape(1, KV_WIDTH),
        wc=jnp.broadcast_to(w_cmp[:, :, None, :], (CMP_BLOCK, 2, KV_HEADS, HEAD_DIM)).reshape(CMP_BLOCK, 2 * KV_WIDTH),
        wt=wt, seg=seg,
        wr_bd=block_diag(w_rg), wi_bd=block_diag(w_ig),
        wa=w_att_out.astype(BF16), wr=w_rnn_out.astype(BF16), wo=w_out.astype(BF16),
        bd=_block_diag_ones(),
    )


def _prompt_layer(x, pw, norm_g, conv_w, conv_b, b_rg, b_ig, lam):
    b, t, _ = x.shape
    tq = WINDOW // 2
    x2d = x.reshape(b * t, D_MODEL)
    z = _proj(x2d, norm_g, pw['w_t'], tm=min(1024, b * t))
    qa, leaf_c, leaf_s, leaf_w, cc, kas, vts, kaw, vtw = _prep_prompt(
        z, b, t, pw['q_gain'], pw['ks_gain'], pw['kw_gain'], pw['wc'], pw['bd'], te=tq)
    n_cmp = t // CMP_BLOCK
    cc_perm = cc.reshape(b, n_cmp // 2, 2, 2 * KV_WIDTH).transpose(0, 2, 1, 3).reshape(b, n_cmp, 2 * KV_WIDTH)
    kca, vct = _cmp_prep(cc_perm, pw['kc_gain'], pw['bd'])
    o_att = _attn_prompt(z, qa, kca, vct, kas, vts, kaw, vtw, b, t, tq)
    h_rnn, h_last, conv_tail = _rglru(z, b, t, jnp.zeros((b, 8, D_RNN), F32), jnp.zeros((b, D_RNN), F32),
                                      conv_w, conv_b, pw['wr_bd'], b_rg, pw['wi_bd'], b_ig, lam, tt=256, nb=b)
    y = _out_proj(x2d, z, o_att, h_rnn, pw['wa'], pw['wr'], pw['wo'], tm=256)
    def kv(leaf_t):
        tokens = leaf_t.shape[2]
        return leaf_t.reshape(b, 2, KV_HEADS, HEAD_DIM, tokens).transpose(0, 4, 1, 2, 3)

    w_keep = min(WINDOW, t)
    return y.reshape(b, t, D_MODEL), (kv(leaf_c), kv(leaf_s), kv(leaf_w[:, :, t - w_keep:]),
                                      h_last.reshape(b, D_RNN), conv_tail[:, 8 - (CONV_W - 1):])


def _feature_major(cache):
    n, tokens = cache.shape[:2]
    return cache.transpose(0, 2, 3, 4, 1).reshape(n, 2 * KV_WIDTH, tokens)


def _sample_layer(x, cache_cmp, cache_sel, cache_win, state_h, state_conv, page_table, pw,
                  norm_g, conv_w, conv_b, b_rg, b_ig, lam):
    b, t, _ = x.shape
    n_pages = page_table.shape[1]
    past_len = n_pages * PAGE_SIZE
    rows = N_HEADS * t
    x2d = x.reshape(b * t, D_MODEL)
    z = _proj(x2d, norm_g, pw['w_t'], tm=b * t)
    qn, leaf_c, leaf_s, leaf_w = _prep_sample(z, pw['q_gain'], pw['ks_gain'], pw['kw_gain'], pw['bd'])

    q5 = qn.reshape(b, t, KV_HEADS, Q_PER_KV, HEAD_DIM).transpose(0, 3, 2, 1, 4)
    qbd = (q5[:, :, :, :, None, :] * jnp.eye(KV_HEADS, dtype=F32)[None, None, :, None, :, None])
    qbd = qbd.reshape(b, rows, KV_WIDTH).astype(BF16)

    ck_t, cv_t = _compress_pages(page_table, _feature_major(cache_cmp), pw['wt'], pw['seg'])
    wb = cache_win.shape[1]
    o_cmp, o_win, bias, win_state_t = _sample_cmp_win(
        qbd, ck_t, cv_t, pw['kc_gain'].reshape(KV_WIDTH, 1), _feature_major(cache_win),
        leaf_w.reshape(b, t, 2 * KV_WIDTH), past_len)
    win_state = win_state_t.reshape(b, 2, KV_HEADS, HEAD_DIM, wb).transpose(0, 4, 1, 2, 3)

    ns = n_pages // SEL_PAGES
    blocks_per_step = SEL_PAGES * PAGE_SIZE // SEL_BLOCK
    bsel = bias.reshape(b, KV_HEADS * t, ns, blocks_per_step).transpose(0, 2, 1, 3)
    bsel = jnp.tile(bsel, (1, 1, Q_PER_KV, 1))
    bsel = jnp.pad(bsel, ((0, 0), (0, 0), (0, 0), (0, LANES - blocks_per_step))).astype(BF16)
    key_block = jnp.arange(SEL_PAGES * PAGE_SIZE) // SEL_BLOCK
    onehot = (jnp.arange(LANES)[:, None] == key_block[None, :]).astype(BF16)
    gl = z[:, C_GNSA:C_GNSA + 3 * N_HEADS].reshape(b, t, 3, KV_HEADS, Q_PER_KV)
    gl = jnp.pad(gl.transpose(0, 4, 3, 1, 2).reshape(b, rows, 3), ((0, 0), (0, 0), (0, LANES - 3)))
    o_full = _sample_sel(page_table, _feature_major(cache_sel), qbd, bsel, onehot,
                         leaf_s.reshape(b, t, 2 * KV_WIDTH), o_cmp, o_win, gl)
    o6 = o_full.reshape(b, Q_PER_KV, KV_HEADS, t, KV_HEADS, HEAD_DIM)
    o_att = jnp.stack([o6[:, :, g, :, g, :] for g in range(KV_HEADS)], axis=2)
    o_att = o_att.transpose(0, 3, 2, 1, 4).reshape(b * t, ATT_WIDTH)

    hist8 = jnp.pad(state_conv, ((0, 0), (8 - (CONV_W - 1), 0), (0, 0)))
    h_rnn, h_last, conv_tail = _rglru(z, b, t, hist8, state_h, conv_w, conv_b,
                                      pw['wr_bd'], b_rg, pw['wi_bd'], b_ig, lam, tt=t,
                                      nb=8 if b % 8 == 0 else b)
    y = _out_proj(x2d, z, o_att, h_rnn, pw['wa'], pw['wr'], pw['wo'], tm=min(256, b * t))
    kv = lambda leaf: leaf.reshape(b, -1, 2, KV_HEADS, HEAD_DIM)
    return y.reshape(b, t, D_MODEL), (kv(leaf_c), kv(leaf_s), win_state,
                                      h_last.reshape(b, D_RNN), conv_tail[:, 8 - (CONV_W - 1):])


def kernel(x_prompt, x_sample, cache_cmp, cache_sel, cache_win, state_h, state_conv, page_table,
           norm_g, w_in, q_norm_g, k_norm_g, w_cmp, conv_w, conv_b, w_rg, b_rg, w_ig, b_ig,
           lru_lambda, w_att_out, w_rnn_out, w_out):
    yp, ys = x_prompt, x_sample
    outs_p, outs_s = [], []
    for l in range(w_in.shape[0]):
        pw = _pack_weights(w_in[l], q_norm_g[l], k_norm_g[l], w_cmp[l], w_rg[l], w_ig[l],
                           w_att_out[l], w_rnn_out[l], w_out[l])
        yp, st_p = _prompt_layer(yp, pw, norm_g[l], conv_w[l], conv_b[l], b_rg[l], b_ig[l], lru_lambda[l])
        ys, st_s = _sample_layer(ys, cache_cmp[l], cache_sel[l], cache_win[l], state_h[l], state_conv[l],
                                 page_table, pw, norm_g[l], conv_w[l], conv_b[l], b_rg[l], b_ig[l], lru_lambda[l])
        outs_p.append(st_p)
        outs_s.append(st_s)
    cmp_p, sel_p, win_p, h_p, conv_p = [jnp.stack(a) for a in zip(*outs_p)]
    cmp_s, sel_s, win_s, h_s, conv_s = [jnp.stack(a) for a in zip(*outs_s)]
    return (yp, ys, cmp_p, sel_p, win_p, h_p, conv_p, cmp_s, sel_s, win_s, h_s, conv_s)
```

```python
import functools

import jax
import jax.numpy as jnp
from jax import lax
from jax.experimental import pallas as pl
from jax.experimental.pallas import tpu as pltpu

F32 = jnp.float32
BF16 = jnp.bfloat16

D_MODEL = 2048
N_HEADS = 16
HEAD_DIM = 64
KV_HEADS = 4
Q_PER_KV = N_HEADS // KV_HEADS
ATT_WIDTH = N_HEADS * HEAD_DIM
KV_WIDTH = KV_HEADS * HEAD_DIM
CMP_BLOCK = 32
SEL_BLOCK = 64
N_SEL = 16
WINDOW = 512
SCALE = HEAD_DIM ** -0.5
LOG2E = 1.4426950408889634
D_RNN = 1024
RNN_BLOCKS = 16
CONV_W = 4
LRU_C = 8.0
PAGE_SIZE = 128
EPS = 1e-6
NEG = -1e30
TINY = 1e-30
SEL_BIAS = -1e30

LANES = 128
MXU_DIM = 256
VMEM_LIMIT = 56 * 1024 * 1024

C_Q = 0
C_GATT = 1024
C_XRNN = 2048
C_GRNN = 3072
C_GMATT = 4096
C_GMRNN = 6144
C_KV = 8192
C_GNSA = 9728
N_PACK = 10240
PROJ_TILE = 1024
_W_Q, _W_KV, _W_GNSA, _W_GATT, _W_XRNN, _W_GRNN, _W_GMATT, _W_GMRNN = 0, 1024, 2560, 2608, 3632, 4656, 5680, 7728
PROJ_ROW_STARTS = (_W_Q, _W_GATT, _W_XRNN, _W_GRNN, _W_GMATT, _W_GMATT + 1024, _W_GMRNN, _W_GMRNN + 1024,
                   _W_KV, _W_KV + 1024)

SEL_PAGES = 64
SEL_SUB = 16
CMP_PAGES = 64
CMP_TILE_PAGES = 32
GROUPS_PER_STEP = 2
ATTN_TILE = 256
SEL_CHUNKS = 2
VT_ROWS = 80


def _cparams(sem):
    return pltpu.CompilerParams(dimension_semantics=sem, vmem_limit_bytes=VMEM_LIMIT)


def _proj_kernel(starts_ref, x_ref, g_ref, w_ref, o_ref, xn_ref):
    del starts_ref
    @pl.when(pl.program_id(1) == 0)
    def _():
        x = x_ref[...]
        ms = jnp.mean(x * x, axis=-1, keepdims=True)
        xn_ref[...] = (x * lax.rsqrt(ms + EPS) * g_ref[...]).astype(BF16)

    o_ref[...] = lax.dot_general(xn_ref[...], w_ref[...].astype(BF16), (((1,), (1,)), ((), ())),
                                 preferred_element_type=F32)


def _proj(x2d, norm_g, w_t, tm):
    m = x2d.shape[0]
    tn = PROJ_TILE
    grid_spec = pltpu.PrefetchScalarGridSpec(
        num_scalar_prefetch=1,
        grid=(m // tm, N_PACK // tn),
        in_specs=[
            pl.BlockSpec((tm, D_MODEL), lambda i, j, st: (i, 0)),
            pl.BlockSpec((1, D_MODEL), lambda i, j, st: (0, 0)),
            pl.BlockSpec((pl.Element(tn), pl.Element(D_MODEL)), lambda i, j, st: (pl.multiple_of(st[j], 16), 0)),
        ],
        out_specs=pl.BlockSpec((tm, tn), lambda i, j, st: (i, j)),
        scratch_shapes=[pltpu.VMEM((tm, D_MODEL), BF16)],
    )
    return pl.pallas_call(
        _proj_kernel,
        grid_spec=grid_spec,
        out_shape=jax.ShapeDtypeStruct((m, N_PACK), F32),
        compiler_params=_cparams(("parallel", "arbitrary")),
        name="proj",
    )(jnp.asarray(PROJ_ROW_STARTS, jnp.int32), x2d, norm_g.reshape(1, D_MODEL), w_t)


def _lane_iota(shape):
    return lax.broadcasted_iota(jnp.int32, shape, len(shape) - 1)


def _row_iota(shape):
    return lax.broadcasted_iota(jnp.int32, shape, 0)


def _seg_rms(x, bd):
    outs = []
    for c in range(x.shape[1] // MXU_DIM):
        xc = x[:, c * MXU_DIM:(c + 1) * MXU_DIM]
        x2 = xc * xc
        hi = x2.astype(BF16)
        lo = (x2 - hi.astype(F32)).astype(BF16)
        ss = (jnp.dot(hi, bd, preferred_element_type=F32)
              + jnp.dot(lo, bd, preferred_element_type=F32))
        outs.append(xc * lax.rsqrt(ss * (1.0 / HEAD_DIM) + EPS))
    return outs[0] if len(outs) == 1 else jnp.concatenate(outs, axis=1)


def _head_lo(x, h):
    tile = x[:, (h // 2) * LANES:(h // 2 + 1) * LANES]
    if h % 2:
        tile = pltpu.roll(tile, HEAD_DIM, 1)
    return jnp.where(_lane_iota(tile.shape) < HEAD_DIM, tile, 0.0)


def _heads_t(x):
    outs = []
    for c in range(x.shape[1] // LANES):
        xt = x[:, c * LANES:(c + 1) * LANES].T
        outs += [xt[:HEAD_DIM], xt[HEAD_DIM:]]
    return outs


def _dot_nt(a, b):
    return lax.dot_general(a, b, (((1,), (1,)), ((), ())), preferred_element_type=F32)


def _block_diag_ones():
    r = jnp.arange(MXU_DIM) // HEAD_DIM
    return (r[:, None] == r[None, :]).astype(BF16)


def _prep_prompt_kernel(zq_ref, zc_ref, zs_ref, zw_ref, qg_ref, ksg_ref, kwg_ref, wc_ref, bd_ref,
                        qa_ref, lc_ref, ls_ref, lw_ref, cc_ref, kas_ref, vts_ref, kaw_ref, vtw_ref, *, te, tq):
    i = pl.program_id(1)
    bd = bd_ref[...]
    qn = _seg_rms(zq_ref[...], bd) * qg_ref[...] * (SCALE * LOG2E)
    for h in range(N_HEADS):
        qa_ref[0, h // Q_PER_KV, h % Q_PER_KV] = _head_lo(qn, h).astype(BF16)

    zc = zc_ref[...]
    for c in range(2 * KV_WIDTH // LANES):
        lc_ref[0, c * LANES:(c + 1) * LANES, :] = zc[:, c * LANES:(c + 1) * LANES].T
    cc_ref[0] = jnp.sum(zc.reshape(te // CMP_BLOCK, CMP_BLOCK, 2 * KV_WIDTH) * wc_ref[...][None], axis=1)

    lane = _lane_iota((te, LANES))
    own_block = (i * te + _row_iota((te, LANES))) // SEL_BLOCK
    onehot = jnp.where(lane - HEAD_DIM == own_block, 1.0, 0.0)
    ones_rows = jnp.where(_row_iota((VT_ROWS - HEAD_DIM, te)) == 0, 1.0, 0.0).astype(BF16)

    for z_ref, g_ref, l_ref, ka_ref, vt_ref, with_onehot in (
            (zs_ref, ksg_ref, ls_ref, kas_ref, vts_ref, True),
            (zw_ref, kwg_ref, lw_ref, kaw_ref, vtw_ref, False)):
        z = z_ref[...]
        kn = _seg_rms(z[:, :KV_WIDTH], bd) * g_ref[...]
        v = z[:, KV_WIDTH:]
        for c in range(KV_WIDTH // LANES):
            l_ref[0, c * LANES:(c + 1) * LANES, :] = kn[:, c * LANES:(c + 1) * LANES].T
        for g, vt in enumerate(_heads_t(v)):
            l_ref[0, KV_WIDTH + g * HEAD_DIM:KV_WIDTH + (g + 1) * HEAD_DIM, :] = vt
            ka = _head_lo(kn, g)
            if with_onehot:
                ka = jnp.where(lane < HEAD_DIM, ka, onehot)
            ka_ref[0, g] = ka.astype(BF16)
            vt_aug = jnp.concatenate([vt.astype(BF16), ones_rows], axis=0)
            for c in range(te // tq):
                vt_ref[0, g, c] = vt_aug[:, c * tq:(c + 1) * tq]


def _prep_prompt(z, b, t, q_gain, ks_gain, kw_gain, wc, bd, te, tq):
    nt = t // te

    def zspec(width, col):
        return pl.BlockSpec((te, width), lambda bi, i, c=col // width: (bi * nt + i, c))

    def const(shape):
        return pl.BlockSpec(shape, lambda bi, i: tuple(0 for _ in shape))

    leaf_spec = pl.BlockSpec((1, 2 * KV_WIDTH, te), lambda bi, i: (bi, 0, i))
    leaf_shape = jax.ShapeDtypeStruct((b, 2 * KV_WIDTH, t), F32)
    k_spec = pl.BlockSpec((1, KV_HEADS, te, LANES), lambda bi, i: (bi, 0, i, 0))
    k_shape = jax.ShapeDtypeStruct((b, KV_HEADS, t, LANES), BF16)
    vt_spec = pl.BlockSpec((1, KV_HEADS, te // tq, VT_ROWS, tq), lambda bi, i: (bi, 0, i, 0, 0))
    vt_shape = jax.ShapeDtypeStruct((b, KV_HEADS, t // tq, VT_ROWS, tq), BF16)
    return pl.pallas_call(
        functools.partial(_prep_prompt_kernel, te=te, tq=tq),
        grid=(b, nt),
        in_specs=[zspec(ATT_WIDTH, C_Q), zspec(512, C_KV), zspec(512, C_KV + 512), zspec(512, C_KV + 1024),
                  const((1, ATT_WIDTH)), const((1, KV_WIDTH)), const((1, KV_WIDTH)),
                  const((CMP_BLOCK, 2 * KV_WIDTH)), const((MXU_DIM, MXU_DIM))],
        out_specs=[pl.BlockSpec((1, KV_HEADS, Q_PER_KV, te, LANES), lambda bi, i: (bi, 0, 0, i, 0)),
                   leaf_spec, leaf_spec, leaf_spec,
                   pl.BlockSpec((1, te // CMP_BLOCK, 2 * KV_WIDTH), lambda bi, i: (bi, i, 0)),
                   k_spec, vt_spec, k_spec, vt_spec],
        out_shape=[jax.ShapeDtypeStruct((b, KV_HEADS, Q_PER_KV, t, LANES), BF16),
                   leaf_shape, leaf_shape, leaf_shape,
                   jax.ShapeDtypeStruct((b, t // CMP_BLOCK, 2 * KV_WIDTH), F32),
                   k_shape, vt_shape, k_shape, vt_shape],
        compiler_params=_cparams(("parallel", "parallel")),
        name="prep_prompt",
    )(z, z, z, z, q_gain, ks_gain, kw_gain, wc, bd)


def _cmp_prep_kernel(cc_ref, kg_ref, bd_ref, kca_ref, vct_ref):
    cc = cc_ref[0]
    kc = _seg_rms(cc[:, :KV_WIDTH], bd_ref[...]) * kg_ref[...]
    for g, vt in enumerate(_heads_t(cc[:, KV_WIDTH:])):
        kca_ref[0, g] = _head_lo(kc, g).astype(BF16)
        vct_ref[0, g] = vt.astype(BF16)


def _cmp_prep(cc_perm, kc_gain, bd):
    b, n, _ = cc_perm.shape
    return pl.pallas_call(
        _cmp_prep_kernel,
        grid=(b,),
        in_specs=[pl.BlockSpec((1, n, 2 * KV_WIDTH), lambda bi: (bi, 0, 0)),
                  pl.BlockSpec((1, KV_WIDTH), lambda bi: (0, 0)),
                  pl.BlockSpec((MXU_DIM, MXU_DIM), lambda bi: (0, 0))],
        out_specs=[pl.BlockSpec((1, KV_HEADS, n, LANES), lambda bi: (bi, 0, 0, 0)),
                   pl.BlockSpec((1, KV_HEADS, HEAD_DIM, n), lambda bi: (bi, 0, 0, 0))],
        out_shape=[jax.ShapeDtypeStruct((b, KV_HEADS, n, LANES), BF16),
                   jax.ShapeDtypeStruct((b, KV_HEADS, HEAD_DIM, n), BF16)],
        compiler_params=_cparams(("parallel",)),
        name="cmp_prep",
    )(cc_perm, kc_gain, bd)


def _rank_select(score, n_keep):
    n, w = score.shape
    sub = lax.broadcasted_iota(jnp.int32, (8, w), 0)
    groups = [score[8 * k:8 * k + 8] for k in range(n // 8)]
    counts = [jnp.zeros((8, w), jnp.int32) for _ in groups]
    for i in range(n):
        si = jnp.broadcast_to(score[i:i + 1, :], (8, w))
        for k, blk in enumerate(groups):
            if 8 * k + 7 < i:
                inc = jnp.where(si > blk, 1, 0)
            elif 8 * k > i:
                inc = jnp.where(si >= blk, 1, 0)
            else:
                inc = jnp.where(sub > (i - 8 * k), jnp.where(si >= blk, 1, 0), jnp.where(si > blk, 1, 0))
            counts[k] = counts[k] + inc
    return jnp.concatenate(counts, axis=0) < n_keep


def _attn_prompt_kernel(qa_ref, kca_ref, vct_ref, kas_ref, vts_ref, kaw_ref, vtw_ref, gl_ref, o_ref, sig_ref,
                        *scratch, tq):
    i = pl.program_id(2)
    rows = Q_PER_KV * tq
    t0 = i * tq
    n_cmp = kca_ref.shape[2]
    n_selb = n_cmp // 2
    t_q = t0 + _lane_iota((1, rows)) % tq

    s_refs, w_refs = scratch[:GROUPS_PER_STEP], scratch[GROUPS_PER_STEP:]

    def prepare(gi):
        qa = qa_ref[0, gi].reshape(rows, LANES)
        w_refs[gi][...] = _dot_nt(k_rows(kaw_ref, gi, c_win, n_w), qa)
        s = _dot_nt(kca_ref[0, gi], qa)
        n_row = _row_iota((n_cmp, tq))
        cmp_id = 2 * (n_row % n_selb) + n_row // n_selb
        vis = (cmp_id + 1) * CMP_BLOCK - 1 <= t0 + _lane_iota((n_cmp, tq))
        vis = jnp.concatenate([vis] * Q_PER_KV, axis=1)
        s = jnp.where(vis, s, NEG)
        e = jnp.where(vis, jnp.exp2(s - jnp.max(s, axis=0, keepdims=True)), 0.0)
        p = e / jnp.maximum(jnp.sum(e, axis=0, keepdims=True), TINY)
        o_cmp = jnp.dot(vct_ref[0, gi], p.astype(BF16), preferred_element_type=F32)

        imp = jnp.zeros((n_selb, tq), F32)
        for r in range(Q_PER_KV):
            imp = imp + (p[:n_selb, r * tq:(r + 1) * tq] + p[n_selb:, r * tq:(r + 1) * tq])
        j_row = _row_iota((n_selb, tq))
        cur = (t0 + _lane_iota((n_selb, tq))) // SEL_BLOCK
        cand = j_row < cur
        forced = (j_row == 0) | (j_row == cur - 1)
        score = jnp.where(cand, jnp.where(forced, jnp.inf, imp), -jnp.inf)
        keep = (_rank_select(score, N_SEL - 1) & cand) | (j_row == cur)
        bias_t = jnp.where(keep, 0.0, SEL_BIAS)
        if n_selb < HEAD_DIM:
            bias_t = jnp.concatenate([bias_t, jnp.full((HEAD_DIM - n_selb, tq), SEL_BIAS, F32)], axis=0)
        bias = jnp.concatenate([jnp.zeros((HEAD_DIM, tq), F32), bias_t], axis=0).T
        bias4 = jnp.concatenate([bias] * Q_PER_KV, axis=0).astype(BF16)
        q_sel = jnp.where(_lane_iota((rows, LANES)) >= HEAD_DIM, bias4, qa)
        s = jnp.where(in_window, w_refs[gi][...], NEG)
        o_win = normalised(partial_softmax(s, vtw_ref, gi, c_win)[1])
        return q_sel, o_cmp, o_win

    def partial_softmax(s, vt_ref, gi, c0):
        m = jnp.max(s, axis=0, keepdims=True)
        pb = jnp.exp2(s - m).astype(BF16)
        acc = None
        for j in range(s.shape[0] // tq):
            part = jnp.dot(vt_ref[0, gi, c0 + j], pb[j * tq:(j + 1) * tq], preferred_element_type=F32)
            acc = part if acc is None else acc + part
        return m, acc

    def merge(a, b):
        m = jnp.maximum(a[0], b[0])
        return m, a[1] * jnp.exp2(a[0] - m) + b[1] * jnp.exp2(b[0] - m)

    def normalised(acc):
        return acc[:HEAD_DIM] * (1.0 / acc[HEAD_DIM:HEAD_DIM + 1])

    def k_rows(ref, gi, c0, n):
        return ref[0, gi, pl.ds(pl.multiple_of(c0 * tq, tq), n * tq), :]

    n_w = WINDOW // tq + 1
    c_win = jnp.maximum(i - (n_w - 1), 0)
    age = t_q - (c_win * tq + _row_iota((n_w * tq, 1)))
    in_window = (age >= 0) & (age < WINDOW)
    groups = [prepare(gi) for gi in range(GROUPS_PER_STEP)]

    n_full = i // SEL_CHUNKS
    k_pos_last = n_full * SEL_CHUNKS * tq + _row_iota((SEL_CHUNKS * tq, 1))

    def produce(gi, slab):
        s_refs[gi][...] = _dot_nt(k_rows(kas_ref, gi, SEL_CHUNKS * slab, SEL_CHUNKS), groups[gi][0])

    def consume(gi, slab, carry, last):
        s = s_refs[gi][...]
        if last:
            s = jnp.where(k_pos_last <= t_q, s, NEG)
        return merge(carry, partial_softmax(s, vts_ref, gi, SEL_CHUNKS * slab))

    def sel_body(it, carries):
        c0, c1 = carries
        produce(1, it)
        c0 = consume(0, it, c0, False)
        produce(0, it + 1)
        c1 = consume(1, it, c1, False)
        return c0, c1

    init = (jnp.full((1, rows), NEG, F32), jnp.zeros((VT_ROWS, rows), F32))
    produce(0, 0)
    c0, c1 = lax.fori_loop(0, n_full, sel_body, (init, init))
    produce(1, n_full)
    carries = (consume(0, n_full, c0, True), consume(1, n_full, c1, True))

    sig_ref[...] = jax.nn.sigmoid(gl_ref[...]).T
    for gi, (q_sel, o_cmp, o_win) in enumerate(groups):
        o_sel = normalised(carries[gi][1])
        group = pl.program_id(1) * GROUPS_PER_STEP + gi

        def gate(branch):
            r0 = branch * N_HEADS + group * Q_PER_KV
            return jnp.concatenate([sig_ref[pl.ds(r0 + r, 1), :] for r in range(Q_PER_KV)], axis=1)

        o_t = gate(0) * o_cmp + gate(1) * o_sel + gate(2) * o_win
        for h in range(0, Q_PER_KV, 2):
            pair_t = jnp.concatenate([o_t[:, h * tq:(h + 1) * tq], o_t[:, (h + 1) * tq:(h + 2) * tq]], axis=0)
            lane0 = (gi * Q_PER_KV + h) * HEAD_DIM
            o_ref[:, lane0:lane0 + LANES] = pair_t.T


def _attn_prompt(z, qa, kca, vct, kas, vts, kaw, vtw, b, t, tq):
    nt = t // tq
    n_cmp = kca.shape[2]
    gps = GROUPS_PER_STEP
    qspec = pl.BlockSpec((1, gps, Q_PER_KV, tq, LANES), lambda bi, g, i: (bi, g, 0, i, 0))
    kspec = pl.BlockSpec((1, gps, t, LANES), lambda bi, g, i: (bi, g, 0, 0))
    vspec = pl.BlockSpec((1, gps, nt, VT_ROWS, tq), lambda bi, g, i: (bi, g, 0, 0, 0))
    gspec = pl.BlockSpec((tq, LANES), lambda bi, g, i: (bi * nt + i, C_GNSA // LANES))
    return pl.pallas_call(
        functools.partial(_attn_prompt_kernel, tq=tq),
        grid=(b, KV_HEADS // gps, nt),
        in_specs=[qspec,
                  pl.BlockSpec((1, gps, n_cmp, LANES), lambda bi, g, i: (bi, g, 0, 0)),
                  pl.BlockSpec((1, gps, HEAD_DIM, n_cmp), lambda bi, g, i: (bi, g, 0, 0)),
                  kspec, vspec, kspec, vspec, gspec],
        out_specs=pl.BlockSpec((tq, gps * Q_PER_KV * HEAD_DIM), lambda bi, g, i: (bi * nt + i, g)),
        out_shape=jax.ShapeDtypeStruct((b * t, ATT_WIDTH), F32),
        scratch_shapes=[pltpu.VMEM((LANES, tq), F32)]
        + [pltpu.VMEM((SEL_CHUNKS * tq, Q_PER_KV * tq), F32) for _ in range(gps)]
        + [pltpu.VMEM((WINDOW + tq, Q_PER_KV * tq), F32) for _ in range(gps)],
        compiler_params=_cparams(("parallel", "parallel", "arbitrary")),
        name="attn_prompt",
    )(qa, kca, vct, kas, vts, kaw, vtw, z)


def _rglru_kernel(x_ref, hist_ref, h0_ref, cw_ref, cb_ref, wr_ref, br_ref, wi_ref, bi_ref, lam_ref,
                  h_ref, hl_ref, cs_ref, carry_ref, tail_ref, a_ref, u_ref, xe_ref, *, tt, nb):
    i = pl.program_id(1)

    @pl.when(i == 0)
    def _():
        carry_ref[...] = h0_ref[...]
        tail_ref[...] = hist_ref[...]

    cw = cw_ref[...]
    nl = -lam_ref[...]
    softplus = jnp.maximum(nl, 0.0) + jnp.log1p(jnp.exp(-jnp.abs(nl)))
    for k in range(nb):
        x = x_ref[k]
        xe_ref[k, 0:8, :] = tail_ref[k]
        xe_ref[k, 8:8 + tt, :] = x
        xc = cb_ref[...]
        for j in range(CONV_W):
            lo = 8 - (CONV_W - 1) + j
            xc = xc + (x if lo == 8 else xe_ref[k, lo:lo + tt, :]) * cw[j:j + 1]
        xb = xc.astype(BF16)

        def gates(w_ref, b_ref):
            parts = [jnp.dot(xb[:, c * MXU_DIM:(c + 1) * MXU_DIM], w_ref[c], preferred_element_type=F32)
                     for c in range(D_RNN // MXU_DIM)]
            z = jnp.concatenate(parts, axis=1) + b_ref[...]
            return 0.5 * jnp.tanh(0.5 * z) + 0.5

        r = gates(wr_ref, br_ref)
        ig = gates(wi_ref, bi_ref)
        log_a = -LRU_C * r * softplus
        a_ref[k] = jnp.exp(log_a)
        th = jnp.tanh(log_a)
        gap = -2.0 * th / (1.0 - th)
        root = jnp.where(gap > 0.0, gap * lax.rsqrt(gap), 0.0)
        u_ref[k] = root * (ig * xc)
        tail_ref[k] = x[tt - 8:tt]

    def step(t, hs):
        out = []
        for k in range(nb):
            h = a_ref[k, pl.ds(t, 1), :] * hs[k] + u_ref[k, pl.ds(t, 1), :]
            h_ref[k, pl.ds(t, 1), :] = h
            out.append(h)
        return tuple(out)

    hs = lax.fori_loop(0, tt, step, tuple(carry_ref[k] for k in range(nb)), unroll=8)
    for k in range(nb):
        carry_ref[k] = hs[k]

    @pl.when(i == pl.num_programs(1) - 1)
    def _():
        hl_ref[...] = carry_ref[...]
        cs_ref[...] = tail_ref[...]


def _rglru(z, b, t, hist8, h0, conv_w, conv_b, wr_bd, b_rg, wi_bd, b_ig, lam, tt, nb):
    nt = t // tt

    def const(shape):
        return pl.BlockSpec(shape, lambda gb, i: tuple(0 for _ in shape))

    def per_seq(rows):
        return pl.BlockSpec((nb, rows, D_RNN), lambda gb, i: (gb, 0, 0))

    row = lambda v: v.reshape(1, D_RNN)
    h, h_last, conv_tail = pl.pallas_call(
        functools.partial(_rglru_kernel, tt=tt, nb=nb),
        grid=(b // nb, nt),
        in_specs=[pl.BlockSpec((nb, tt, D_RNN), lambda gb, i: (gb, i, C_XRNN // D_RNN)),
                  per_seq(8), per_seq(1),
                  const((CONV_W, D_RNN)), const((1, D_RNN)),
                  const((D_RNN // MXU_DIM, MXU_DIM, MXU_DIM)), const((1, D_RNN)),
                  const((D_RNN // MXU_DIM, MXU_DIM, MXU_DIM)), const((1, D_RNN)),
                  const((1, D_RNN))],
        out_specs=[pl.BlockSpec((nb, tt, D_RNN), lambda gb, i: (gb, i, 0)), per_seq(1), per_seq(8)],
        out_shape=[jax.ShapeDtypeStruct((b, t, D_RNN), F32),
                   jax.ShapeDtypeStruct((b, 1, D_RNN), F32),
                   jax.ShapeDtypeStruct((b, 8, D_RNN), F32)],
        scratch_shapes=[pltpu.VMEM((nb, 1, D_RNN), F32), pltpu.VMEM((nb, 8, D_RNN), F32),
                        pltpu.VMEM((nb, tt, D_RNN), F32), pltpu.VMEM((nb, tt, D_RNN), F32),
                        pltpu.VMEM((nb, tt + 8, D_RNN), F32)],
        compiler_params=_cparams(("parallel", "arbitrary")),
        name="rglru",
    )(z.reshape(b, t, N_PACK), hist8, h0.reshape(b, 1, D_RNN), conv_w, row(conv_b),
      wr_bd, row(b_rg), wi_bd, row(b_ig), row(lam))
    return h.reshape(b * t, D_RNN), h_last, conv_tail


def _out_kernel(x_ref, oatt_ref, gatt_ref, h_ref, grnn_ref, gma_ref, gmr_ref, wa_ref, wr_ref, wo_ref, y_ref):
    a = (oatt_ref[...] * jax.nn.silu(gatt_ref[...])).astype(BF16)
    u_att = jnp.dot(a, wa_ref[...], preferred_element_type=F32)
    r = (h_ref[...] * jax.nn.silu(grnn_ref[...])).astype(BF16)
    u_rnn = jnp.dot(r, wr_ref[...], preferred_element_type=F32)
    m = jax.nn.sigmoid(gma_ref[...]) * u_att + jax.nn.sigmoid(gmr_ref[...]) * u_rnn
    y_ref[...] = x_ref[...] + jnp.dot(m.astype(BF16), wo_ref[...], preferred_element_type=F32)


def _out_proj(x2d, z, o_att, h_rnn, wa, wr, wo, tm):
    m = x2d.shape[0]

    def zspec(width, col):
        return pl.BlockSpec((tm, width), lambda i, c=col // width: (i, c))

    def wspec(shape):
        return pl.BlockSpec(shape, lambda i: (0, 0), pipeline_mode=pl.Buffered(1))

    rows = lambda width: pl.BlockSpec((tm, width), lambda i: (i, 0))
    return pl.pallas_call(
        _out_kernel,
        grid=(m // tm,),
        in_specs=[rows(D_MODEL), rows(ATT_WIDTH), zspec(ATT_WIDTH, C_GATT), rows(D_RNN), zspec(D_RNN, C_GRNN),
                  zspec(D_MODEL, C_GMATT), zspec(D_MODEL, C_GMRNN),
                  wspec((ATT_WIDTH, D_MODEL)), wspec((D_RNN, D_MODEL)), wspec((D_MODEL, D_MODEL))],
        out_specs=rows(D_MODEL),
        out_shape=jax.ShapeDtypeStruct((m, D_MODEL), F32),
        compiler_params=_cparams(("parallel",)),
        name="out_proj",
    )(x2d, o_att, z, h_rnn, z, z, z, wa, wr, wo)


def _prep_sample_kernel(zq_ref, zc_ref, zs_ref, zw_ref, qg_ref, ksg_ref, kwg_ref, bd_ref,
                        qn_ref, lc_ref, ls_ref, lw_ref):
    bd = bd_ref[...]
    qn_ref[...] = _seg_rms(zq_ref[...], bd) * qg_ref[...] * SCALE
    lc_ref[...] = zc_ref[...]
    for z_ref, g_ref, l_ref in ((zs_ref, ksg_ref, ls_ref), (zw_ref, kwg_ref, lw_ref)):
        z = z_ref[...]
        l_ref[:, :KV_WIDTH] = _seg_rms(z[:, :KV_WIDTH], bd) * g_ref[...]
        l_ref[:, KV_WIDTH:] = z[:, KV_WIDTH:]


def _prep_sample(z, q_gain, ks_gain, kw_gain, bd):
    m = z.shape[0]

    def zspec(width, col):
        return pl.BlockSpec((m, width), lambda i, c=col // width: (0, c))

    def const(shape):
        return pl.BlockSpec(shape, lambda i: tuple(0 for _ in shape))

    full = lambda width: pl.BlockSpec((m, width), lambda i: (0, 0))
    leaf = jax.ShapeDtypeStruct((m, 2 * KV_WIDTH), F32)
    return pl.pallas_call(
        _prep_sample_kernel,
        grid=(1,),
        in_specs=[zspec(ATT_WIDTH, C_Q), zspec(512, C_KV), zspec(512, C_KV + 512), zspec(512, C_KV + 1024),
                  const((1, ATT_WIDTH)), const((1, KV_WIDTH)), const((1, KV_WIDTH)), const((MXU_DIM, MXU_DIM))],
        out_specs=[full(ATT_WIDTH), full(512), full(512), full(512)],
        out_shape=[jax.ShapeDtypeStruct((m, ATT_WIDTH), F32), leaf, leaf, leaf],
        compiler_params=_cparams(("arbitrary",)),
        name="prep_sample",
    )(z, z, z, z, q_gain, ks_gain, kw_gain, bd)


def _page_specs(n):
    return [pl.BlockSpec((1, 2 * KV_WIDTH, PAGE_SIZE),
                         lambda bi, s, pt, k=k, n=n: (pt[bi, s * n + k], 0, 0)) for k in range(n)]


def _compress_pages_kernel(pt_ref, *refs):
    del pt_ref
    pages = refs[:CMP_PAGES]
    wt_ref, seg_ref, ok_ref, ov_ref = refs[CMP_PAGES:]
    wt = wt_ref[...]
    for tile in range(CMP_PAGES // CMP_TILE_PAGES):
        acc_k = jnp.zeros((KV_WIDTH, LANES), F32)
        acc_v = jnp.zeros((KV_WIDTH, LANES), F32)
        for pair in range(CMP_TILE_PAGES // 2):
            first = tile * CMP_TILE_PAGES + 2 * pair
            pa = pages[first][0] * wt
            pb = pages[first + 1][0] * wt
            seg = seg_ref[pair]
            lhs_k = jnp.concatenate([pa[:KV_WIDTH], pb[:KV_WIDTH]], axis=1).astype(BF16)
            lhs_v = jnp.concatenate([pa[KV_WIDTH:], pb[KV_WIDTH:]], axis=1).astype(BF16)
            acc_k = acc_k + jnp.dot(lhs_k, seg, preferred_element_type=F32)
            acc_v = acc_v + jnp.dot(lhs_v, seg, preferred_element_type=F32)
        ok_ref[0, :, tile * LANES:(tile + 1) * LANES] = acc_k
        ov_ref[0, :, tile * LANES:(tile + 1) * LANES] = acc_v


def _compress_pages(page_table, pool_t, wt, seg):
    b, n_pages = page_table.shape
    ns = n_pages // CMP_PAGES
    cols = CMP_PAGES // CMP_TILE_PAGES * LANES
    out_spec = pl.BlockSpec((1, KV_WIDTH, cols), lambda bi, s, pt: (bi, 0, s))
    grid_spec = pltpu.PrefetchScalarGridSpec(
        num_scalar_prefetch=1,
        grid=(b, ns),
        in_specs=_page_specs(CMP_PAGES) + [
            pl.BlockSpec((2 * KV_WIDTH, PAGE_SIZE), lambda bi, s, pt: (0, 0)),
            pl.BlockSpec((CMP_TILE_PAGES // 2, 2 * PAGE_SIZE, LANES), lambda bi, s, pt: (0, 0, 0))],
        out_specs=[out_spec, out_spec],
    )
    shape = jax.ShapeDtypeStruct((b, KV_WIDTH, ns * cols), F32)
    return pl.pallas_call(
        _compress_pages_kernel,
        grid_spec=grid_spec,
        out_shape=[shape, shape],
        compiler_params=_cparams(("parallel", "arbitrary")),
        name="compress_pages",
    )(page_table, *([pool_t] * CMP_PAGES), wt, seg)


def _softmax_lanes(parts, masks):
    parts = [jnp.where(mk, s, NEG) for s, mk in zip(parts, masks)]
    mx = functools.reduce(jnp.maximum, [jnp.max(s, axis=1, keepdims=True) for s in parts])
    es = [jnp.where(mk, jnp.exp(s - mx), 0.0) for s, mk in zip(parts, masks)]
    den = jnp.maximum(sum(jnp.sum(e, axis=1, keepdims=True) for e in es), TINY)
    return [e / den for e in es]


def _sample_cmp_win_kernel(qbd_ref, ck_ref, cv_ref, kg_ref, cwin_ref, wnew_ref,
                           ocmp_ref, owin_ref, bias_ref, wst_ref, *, past_len, t_new):
    qbd = qbd_ref[0]
    rows = qbd.shape[0]
    ck = ck_ref[0]
    n_cmp = ck.shape[1]
    parts = []
    for g in range(KV_HEADS):
        xs = ck[g * HEAD_DIM:(g + 1) * HEAD_DIM]
        parts.append(xs * lax.rsqrt(jnp.mean(xs * xs, axis=0, keepdims=True) + EPS))
    kc = (jnp.concatenate(parts, axis=0) * kg_ref[...]).astype(BF16)

    t_row = _row_iota((rows, n_cmp)) % t_new
    n_lane = _lane_iota((rows, n_cmp))
    cmp_id = LANES * (n_lane // LANES) + 2 * (n_lane % (LANES // 2)) + (n_lane % LANES) // (LANES // 2)
    vis = (cmp_id + 1) * CMP_BLOCK - 1 <= past_len + t_row
    (p,) = _softmax_lanes([jnp.dot(qbd, kc, preferred_element_type=F32)], [vis])
    ocmp_ref[0] = _dot_nt(p.astype(BF16), cv_ref[0].astype(BF16))

    gt = KV_HEADS * t_new
    psum = p[0:gt]
    for r in range(1, Q_PER_KV):
        psum = psum + p[r * gt:(r + 1) * gt]
    halves = []
    for c in range(n_cmp // LANES):
        tile = psum[:, c * LANES:(c + 1) * LANES]
        halves.append(tile + pltpu.roll(tile, LANES // 2, 1))
    if len(halves) == 1:
        imp = halves[0][:, :LANES // 2]
    else:
        low = _lane_iota((gt, LANES)) < LANES // 2
        imp = jnp.concatenate([jnp.where(low, halves[c], halves[c + 1]) for c in range(0, len(halves), 2)], axis=1)
    n_blk = n_cmp // 2
    j = _lane_iota((gt, n_blk))
    cur = (past_len + _row_iota((gt, n_blk)) % t_new) // SEL_BLOCK
    cand = j < cur
    forced = (j == 0) | (j == cur - 1)
    score = jnp.where(cand, jnp.where(forced, jnp.inf, imp), -jnp.inf)
    width = min(LANES, n_blk)
    tiles = [score[:, k * width:(k + 1) * width] for k in range(n_blk // width)]
    counts = [jnp.zeros((gt, width), jnp.int32) for _ in tiles]
    lane_w = _lane_iota((gt, width))
    for c in range(n_blk):
        col = score[:, c:c + 1]
        for k, tile in enumerate(tiles):
            if k < c // width:
                inc = jnp.where(col > tile, 1, 0)
            elif k > c // width:
                inc = jnp.where(col >= tile, 1, 0)
            else:
                inc = jnp.where(lane_w > c % width, jnp.where(col >= tile, 1, 0), jnp.where(col > tile, 1, 0))
            counts[k] = counts[k] + inc
    count = counts[0] if len(counts) == 1 else jnp.concatenate(counts, axis=1)
    keep = (count < N_SEL - 1) & cand
    bias_ref[0] = jnp.where(keep, 0.0, SEL_BIAS)

    cwin = cwin_ref[0]
    wb = cwin.shape[1]
    wnew = wnew_ref[0]
    wnew_p = jnp.concatenate([wnew, jnp.zeros((LANES - t_new, 2 * KV_WIDTH), F32)], axis=0)
    t_w = _row_iota((rows, wb)) % t_new
    idx = _lane_iota((rows, wb))
    ok_w = (idx <= wb + t_w) & (idx > wb + t_w - WINDOW)
    t_n = _row_iota((rows, LANES)) % t_new
    idx_n = wb + _lane_iota((rows, LANES))
    ok_n = (idx_n <= wb + t_n) & (idx_n > wb + t_n - WINDOW) & (_lane_iota((rows, LANES)) < t_new)
    p_w, p_n = _softmax_lanes([jnp.dot(qbd, cwin[:KV_WIDTH].astype(BF16), preferred_element_type=F32),
                               _dot_nt(qbd, wnew_p[:, :KV_WIDTH].astype(BF16))], [ok_w, ok_n])
    owin_ref[0] = (_dot_nt(p_w.astype(BF16), cwin[KV_WIDTH:].astype(BF16))
                   + jnp.dot(p_n.astype(BF16), wnew_p[:, KV_WIDTH:].astype(BF16), preferred_element_type=F32))

    keep_lanes = LANES - t_new
    new_t = jnp.concatenate([jnp.zeros((keep_lanes, 2 * KV_WIDTH), F32), wnew], axis=0).T
    rolled = [pltpu.roll(cwin[:, c * LANES:(c + 1) * LANES], keep_lanes, 1) for c in range(wb // LANES)]
    rolled.append(new_t)
    first = _lane_iota((2 * KV_WIDTH, LANES)) < keep_lanes
    for c in range(wb // LANES):
        wst_ref[0, :, c * LANES:(c + 1) * LANES] = jnp.where(first, rolled[c], rolled[c + 1])


def _sample_cmp_win(qbd, ck_t, cv_t, kc_gain_col, cwin_t, wnew, past_len):
    b, rows, _ = qbd.shape
    n_cmp = ck_t.shape[2]
    wb = cwin_t.shape[2]
    t_new = wnew.shape[1]
    gt = KV_HEADS * t_new

    def per_b(shape):
        return pl.BlockSpec((1,) + shape, lambda bi: (bi, 0, 0))

    return pl.pallas_call(
        functools.partial(_sample_cmp_win_kernel, past_len=past_len, t_new=t_new),
        grid=(b,),
        in_specs=[per_b((rows, MXU_DIM)), per_b((KV_WIDTH, n_cmp)), per_b((KV_WIDTH, n_cmp)),
                  pl.BlockSpec((KV_WIDTH, 1), lambda bi: (0, 0)),
                  per_b((2 * KV_WIDTH, wb)), per_b((t_new, 2 * KV_WIDTH))],
        out_specs=[per_b((rows, MXU_DIM)), per_b((rows, MXU_DIM)), per_b((gt, n_cmp // 2)), per_b((2 * KV_WIDTH, wb))],
        out_shape=[jax.ShapeDtypeStruct((b, rows, MXU_DIM), F32), jax.ShapeDtypeStruct((b, rows, MXU_DIM), F32),
                   jax.ShapeDtypeStruct((b, gt, n_cmp // 2), F32), jax.ShapeDtypeStruct((b, 2 * KV_WIDTH, wb), F32)],
        compiler_params=_cparams(("parallel",)),
        name="sample_cmp_win",
    )(qbd, ck_t, cv_t, kc_gain_col, cwin_t, wnew)


def _sample_sel_kernel(pt_ref, *refs, t_new):
    del pt_ref
    pages = refs[:SEL_PAGES]
    (qbd_ref, bsel_ref, onehot_ref, snew_ref, ocmp_ref, owin_ref, gl_ref,
     o_ref, m_ref, l_ref, acc_ref) = refs[SEL_PAGES:SEL_PAGES + 11]
    s_refs = refs[SEL_PAGES + 11:]
    s_idx = pl.program_id(1)
    qbd = qbd_ref[0]
    rows = qbd.shape[0]

    @pl.when(s_idx == 0)
    def _():
        m_ref[...] = jnp.full(m_ref.shape, NEG, F32)
        l_ref[...] = jnp.zeros(l_ref.shape, F32)
        acc_ref[...] = jnp.zeros(acc_ref.shape, F32)

    def partial_softmax(s, pv):
        m = jnp.max(s, axis=1, keepdims=True)
        pe = jnp.exp(s - m)
        return m, jnp.sum(pe, axis=1, keepdims=True), pv(pe.astype(BF16))

    def update(parts):
        m_old = m_ref[...]
        m_new = functools.reduce(jnp.maximum, [p[0] for p in parts], m_old)
        scale = jnp.exp(m_old - m_new)
        l = scale * l_ref[...]
        acc = scale * acc_ref[...]
        for m, l_part, acc_part in parts:
            c = jnp.exp(m - m_new)
            l = l + c * l_part
            acc = acc + c * acc_part
        m_ref[...] = m_new
        l_ref[...] = l
        acc_ref[...] = acc

    bsel = bsel_ref[0, 0]
    keys = SEL_SUB * PAGE_SIZE
    for g, s_ref in enumerate(s_refs):
        pgs = pages[g * SEL_SUB:(g + 1) * SEL_SUB]
        k_t = jnp.concatenate([pg[0, :KV_WIDTH, :].astype(BF16) for pg in pgs], axis=1)
        s_ref[...] = (jnp.dot(qbd, k_t, preferred_element_type=F32)
                      + jnp.dot(bsel, onehot_ref[:, g * keys:(g + 1) * keys], preferred_element_type=F32))
    parts = []
    for g, s_ref in enumerate(s_refs):
        pgs = pages[g * SEL_SUB:(g + 1) * SEL_SUB]
        v_t = jnp.concatenate([pg[0, KV_WIDTH:, :].astype(BF16) for pg in pgs], axis=1)
        parts.append(partial_softmax(s_ref[...], lambda pb, v_t=v_t: _dot_nt(pb, v_t)))
    update(parts)

    @pl.when(s_idx == pl.num_programs(1) - 1)
    def _():
        snew = jnp.concatenate([snew_ref[0], jnp.zeros((LANES - t_new, 2 * KV_WIDTH), F32)], axis=0)
        t_q = _row_iota((rows, LANES)) % t_new
        t_k = _lane_iota((rows, LANES))
        s = jnp.where((t_k <= t_q) & (t_k < t_new), _dot_nt(qbd, snew[:, :KV_WIDTH].astype(BF16)), NEG)
        v_new = snew[:, KV_WIDTH:].astype(BF16)
        update([partial_softmax(s, lambda pb: jnp.dot(pb, v_new, preferred_element_type=F32))])
        o_sel = acc_ref[...] / l_ref[...]
        sig = jax.nn.sigmoid(gl_ref[0])
        o_ref[0] = sig[:, 0:1] * ocmp_ref[0] + sig[:, 1:2] * o_sel + sig[:, 2:3] * owin_ref[0]


def _sample_sel(page_table, pool_t, qbd, bsel, onehot, snew, ocmp, owin, gl):
    b, n_pages = page_table.shape
    ns = n_pages // SEL_PAGES
    rows = qbd.shape[1]
    t_new = snew.shape[1]
    keys = SEL_PAGES * PAGE_SIZE

    def per_b(shape):
        return pl.BlockSpec((1,) + shape, lambda bi, s, pt: (bi,) + tuple(0 for _ in shape))

    grid_spec = pltpu.PrefetchScalarGridSpec(
        num_scalar_prefetch=1,
        grid=(b, ns),
        in_specs=_page_specs(SEL_PAGES) + [
            per_b((rows, MXU_DIM)),
            pl.BlockSpec((1, 1, rows, LANES), lambda bi, s, pt: (bi, s, 0, 0)),
            pl.BlockSpec((LANES, keys), lambda bi, s, pt: (0, 0)),
            per_b((t_new, 2 * KV_WIDTH)), per_b((rows, MXU_DIM)), per_b((rows, MXU_DIM)), per_b((rows, LANES))],
        out_specs=per_b((rows, MXU_DIM)),
        scratch_shapes=[pltpu.VMEM((rows, 1), F32), pltpu.VMEM((rows, 1), F32), pltpu.VMEM((rows, MXU_DIM), F32)]
        + [pltpu.VMEM((rows, SEL_SUB * PAGE_SIZE), F32) for _ in range(SEL_PAGES // SEL_SUB)],
    )
    return pl.pallas_call(
        functools.partial(_sample_sel_kernel, t_new=t_new),
        grid_spec=grid_spec,
        out_shape=jax.ShapeDtypeStruct((b, rows, MXU_DIM), F32),
        compiler_params=_cparams(("parallel", "arbitrary")),
        name="sample_sel",
    )(page_table, *([pool_t] * SEL_PAGES), qbd, bsel, onehot, snew, ocmp, owin, gl)


def _pack_weights(w_in, q_norm_g, k_norm_g, w_cmp, w_rg, w_ig, w_att_out, w_rnn_out, w_out):
    def block_diag(w):
        per = MXU_DIM // HEAD_DIM
        w4 = w.reshape(RNN_BLOCKS // per, per, HEAD_DIM, HEAD_DIM)
        eye = jnp.eye(per, dtype=w.dtype)
        return jnp.einsum('cpde,pq->cpdqe', w4, eye).reshape(RNN_BLOCKS // per, MXU_DIM, MXU_DIM).astype(BF16)

    wt = jnp.broadcast_to(w_cmp.transpose(1, 2, 0)[:, None], (2, KV_HEADS, HEAD_DIM, CMP_BLOCK))
    wt = jnp.tile(wt, (1, 1, 1, PAGE_SIZE // CMP_BLOCK)).reshape(2 * KV_WIDTH, PAGE_SIZE)
    lane = jnp.arange(2 * PAGE_SIZE)
    pair = jnp.arange(CMP_TILE_PAGES // 2)
    n_local = ((PAGE_SIZE // CMP_BLOCK) * (2 * pair[:, None] + lane[None, :] // PAGE_SIZE)
               + (lane[None, :] % PAGE_SIZE) // CMP_BLOCK)
    col = (n_local % 2) * (LANES // 2) + n_local // 2
    seg = (jnp.arange(LANES)[None, None, :] == col[:, :, None]).astype(BF16)

    return dict(
        w_t=w_in.T,
        q_gain=jnp.tile(q_norm_g, N_HEADS).reshape(1, ATT_WIDTH),
        kc_gain=jnp.tile(k_norm_g[0], KV_HEADS).reshape(1, KV_WIDTH),
        ks_gain=jnp.tile(k_norm_g[1], KV_HEADS).reshape(1, KV_WIDTH),
        kw_gain=jnp.tile(k_norm_g[2], KV_HEADS).reshape(1, KV_WIDTH),
        wc=jnp.broadcast_to(w_cmp[:, :, None, :], (CMP_BLOCK, 2, KV_HEADS, HEAD_DIM)).reshape(CMP_BLOCK, 2 * KV_WIDTH),
        wt=wt, seg=seg,
        wr_bd=block_diag(w_rg), wi_bd=block_diag(w_ig),
        wa=w_att_out.astype(BF16), wr=w_rnn_out.astype(BF16), wo=w_out.astype(BF16),
        bd=_block_diag_ones(),
    )


def _prompt_layer(x, pw, norm_g, conv_w, conv_b, b_rg, b_ig, lam):
    b, t, _ = x.shape
    tq = ATTN_TILE
    x2d = x.reshape(b * t, D_MODEL)
    z = _proj(x2d, norm_g, pw['w_t'], tm=min(1024, b * t))
    qa, leaf_c, leaf_s, leaf_w, cc, kas, vts, kaw, vtw = _prep_prompt(
        z, b, t, pw['q_gain'], pw['ks_gain'], pw['kw_gain'], pw['wc'], pw['bd'], te=256, tq=tq)
    n_cmp = t // CMP_BLOCK
    cc_perm = cc.reshape(b, n_cmp // 2, 2, 2 * KV_WIDTH).transpose(0, 2, 1, 3).reshape(b, n_cmp, 2 * KV_WIDTH)
    kca, vct = _cmp_prep(cc_perm, pw['kc_gain'], pw['bd'])
    o_att = _attn_prompt(z, qa, kca, vct, kas, vts, kaw, vtw, b, t, tq)
    h_rnn, h_last, conv_tail = _rglru(z, b, t, jnp.zeros((b, 8, D_RNN), F32), jnp.zeros((b, D_RNN), F32),
                                      conv_w, conv_b, pw['wr_bd'], b_rg, pw['wi_bd'], b_ig, lam, tt=256, nb=b)
    y = _out_proj(x2d, z, o_att, h_rnn, pw['wa'], pw['wr'], pw['wo'], tm=256)
    def kv(leaf_t):
        tokens = leaf_t.shape[2]
        return leaf_t.reshape(b, 2, KV_HEADS, HEAD_DIM, tokens).transpose(0, 4, 1, 2, 3)

    w_keep = min(WINDOW, t)
    return y.reshape(b, t, D_MODEL), (kv(leaf_c), kv(leaf_s), kv(leaf_w[:, :, t - w_keep:]),
                                      h_last.reshape(b, D_RNN), conv_tail[:, 8 - (CONV_W - 1):])


def _feature_major(cache):
    n, tokens = cache.shape[:2]
    return cache.transpose(0, 2, 3, 4, 1).reshape(n, 2 * KV_WIDTH, tokens)


def _sample_layer(x, cache_cmp, cache_sel, cache_win, state_h, state_conv, page_table, pw,
                  norm_g, conv_w, conv_b, b_rg, b_ig, lam):
    b, t, _ = x.shape
    n_pages = page_table.shape[1]
    past_len = n_pages * PAGE_SIZE
    rows = N_HEADS * t
    x2d = x.reshape(b * t, D_MODEL)
    z = _proj(x2d, norm_g, pw['w_t'], tm=b * t)
    qn, leaf_c, leaf_s, leaf_w = _prep_sample(z, pw['q_gain'], pw['ks_gain'], pw['kw_gain'], pw['bd'])

    q5 = qn.reshape(b, t, KV_HEADS, Q_PER_KV, HEAD_DIM).transpose(0, 3, 2, 1, 4)
    qbd = (q5[:, :, :, :, None, :] * jnp.eye(KV_HEADS, dtype=F32)[None, None, :, None, :, None])
    qbd = qbd.reshape(b, rows, KV_WIDTH).astype(BF16)

    ck_t, cv_t = _compress_pages(page_table, _feature_major(cache_cmp), pw['wt'], pw['seg'])
    wb = cache_win.shape[1]
    o_cmp, o_win, bias, win_state_t = _sample_cmp_win(
        qbd, ck_t, cv_t, pw['kc_gain'].reshape(KV_WIDTH, 1), _feature_major(cache_win),
        leaf_w.reshape(b, t, 2 * KV_WIDTH), past_len)
    win_state = win_state_t.reshape(b, 2, KV_HEADS, HEAD_DIM, wb).transpose(0, 4, 1, 2, 3)

    ns = n_pages // SEL_PAGES
    blocks_per_step = SEL_PAGES * PAGE_SIZE // SEL_BLOCK
    bsel = bias.reshape(b, KV_HEADS * t, ns, blocks_per_step).transpose(0, 2, 1, 3)
    bsel = jnp.tile(bsel, (1, 1, Q_PER_KV, 1))
    bsel = jnp.pad(bsel, ((0, 0), (0, 0), (0, 0), (0, LANES - blocks_per_step))).astype(BF16)
    key_block = jnp.arange(SEL_PAGES * PAGE_SIZE) // SEL_BLOCK
    onehot = (jnp.arange(LANES)[:, None] == key_block[None, :]).astype(BF16)
    gl = z[:, C_GNSA:C_GNSA + 3 * N_HEADS].reshape(b, t, 3, KV_HEADS, Q_PER_KV)
    gl = jnp.pad(gl.transpose(0, 4, 3, 1, 2).reshape(b, rows, 3), ((0, 0), (0, 0), (0, LANES - 3)))
    o_full = _sample_sel(page_table, _feature_major(cache_sel), qbd, bsel, onehot,
                         leaf_s.reshape(b, t, 2 * KV_WIDTH), o_cmp, o_win, gl)
    o6 = o_full.reshape(b, Q_PER_KV, KV_HEADS, t, KV_HEADS, HEAD_DIM)
    o_att = jnp.stack([o6[:, :, g, :, g, :] for g in range(KV_HEADS)], axis=2)
    o_att = o_att.transpose(0, 3, 2, 1, 4).reshape(b * t, ATT_WIDTH)

    hist8 = jnp.pad(state_conv, ((0, 0), (8 - (CONV_W - 1), 0), (0, 0)))
    h_rnn, h_last, conv_tail = _rglru(z, b, t, hist8, state_h, conv_w, conv_b,
                                      pw['wr_bd'], b_rg, pw['wi_bd'], b_ig, lam, tt=t,
                                      nb=8 if b % 8 == 0 else b)
    y = _out_proj(x2d, z, o_att, h_rnn, pw['wa'], pw['wr'], pw['wo'], tm=min(256, b * t))
    kv = lambda leaf: leaf.reshape(b, -1, 2, KV_HEADS, HEAD_DIM)
    return y.reshape(b, t, D_MODEL), (kv(leaf_c), kv(leaf_s), win_state,
                                      h_last.reshape(b, D_RNN), conv_tail[:, 8 - (CONV_W - 1):])


def kernel(x_prompt, x_sample, cache_cmp, cache_sel, cache_win, state_h, state_conv, page_table,
           norm_g, w_in, q_norm_g, k_norm_g, w_cmp, conv_w, conv_b, w_rg, b_rg, w_ig, b_ig,
           lru_lambda, w_att_out, w_rnn_out, w_out):
    yp, ys = x_prompt, x_sample
    outs_p, outs_s = [], []
    for l in range(w_in.shape[0]):
        pw = _pack_weights(w_in[l], q_norm_g[l], k_norm_g[l], w_cmp[l], w_rg[l], w_ig[l],
                           w_att_out[l], w_rnn_out[l], w_out[l])
        yp, st_p = _prompt_layer(yp, pw, norm_g[l], conv_w[l], conv_b[l], b_rg[l], b_ig[l], lru_lambda[l])
        ys, st_s = _sample_layer(ys, cache_cmp[l], cache_sel[l], cache_win[l], state_h[l], state_conv[l],
                                 page_table, pw, norm_g[l], conv_w[l], conv_b[l], b_rg[l], b_ig[l], lru_lambda[l])
        outs_p.append(st_p)
        outs_s.append(st_s)
    cmp_p, sel_p, win_p, h_p, conv_p = [jnp.stack(a) for a in zip(*outs_p)]
    cmp_s, sel_s, win_s, h_s, conv_s = [jnp.stack(a) for a in zip(*outs_s)]
    return (yp, ys, cmp_p, sel_p, win_p, h_p, conv_p, cmp_s, sel_s, win_s, h_s, conv_s)
```

```python
import functools

import jax
import jax.numpy as jnp
from jax import lax
from jax.experimental import pallas as pl
from jax.experimental.pallas import tpu as pltpu

F32 = jnp.float32
BF16 = jnp.bfloat16

D_MODEL = 2048
N_HEADS = 16
HEAD_DIM = 64
KV_HEADS = 4
Q_PER_KV = N_HEADS // KV_HEADS
ATT_WIDTH = N_HEADS * HEAD_DIM
KV_WIDTH = KV_HEADS * HEAD_DIM
CMP_BLOCK = 32
SEL_BLOCK = 64
N_SEL = 16
WINDOW = 512
SCALE = HEAD_DIM ** -0.5
LOG2E = 1.4426950408889634
D_RNN = 1024
RNN_BLOCKS = 16
CONV_W = 4
LRU_C = 8.0
PAGE_SIZE = 128
EPS = 1e-6
NEG = -1e30
TINY = 1e-30
SEL_BIAS = -1e30

LANES = 128
MXU_DIM = 256
VMEM_LIMIT = 56 * 1024 * 1024

C_Q = 0
C_GATT = 1024
C_XRNN = 2048
C_GRNN = 3072
C_GMATT = 4096
C_GMRNN = 6144
C_KV = 8192
C_GNSA = 9728
N_PACK = 10240
PROJ_TILE = 1024
_W_Q, _W_KV, _W_GNSA, _W_GATT, _W_XRNN, _W_GRNN, _W_GMATT, _W_GMRNN = 0, 1024, 2560, 2608, 3632, 4656, 5680, 7728
PROJ_ROW_STARTS = (_W_Q, _W_GATT, _W_XRNN, _W_GRNN, _W_GMATT, _W_GMATT + 1024, _W_GMRNN, _W_GMRNN + 1024,
                   _W_KV, _W_KV + 1024)

SEL_PAGES = 64
SEL_SUB = 16
CMP_PAGES = 64
CMP_TILE_PAGES = 32
GROUPS_PER_STEP = 2
ATTN_TILE = 256
SEL_CHUNKS = 2
VT_ROWS = 80


def _cparams(sem):
    return pltpu.CompilerParams(dimension_semantics=sem, vmem_limit_bytes=VMEM_LIMIT)


def _proj_kernel(starts_ref, x_ref, g_ref, w_ref, o_ref, xn_ref):
    del starts_ref
    @pl.when(pl.program_id(1) == 0)
    def _():
        x = x_ref[...]
        ms = jnp.mean(x * x, axis=-1, keepdims=True)
        xn_ref[...] = (x * lax.rsqrt(ms + EPS) * g_ref[...]).astype(BF16)

    o_ref[...] = lax.dot_general(xn_ref[...], w_ref[...].astype(BF16), (((1,), (1,)), ((), ())),
                                 preferred_element_type=F32)


def _proj(x2d, norm_g, w_t, tm):
    m = x2d.shape[0]
    tn = PROJ_TILE
    grid_spec = pltpu.PrefetchScalarGridSpec(
        num_scalar_prefetch=1,
        grid=(m // tm, N_PACK // tn),
        in_specs=[
            pl.BlockSpec((tm, D_MODEL), lambda i, j, st: (i, 0)),
            pl.BlockSpec((1, D_MODEL), lambda i, j, st: (0, 0)),
            pl.BlockSpec((pl.Element(tn), pl.Element(D_MODEL)), lambda i, j, st: (pl.multiple_of(st[j], 16), 0)),
        ],
        out_specs=pl.BlockSpec((tm, tn), lambda i, j, st: (i, j)),
        scratch_shapes=[pltpu.VMEM((tm, D_MODEL), BF16)],
    )
    return pl.pallas_call(
        _proj_kernel,
        grid_spec=grid_spec,
        out_shape=jax.ShapeDtypeStruct((m, N_PACK), F32),
        compiler_params=_cparams(("parallel", "arbitrary")),
        name="proj",
    )(jnp.asarray(PROJ_ROW_STARTS, jnp.int32), x2d, norm_g.reshape(1, D_MODEL), w_t)


def _lane_iota(shape):
    return lax.broadcasted_iota(jnp.int32, shape, len(shape) - 1)


def _row_iota(shape):
    return lax.broadcasted_iota(jnp.int32, shape, 0)


def _seg_rms(x, bd):
    outs = []
    for c in range(x.shape[1] // MXU_DIM):
        xc = x[:, c * MXU_DIM:(c + 1) * MXU_DIM]
        x2 = xc * xc
        hi = x2.astype(BF16)
        lo = (x2 - hi.astype(F32)).astype(BF16)
        ss = (jnp.dot(hi, bd, preferred_element_type=F32)
              + jnp.dot(lo, bd, preferred_element_type=F32))
        outs.append(xc * lax.rsqrt(ss * (1.0 / HEAD_DIM) + EPS))
    return outs[0] if len(outs) == 1 else jnp.concatenate(outs, axis=1)


def _head_lo(x, h):
    tile = x[:, (h // 2) * LANES:(h // 2 + 1) * LANES]
    if h % 2:
        tile = pltpu.roll(tile, HEAD_DIM, 1)
    return jnp.where(_lane_iota(tile.shape) < HEAD_DIM, tile, 0.0)


def _heads_t(x):
    outs = []
    for c in range(x.shape[1] // LANES):
        xt = x[:, c * LANES:(c + 1) * LANES].T
        outs += [xt[:HEAD_DIM], xt[HEAD_DIM:]]
    return outs


def _dot_nt(a, b):
    return lax.dot_general(a, b, (((1,), (1,)), ((), ())), preferred_element_type=F32)


def _block_diag_ones():
    r = jnp.arange(MXU_DIM) // HEAD_DIM
    return (r[:, None] == r[None, :]).astype(BF16)


def _prep_prompt_kernel(zq_ref, zc_ref, zs_ref, zw_ref, qg_ref, ksg_ref, kwg_ref, wc_ref, bd_ref,
                        qa_ref, lc_ref, ls_ref, lw_ref, cc_ref, kas_ref, vts_ref, kaw_ref, vtw_ref, *, te, tq):
    i = pl.program_id(1)
    bd = bd_ref[...]
    qn = _seg_rms(zq_ref[...], bd) * qg_ref[...] * (SCALE * LOG2E)
    for h in range(N_HEADS):
        qa_ref[0, h // Q_PER_KV, h % Q_PER_KV] = _head_lo(qn, h).astype(BF16)

    zc = zc_ref[...]
    for c in range(2 * KV_WIDTH // LANES):
        lc_ref[0, c * LANES:(c + 1) * LANES, :] = zc[:, c * LANES:(c + 1) * LANES].T
    cc_ref[0] = jnp.sum(zc.reshape(te // CMP_BLOCK, CMP_BLOCK, 2 * KV_WIDTH) * wc_ref[...][None], axis=1)

    lane = _lane_iota((te, LANES))
    own_block = (i * te + _row_iota((te, LANES))) // SEL_BLOCK
    onehot = jnp.where(lane - HEAD_DIM == own_block, 1.0, 0.0)
    ones_rows = jnp.where(_row_iota((VT_ROWS - HEAD_DIM, te)) == 0, 1.0, 0.0).astype(BF16)

    for z_ref, g_ref, l_ref, ka_ref, vt_ref, with_onehot in (
            (zs_ref, ksg_ref, ls_ref, kas_ref, vts_ref, True),
            (zw_ref, kwg_ref, lw_ref, kaw_ref, vtw_ref, False)):
        z = z_ref[...]
        kn = _seg_rms(z[:, :KV_WIDTH], bd) * g_ref[...]
        v = z[:, KV_WIDTH:]
        for c in range(KV_WIDTH // LANES):
            l_ref[0, c * LANES:(c + 1) * LANES, :] = kn[:, c * LANES:(c + 1) * LANES].T
        for g, vt in enumerate(_heads_t(v)):
            l_ref[0, KV_WIDTH + g * HEAD_DIM:KV_WIDTH + (g + 1) * HEAD_DIM, :] = vt
            ka = _head_lo(kn, g)
            if with_onehot:
                ka = jnp.where(lane < HEAD_DIM, ka, onehot)
            ka_ref[0, g] = ka.astype(BF16)
            vt_aug = jnp.concatenate([vt.astype(BF16), ones_rows], axis=0)
            for c in range(te // tq):
                vt_ref[0, g, c] = vt_aug[:, c * tq:(c + 1) * tq]


def _prep_prompt(z, b, t, q_gain, ks_gain, kw_gain, wc, bd, te, tq):
    nt = t // te

    def zspec(width, col):
        return pl.BlockSpec((te, width), lambda bi, i, c=col // width: (bi * nt + i, c))

    def const(shape):
        return pl.BlockSpec(shape, lambda bi, i: tuple(0 for _ in shape))

    leaf_spec = pl.BlockSpec((1, 2 * KV_WIDTH, te), lambda bi, i: (bi, 0, i))
    leaf_shape = jax.ShapeDtypeStruct((b, 2 * KV_WIDTH, t), F32)
    k_spec = pl.BlockSpec((1, KV_HEADS, te, LANES), lambda bi, i: (bi, 0, i, 0))
    k_shape = jax.ShapeDtypeStruct((b, KV_HEADS, t, LANES), BF16)
    vt_spec = pl.BlockSpec((1, KV_HEADS, te // tq, VT_ROWS, tq), lambda bi, i: (bi, 0, i, 0, 0))
    vt_shape = jax.ShapeDtypeStruct((b, KV_HEADS, t // tq, VT_ROWS, tq), BF16)
    return pl.pallas_call(
        functools.partial(_prep_prompt_kernel, te=te, tq=tq),
        grid=(b, nt),
        in_specs=[zspec(ATT_WIDTH, C_Q), zspec(512, C_KV), zspec(512, C_KV + 512), zspec(512, C_KV + 1024),
                  const((1, ATT_WIDTH)), const((1, KV_WIDTH)), const((1, KV_WIDTH)),
                  const((CMP_BLOCK, 2 * KV_WIDTH)), const((MXU_DIM, MXU_DIM))],
        out_specs=[pl.BlockSpec((1, KV_HEADS, Q_PER_KV, te, LANES), lambda bi, i: (bi, 0, 0, i, 0)),
                   leaf_spec, leaf_spec, leaf_spec,
                   pl.BlockSpec((1, te // CMP_BLOCK, 2 * KV_WIDTH), lambda bi, i: (bi, i, 0)),
                   k_spec, vt_spec, k_spec, vt_spec],
        out_shape=[jax.ShapeDtypeStruct((b, KV_HEADS, Q_PER_KV, t, LANES), BF16),
                   leaf_shape, leaf_shape, leaf_shape,
                   jax.ShapeDtypeStruct((b, t // CMP_BLOCK, 2 * KV_WIDTH), F32),
                   k_shape, vt_shape, k_shape, vt_shape],
        compiler_params=_cparams(("parallel", "parallel")),
        name="prep_prompt",
    )(z, z, z, z, q_gain, ks_gain, kw_gain, wc, bd)


def _cmp_prep_kernel(cc_ref, kg_ref, bd_ref, kca_ref, vct_ref):
    cc = cc_ref[0]
    kc = _seg_rms(cc[:, :KV_WIDTH], bd_ref[...]) * kg_ref[...]
    for g, vt in enumerate(_heads_t(cc[:, KV_WIDTH:])):
        kca_ref[0, g] = _head_lo(kc, g).astype(BF16)
        vct_ref[0, g] = vt.astype(BF16)


def _cmp_prep(cc_perm, kc_gain, bd):
    b, n, _ = cc_perm.shape
    return pl.pallas_call(
        _cmp_prep_kernel,
        grid=(b,),
        in_specs=[pl.BlockSpec((1, n, 2 * KV_WIDTH), lambda bi: (bi, 0, 0)),
                  pl.BlockSpec((1, KV_WIDTH), lambda bi: (0, 0)),
                  pl.BlockSpec((MXU_DIM, MXU_DIM), lambda bi: (0, 0))],
        out_specs=[pl.BlockSpec((1, KV_HEADS, n, LANES), lambda bi: (bi, 0, 0, 0)),
                   pl.BlockSpec((1, KV_HEADS, HEAD_DIM, n), lambda bi: (bi, 0, 0, 0))],
        out_shape=[jax.ShapeDtypeStruct((b, KV_HEADS, n, LANES), BF16),
                   jax.ShapeDtypeStruct((b, KV_HEADS, HEAD_DIM, n), BF16)],
        compiler_params=_cparams(("parallel",)),
        name="cmp_prep",
    )(cc_perm, kc_gain, bd)


def _rank_select(score, n_keep):
    n, w = score.shape
    sub = lax.broadcasted_iota(jnp.int32, (8, w), 0)
    groups = [score[8 * k:8 * k + 8] for k in range(n // 8)]
    counts = [jnp.zeros((8, w), jnp.int32) for _ in groups]
    for i in range(n):
        si = jnp.broadcast_to(score[i:i + 1, :], (8, w))
        for k, blk in enumerate(groups):
            if 8 * k + 7 < i:
                inc = jnp.where(si > blk, 1, 0)
            elif 8 * k > i:
                inc = jnp.where(si >= blk, 1, 0)
            else:
                inc = jnp.where(sub > (i - 8 * k), jnp.where(si >= blk, 1, 0), jnp.where(si > blk, 1, 0))
            counts[k] = counts[k] + inc
    return jnp.concatenate(counts, axis=0) < n_keep


def _attn_prompt_kernel(qa_ref, kca_ref, vct_ref, kas_ref, vts_ref, kaw_ref, vtw_ref, gl_ref, o_ref, sig_ref,
                        *scratch, tq):
    i = pl.program_id(2)
    rows = Q_PER_KV * tq
    t0 = i * tq
    n_cmp = kca_ref.shape[2]
    n_selb = n_cmp // 2
    t_q = t0 + _lane_iota((1, rows)) % tq

    s_refs, w_refs = scratch[:GROUPS_PER_STEP], scratch[GROUPS_PER_STEP:]

    n_row = _row_iota((n_cmp, tq))
    cmp_id = 2 * (n_row % n_selb) + n_row // n_selb
    vis = (cmp_id + 1) * CMP_BLOCK - 1 <= t0 + _lane_iota((n_cmp, tq))
    vis = jnp.concatenate([vis] * Q_PER_KV, axis=1)
    j_row = _row_iota((n_selb, tq))
    cur = (t0 + _lane_iota((n_selb, tq))) // SEL_BLOCK
    cand = j_row < cur
    forced = (j_row == 0) | (j_row == cur - 1)

    def prepare(gi):
        qa = qa_ref[0, gi].reshape(rows, LANES)
        w_refs[gi][...] = _dot_nt(k_rows(kaw_ref, gi, c_win, n_w), qa)
        s = _dot_nt(kca_ref[0, gi], qa)
        s = jnp.where(vis, s, NEG)
        e = jnp.where(vis, jnp.exp2(s - jnp.max(s, axis=0, keepdims=True)), 0.0)
        p = e / jnp.maximum(jnp.sum(e, axis=0, keepdims=True), TINY)
        o_cmp = jnp.dot(vct_ref[0, gi], p.astype(BF16), preferred_element_type=F32)

        imp = jnp.zeros((n_selb, tq), F32)
        for r in range(Q_PER_KV):
            imp = imp + (p[:n_selb, r * tq:(r + 1) * tq] + p[n_selb:, r * tq:(r + 1) * tq])
        score = jnp.where(cand, jnp.where(forced, jnp.inf, imp), -jnp.inf)
        keep = (_rank_select(score, N_SEL - 1) & cand) | (j_row == cur)
        bias_t = jnp.where(keep, 0.0, SEL_BIAS)
        if n_selb < HEAD_DIM:
            bias_t = jnp.concatenate([bias_t, jnp.full((HEAD_DIM - n_selb, tq), SEL_BIAS, F32)], axis=0)
        bias = jnp.concatenate([jnp.zeros((HEAD_DIM, tq), F32), bias_t], axis=0).T
        bias4 = jnp.concatenate([bias] * Q_PER_KV, axis=0).astype(BF16)
        q_sel = jnp.where(_lane_iota((rows, LANES)) >= HEAD_DIM, bias4, qa)
        s = jnp.where(in_window, w_refs[gi][...], NEG)
        o_win = normalised(partial_softmax(s, vtw_ref, gi, c_win)[1])
        return q_sel, o_cmp, o_win

    def partial_softmax(s, vt_ref, gi, c0):
        m = jnp.max(s, axis=0, keepdims=True)
        pb = jnp.exp2(s - m).astype(BF16)
        acc = None
        for j in range(s.shape[0] // tq):
            part = jnp.dot(vt_ref[0, gi, c0 + j], pb[j * tq:(j + 1) * tq], preferred_element_type=F32)
            acc = part if acc is None else acc + part
        return m, acc

    def merge(a, b):
        m = jnp.maximum(a[0], b[0])
        return m, a[1] * jnp.exp2(a[0] - m) + b[1] * jnp.exp2(b[0] - m)

    def normalised(acc):
        return acc[:HEAD_DIM] * (1.0 / acc[HEAD_DIM:HEAD_DIM + 1])

    def k_rows(ref, gi, c0, n):
        return ref[0, gi, pl.ds(pl.multiple_of(c0 * tq, tq), n * tq), :]

    n_w = WINDOW // tq + 1
    c_win = jnp.maximum(i - (n_w - 1), 0)
    age = t_q - (c_win * tq + _row_iota((n_w * tq, 1)))
    in_window = lax.bitcast_convert_type(age, jnp.uint32) < jnp.uint32(WINDOW)
    groups = [prepare(gi) for gi in range(GROUPS_PER_STEP)]

    n_full = i // SEL_CHUNKS
    causal_last = n_full * SEL_CHUNKS * tq + _row_iota((SEL_CHUNKS * tq, 1)) <= t_q

    def produce(gi, slab):
        s_refs[gi][...] = _dot_nt(k_rows(kas_ref, gi, SEL_CHUNKS * slab, SEL_CHUNKS), groups[gi][0])

    def consume(gi, slab, carry, last):
        s = s_refs[gi][...]
        if last:
            s = jnp.where(causal_last, s, NEG)
        return merge(carry, partial_softmax(s, vts_ref, gi, SEL_CHUNKS * slab))

    def sel_body(it, carries):
        c0, c1 = carries
        produce(1, it)
        c0 = consume(0, it, c0, False)
        produce(0, it + 1)
        c1 = consume(1, it, c1, False)
        return c0, c1

    init = (jnp.full((1, rows), NEG, F32), jnp.zeros((VT_ROWS, rows), F32))
    produce(0, 0)
    c0, c1 = lax.fori_loop(0, n_full, sel_body, (init, init))
    produce(1, n_full)
    carries = (consume(0, n_full, c0, True), consume(1, n_full, c1, True))

    sig_ref[...] = jax.nn.sigmoid(gl_ref[...]).T
    for gi, (q_sel, o_cmp, o_win) in enumerate(groups):
        o_sel = normalised(carries[gi][1])
        group = pl.program_id(1) * GROUPS_PER_STEP + gi

        def gate(branch):
            r0 = branch * N_HEADS + group * Q_PER_KV
            return jnp.concatenate([sig_ref[pl.ds(r0 + r, 1), :] for r in range(Q_PER_KV)], axis=1)

        o_t = gate(0) * o_cmp + gate(1) * o_sel + gate(2) * o_win
        for h in range(0, Q_PER_KV, 2):
            pair_t = jnp.concatenate([o_t[:, h * tq:(h + 1) * tq], o_t[:, (h + 1) * tq:(h + 2) * tq]], axis=0)
            lane0 = (gi * Q_PER_KV + h) * HEAD_DIM
            o_ref[:, lane0:lane0 + LANES] = pair_t.T


def _attn_prompt(z, qa, kca, vct, kas, vts, kaw, vtw, b, t, tq):
    nt = t // tq
    n_cmp = kca.shape[2]
    gps = GROUPS_PER_STEP
    qspec = pl.BlockSpec((1, gps, Q_PER_KV, tq, LANES), lambda bi, g, i: (bi, g, 0, i, 0))
    kspec = pl.BlockSpec((1, gps, t, LANES), lambda bi, g, i: (bi, g, 0, 0))
    vspec = pl.BlockSpec((1, gps, nt, VT_ROWS, tq), lambda bi, g, i: (bi, g, 0, 0, 0))
    gspec = pl.BlockSpec((tq, LANES), lambda bi, g, i: (bi * nt + i, C_GNSA // LANES))
    return pl.pallas_call(
        functools.partial(_attn_prompt_kernel, tq=tq),
        grid=(b, KV_HEADS // gps, nt),
        in_specs=[qspec,
                  pl.BlockSpec((1, gps, n_cmp, LANES), lambda bi, g, i: (bi, g, 0, 0)),
                  pl.BlockSpec((1, gps, HEAD_DIM, n_cmp), lambda bi, g, i: (bi, g, 0, 0)),
                  kspec, vspec, kspec, vspec, gspec],
        out_specs=pl.BlockSpec((tq, gps * Q_PER_KV * HEAD_DIM), lambda bi, g, i: (bi * nt + i, g)),
        out_shape=jax.ShapeDtypeStruct((b * t, ATT_WIDTH), F32),
        scratch_shapes=[pltpu.VMEM((LANES, tq), F32)]
        + [pltpu.VMEM((SEL_CHUNKS * tq, Q_PER_KV * tq), F32) for _ in range(gps)]
        + [pltpu.VMEM((WINDOW + tq, Q_PER_KV * tq), F32) for _ in range(gps)],
        compiler_params=_cparams(("parallel", "parallel", "arbitrary")),
        name="attn_prompt",
    )(qa, kca, vct, kas, vts, kaw, vtw, z)


def _rglru_kernel(x_ref, hist_ref, h0_ref, cw_ref, cb_ref, wr_ref, br_ref, wi_ref, bi_ref, lam_ref,
                  h_ref, hl_ref, cs_ref, carry_ref, tail_ref, a_ref, u_ref, xe_ref, *, tt, nb):
    i = pl.program_id(1)

    @pl.when(i == 0)
    def _():
        carry_ref[...] = h0_ref[...]
        tail_ref[...] = hist_ref[...]

    cw = cw_ref[...]
    nl = -lam_ref[...]
    softplus = jnp.maximum(nl, 0.0) + jnp.log1p(jnp.exp(-jnp.abs(nl)))
    for k in range(nb):
        x = x_ref[k]
        xe_ref[k, 0:8, :] = tail_ref[k]
        xe_ref[k, 8:8 + tt, :] = x
        xc = cb_ref[...]
        for j in range(CONV_W):
            lo = 8 - (CONV_W - 1) + j
            xc = xc + (x if lo == 8 else xe_ref[k, lo:lo + tt, :]) * cw[j:j + 1]
        xb = xc.astype(BF16)

        def gates(w_ref, b_ref):
            parts = [jnp.dot(xb[:, c * MXU_DIM:(c + 1) * MXU_DIM], w_ref[c], preferred_element_type=F32)
                     for c in range(D_RNN // MXU_DIM)]
            z = jnp.concatenate(parts, axis=1) + b_ref[...]
            return 0.5 * jnp.tanh(0.5 * z) + 0.5

        r = gates(wr_ref, br_ref)
        ig = gates(wi_ref, bi_ref)
        log_a = -LRU_C * r * softplus
        a_ref[k] = jnp.exp(log_a)
        th = jnp.tanh(log_a)
        gap = -2.0 * th / (1.0 - th)
        root = jnp.where(gap > 0.0, gap * lax.rsqrt(gap), 0.0)
        u_ref[k] = root * (ig * xc)
        tail_ref[k] = x[tt - 8:tt]

    def step(t, hs):
        out = []
        for k in range(nb):
            h = a_ref[k, pl.ds(t, 1), :] * hs[k] + u_ref[k, pl.ds(t, 1), :]
            h_ref[k, pl.ds(t, 1), :] = h
            out.append(h)
        return tuple(out)

    hs = lax.fori_loop(0, tt, step, tuple(carry_ref[k] for k in range(nb)), unroll=8)
    for k in range(nb):
        carry_ref[k] = hs[k]

    @pl.when(i == pl.num_programs(1) - 1)
    def _():
        hl_ref[...] = carry_ref[...]
        cs_ref[...] = tail_ref[...]


def _rglru(z, b, t, hist8, h0, conv_w, conv_b, wr_bd, b_rg, wi_bd, b_ig, lam, tt, nb):
    nt = t // tt

    def const(shape):
        return pl.BlockSpec(shape, lambda gb, i: tuple(0 for _ in shape))

    def per_seq(rows):
        return pl.BlockSpec((nb, rows, D_RNN), lambda gb, i: (gb, 0, 0))

    row = lambda v: v.reshape(1, D_RNN)
    h, h_last, conv_tail = pl.pallas_call(
        functools.partial(_rglru_kernel, tt=tt, nb=nb),
        grid=(b // nb, nt),
        in_specs=[pl.BlockSpec((nb, tt, D_RNN), lambda gb, i: (gb, i, C_XRNN // D_RNN)),
                  per_seq(8), per_seq(1),
                  const((CONV_W, D_RNN)), const((1, D_RNN)),
                  const((D_RNN // MXU_DIM, MXU_DIM, MXU_DIM)), const((1, D_RNN)),
                  const((D_RNN // MXU_DIM, MXU_DIM, MXU_DIM)), const((1, D_RNN)),
                  const((1, D_RNN))],
        out_specs=[pl.BlockSpec((nb, tt, D_RNN), lambda gb, i: (gb, i, 0)), per_seq(1), per_seq(8)],
        out_shape=[jax.ShapeDtypeStruct((b, t, D_RNN), F32),
                   jax.ShapeDtypeStruct((b, 1, D_RNN), F32),
                   jax.ShapeDtypeStruct((b, 8, D_RNN), F32)],
        scratch_shapes=[pltpu.VMEM((nb, 1, D_RNN), F32), pltpu.VMEM((nb, 8, D_RNN), F32),
                        pltpu.VMEM((nb, tt, D_RNN), F32), pltpu.VMEM((nb, tt, D_RNN), F32),
                        pltpu.VMEM((nb, tt + 8, D_RNN), F32)],
        compiler_params=_cparams(("parallel", "arbitrary")),
        name="rglru",
    )(z.reshape(b, t, N_PACK), hist8, h0.reshape(b, 1, D_RNN), conv_w, row(conv_b),
      wr_bd, row(b_rg), wi_bd, row(b_ig), row(lam))
    return h.reshape(b * t, D_RNN), h_last, conv_tail


def _out_kernel(x_ref, oatt_ref, gatt_ref, h_ref, grnn_ref, gma_ref, gmr_ref, wa_ref, wr_ref, wo_ref, y_ref):
    a = (oatt_ref[...] * jax.nn.silu(gatt_ref[...])).astype(BF16)
    u_att = jnp.dot(a, wa_ref[...], preferred_element_type=F32)
    r = (h_ref[...] * jax.nn.silu(grnn_ref[...])).astype(BF16)
    u_rnn = jnp.dot(r, wr_ref[...], preferred_element_type=F32)
    m = jax.nn.sigmoid(gma_ref[...]) * u_att + jax.nn.sigmoid(gmr_ref[...]) * u_rnn
    y_ref[...] = x_ref[...] + jnp.dot(m.astype(BF16), wo_ref[...], preferred_element_type=F32)


def _out_proj(x2d, z, o_att, h_rnn, wa, wr, wo, tm):
    m = x2d.shape[0]

    def zspec(width, col):
        return pl.BlockSpec((tm, width), lambda i, c=col // width: (i, c))

    def wspec(shape):
        return pl.BlockSpec(shape, lambda i: (0, 0), pipeline_mode=pl.Buffered(1))

    rows = lambda width: pl.BlockSpec((tm, width), lambda i: (i, 0))
    return pl.pallas_call(
        _out_kernel,
        grid=(m // tm,),
        in_specs=[rows(D_MODEL), rows(ATT_WIDTH), zspec(ATT_WIDTH, C_GATT), rows(D_RNN), zspec(D_RNN, C_GRNN),
                  zspec(D_MODEL, C_GMATT), zspec(D_MODEL, C_GMRNN),
                  wspec((ATT_WIDTH, D_MODEL)), wspec((D_RNN, D_MODEL)), wspec((D_MODEL, D_MODEL))],
        out_specs=rows(D_MODEL),
        out_shape=jax.ShapeDtypeStruct((m, D_MODEL), F32),
        compiler_params=_cparams(("parallel",)),
        name="out_proj",
    )(x2d, o_att, z, h_rnn, z, z, z, wa, wr, wo)


def _prep_sample_kernel(zq_ref, zc_ref, zs_ref, zw_ref, qg_ref, ksg_ref, kwg_ref, bd_ref,
                        qn_ref, lc_ref, ls_ref, lw_ref):
    bd = bd_ref[...]
    qn_ref[...] = _seg_rms(zq_ref[...], bd) * qg_ref[...] * SCALE
    lc_ref[...] = zc_ref[...]
    for z_ref, g_ref, l_ref in ((zs_ref, ksg_ref, ls_ref), (zw_ref, kwg_ref, lw_ref)):
        z = z_ref[...]
        l_ref[:, :KV_WIDTH] = _seg_rms(z[:, :KV_WIDTH], bd) * g_ref[...]
        l_ref[:, KV_WIDTH:] = z[:, KV_WIDTH:]


def _prep_sample(z, q_gain, ks_gain, kw_gain, bd):
    m = z.shape[0]

    def zspec(width, col):
        return pl.BlockSpec((m, width), lambda i, c=col // width: (0, c))

    def const(shape):
        return pl.BlockSpec(shape, lambda i: tuple(0 for _ in shape))

    full = lambda width: pl.BlockSpec((m, width), lambda i: (0, 0))
    leaf = jax.ShapeDtypeStruct((m, 2 * KV_WIDTH), F32)
    return pl.pallas_call(
        _prep_sample_kernel,
        grid=(1,),
        in_specs=[zspec(ATT_WIDTH, C_Q), zspec(512, C_KV), zspec(512, C_KV + 512), zspec(512, C_KV + 1024),
                  const((1, ATT_WIDTH)), const((1, KV_WIDTH)), const((1, KV_WIDTH)), const((MXU_DIM, MXU_DIM))],
        out_specs=[full(ATT_WIDTH), full(512), full(512), full(512)],
        out_shape=[jax.ShapeDtypeStruct((m, ATT_WIDTH), F32), leaf, leaf, leaf],
        compiler_params=_cparams(("arbitrary",)),
        name="prep_sample",
    )(z, z, z, z, q_gain, ks_gain, kw_gain, bd)


def _page_specs(n):
    return [pl.BlockSpec((1, 2 * KV_WIDTH, PAGE_SIZE),
                         lambda bi, s, pt, k=k, n=n: (pt[bi, s * n + k], 0, 0)) for k in range(n)]


def _compress_pages_kernel(pt_ref, *refs):
    del pt_ref
    pages = refs[:CMP_PAGES]
    wt_ref, seg_ref, ok_ref, ov_ref = refs[CMP_PAGES:]
    wt = wt_ref[...]
    for tile in range(CMP_PAGES // CMP_TILE_PAGES):
        acc_k = jnp.zeros((KV_WIDTH, LANES), F32)
        acc_v = jnp.zeros((KV_WIDTH, LANES), F32)
        for pair in range(CMP_TILE_PAGES // 2):
            first = tile * CMP_TILE_PAGES + 2 * pair
            pa = pages[first][0] * wt
            pb = pages[first + 1][0] * wt
            seg = seg_ref[pair]
            lhs_k = jnp.concatenate([pa[:KV_WIDTH], pb[:KV_WIDTH]], axis=1).astype(BF16)
            lhs_v = jnp.concatenate([pa[KV_WIDTH:], pb[KV_WIDTH:]], axis=1).astype(BF16)
            acc_k = acc_k + jnp.dot(lhs_k, seg, preferred_element_type=F32)
            acc_v = acc_v + jnp.dot(lhs_v, seg, preferred_element_type=F32)
        ok_ref[0, :, tile * LANES:(tile + 1) * LANES] = acc_k
        ov_ref[0, :, tile * LANES:(tile + 1) * LANES] = acc_v


def _compress_pages(page_table, pool_t, wt, seg):
    b, n_pages = page_table.shape
    ns = n_pages // CMP_PAGES
    cols = CMP_PAGES // CMP_TILE_PAGES * LANES
    out_spec = pl.BlockSpec((1, KV_WIDTH, cols), lambda bi, s, pt: (bi, 0, s))
    grid_spec = pltpu.PrefetchScalarGridSpec(
        num_scalar_prefetch=1,
        grid=(b, ns),
        in_specs=_page_specs(CMP_PAGES) + [
            pl.BlockSpec((2 * KV_WIDTH, PAGE_SIZE), lambda bi, s, pt: (0, 0)),
            pl.BlockSpec((CMP_TILE_PAGES // 2, 2 * PAGE_SIZE, LANES), lambda bi, s, pt: (0, 0, 0))],
        out_specs=[out_spec, out_spec],
    )
    shape = jax.ShapeDtypeStruct((b, KV_WIDTH, ns * cols), F32)
    return pl.pallas_call(
        _compress_pages_kernel,
        grid_spec=grid_spec,
        out_shape=[shape, shape],
        compiler_params=_cparams(("parallel", "arbitrary")),
        name="compress_pages",
    )(page_table, *([pool_t] * CMP_PAGES), wt, seg)


def _softmax_lanes(parts, masks):
    parts = [jnp.where(mk, s, NEG) for s, mk in zip(parts, masks)]
    mx = functools.reduce(jnp.maximum, [jnp.max(s, axis=1, keepdims=True) for s in parts])
    es = [jnp.where(mk, jnp.exp(s - mx), 0.0) for s, mk in zip(parts, masks)]
    den = jnp.maximum(sum(jnp.sum(e, axis=1, keepdims=True) for e in es), TINY)
    return [e / den for e in es]


def _sample_cmp_win_kernel(qbd_ref, ck_ref, cv_ref, kg_ref, cwin_ref, wnew_ref,
                           ocmp_ref, owin_ref, bias_ref, wst_ref, *, past_len, t_new):
    qbd = qbd_ref[0]
    rows = qbd.shape[0]
    ck = ck_ref[0]
    n_cmp = ck.shape[1]
    parts = []
    for g in range(KV_HEADS):
        xs = ck[g * HEAD_DIM:(g + 1) * HEAD_DIM]
        parts.append(xs * lax.rsqrt(jnp.mean(xs * xs, axis=0, keepdims=True) + EPS))
    kc = (jnp.concatenate(parts, axis=0) * kg_ref[...]).astype(BF16)

    t_row = _row_iota((rows, n_cmp)) % t_new
    n_lane = _lane_iota((rows, n_cmp))
    cmp_id = LANES * (n_lane // LANES) + 2 * (n_lane % (LANES // 2)) + (n_lane % LANES) // (LANES // 2)
    vis = (cmp_id + 1) * CMP_BLOCK - 1 <= past_len + t_row
    (p,) = _softmax_lanes([jnp.dot(qbd, kc, preferred_element_type=F32)], [vis])
    ocmp_ref[0] = _dot_nt(p.astype(BF16), cv_ref[0].astype(BF16))

    gt = KV_HEADS * t_new
    psum = p[0:gt]
    for r in range(1, Q_PER_KV):
        psum = psum + p[r * gt:(r + 1) * gt]
    halves = []
    for c in range(n_cmp // LANES):
        tile = psum[:, c * LANES:(c + 1) * LANES]
        halves.append(tile + pltpu.roll(tile, LANES // 2, 1))
    if len(halves) == 1:
        imp = halves[0][:, :LANES // 2]
    else:
        low = _lane_iota((gt, LANES)) < LANES // 2
        imp = jnp.concatenate([jnp.where(low, halves[c], halves[c + 1]) for c in range(0, len(halves), 2)], axis=1)
    n_blk = n_cmp // 2
    j = _lane_iota((gt, n_blk))
    cur = (past_len + _row_iota((gt, n_blk)) % t_new) // SEL_BLOCK
    cand = j < cur
    forced = (j == 0) | (j == cur - 1)
    score = jnp.where(cand, jnp.where(forced, jnp.inf, imp), -jnp.inf)
    width = min(LANES, n_blk)
    tiles = [score[:, k * width:(k + 1) * width] for k in range(n_blk // width)]
    counts = [jnp.zeros((gt, width), jnp.int32) for _ in tiles]
    lane_w = _lane_iota((gt, width))
    for c in range(n_blk):
        col = score[:, c:c + 1]
        for k, tile in enumerate(tiles):
            if k < c // width:
                inc = jnp.where(col > tile, 1, 0)
            elif k > c // width:
                inc = jnp.where(col >= tile, 1, 0)
            else:
                inc = jnp.where(lane_w > c % width, jnp.where(col >= tile, 1, 0), jnp.where(col > tile, 1, 0))
            counts[k] = counts[k] + inc
    count = counts[0] if len(counts) == 1 else jnp.concatenate(counts, axis=1)
    keep = (count < N_SEL - 1) & cand
    bias_ref[0] = jnp.where(keep, 0.0, SEL_BIAS)

    cwin = cwin_ref[0]
    wb = cwin.shape[1]
    wnew = wnew_ref[0]
    wnew_p = jnp.concatenate([wnew, jnp.zeros((LANES - t_new, 2 * KV_WIDTH), F32)], axis=0)
    t_w = _row_iota((rows, wb)) % t_new
    idx = _lane_iota((rows, wb))
    ok_w = (idx <= wb + t_w) & (idx > wb + t_w - WINDOW)
    t_n = _row_iota((rows, LANES)) % t_new
    idx_n = wb + _lane_iota((rows, LANES))
    ok_n = (idx_n <= wb + t_n) & (idx_n > wb + t_n - WINDOW) & (_lane_iota((rows, LANES)) < t_new)
    p_w, p_n = _softmax_lanes([jnp.dot(qbd, cwin[:KV_WIDTH].astype(BF16), preferred_element_type=F32),
                               _dot_nt(qbd, wnew_p[:, :KV_WIDTH].astype(BF16))], [ok_w, ok_n])
    owin_ref[0] = (_dot_nt(p_w.astype(BF16), cwin[KV_WIDTH:].astype(BF16))
                   + jnp.dot(p_n.astype(BF16), wnew_p[:, KV_WIDTH:].astype(BF16), preferred_element_type=F32))

    keep_lanes = LANES - t_new
    new_t = jnp.concatenate([jnp.zeros((keep_lanes, 2 * KV_WIDTH), F32), wnew], axis=0).T
    rolled = [pltpu.roll(cwin[:, c * LANES:(c + 1) * LANES], keep_lanes, 1) for c in range(wb // LANES)]
    rolled.append(new_t)
    first = _lane_iota((2 * KV_WIDTH, LANES)) < keep_lanes
    for c in range(wb // LANES):
        wst_ref[0, :, c * LANES:(c + 1) * LANES] = jnp.where(first, rolled[c], rolled[c + 1])


def _sample_cmp_win(qbd, ck_t, cv_t, kc_gain_col, cwin_t, wnew, past_len):
    b, rows, _ = qbd.shape
    n_cmp = ck_t.shape[2]
    wb = cwin_t.shape[2]
    t_new = wnew.shape[1]
    gt = KV_HEADS * t_new

    def per_b(shape):
        return pl.BlockSpec((1,) + shape, lambda bi: (bi, 0, 0))

    return pl.pallas_call(
        functools.partial(_sample_cmp_win_kernel, past_len=past_len, t_new=t_new),
        grid=(b,),
        in_specs=[per_b((rows, MXU_DIM)), per_b((KV_WIDTH, n_cmp)), per_b((KV_WIDTH, n_cmp)),
                  pl.BlockSpec((KV_WIDTH, 1), lambda bi: (0, 0)),
                  per_b((2 * KV_WIDTH, wb)), per_b((t_new, 2 * KV_WIDTH))],
        out_specs=[per_b((rows, MXU_DIM)), per_b((rows, MXU_DIM)), per_b((gt, n_cmp // 2)), per_b((2 * KV_WIDTH, wb))],
        out_shape=[jax.ShapeDtypeStruct((b, rows, MXU_DIM), F32), jax.ShapeDtypeStruct((b, rows, MXU_DIM), F32),
                   jax.ShapeDtypeStruct((b, gt, n_cmp // 2), F32), jax.ShapeDtypeStruct((b, 2 * KV_WIDTH, wb), F32)],
        compiler_params=_cparams(("parallel",)),
        name="sample_cmp_win",
    )(qbd, ck_t, cv_t, kc_gain_col, cwin_t, wnew)


def _sample_sel_kernel(pt_ref, *refs, t_new):
    del pt_ref
    pages = refs[:SEL_PAGES]
    (qbd_ref, bsel_ref, onehot_ref, snew_ref, ocmp_ref, owin_ref, gl_ref,
     o_ref, m_ref, l_ref, acc_ref) = refs[SEL_PAGES:SEL_PAGES + 11]
    s_refs = refs[SEL_PAGES + 11:]
    s_idx = pl.program_id(1)
    qbd = qbd_ref[0]
    rows = qbd.shape[0]

    @pl.when(s_idx == 0)
    def _():
        m_ref[...] = jnp.full(m_ref.shape, NEG, F32)
        l_ref[...] = jnp.zeros(l_ref.shape, F32)
        acc_ref[...] = jnp.zeros(acc_ref.shape, F32)

    def partial_softmax(s, pv):
        m = jnp.max(s, axis=1, keepdims=True)
        pe = jnp.exp(s - m)
        return m, jnp.sum(pe, axis=1, keepdims=True), pv(pe.astype(BF16))

    def update(parts):
        m_old = m_ref[...]
        m_new = functools.reduce(jnp.maximum, [p[0] for p in parts], m_old)
        scale = jnp.exp(m_old - m_new)
        l = scale * l_ref[...]
        acc = scale * acc_ref[...]
        for m, l_part, acc_part in parts:
            c = jnp.exp(m - m_new)
            l = l + c * l_part
            acc = acc + c * acc_part
        m_ref[...] = m_new
        l_ref[...] = l
        acc_ref[...] = acc

    bsel = bsel_ref[0, 0]
    keys = SEL_SUB * PAGE_SIZE
    for g, s_ref in enumerate(s_refs):
        pgs = pages[g * SEL_SUB:(g + 1) * SEL_SUB]
        k_t = jnp.concatenate([pg[0, :KV_WIDTH, :].astype(BF16) for pg in pgs], axis=1)
        s_ref[...] = (jnp.dot(qbd, k_t, preferred_element_type=F32)
                      + jnp.dot(bsel, onehot_ref[:, g * keys:(g + 1) * keys], preferred_element_type=F32))
    parts = []
    for g, s_ref in enumerate(s_refs):
        pgs = pages[g * SEL_SUB:(g + 1) * SEL_SUB]
        v_t = jnp.concatenate([pg[0, KV_WIDTH:, :].astype(BF16) for pg in pgs], axis=1)
        parts.append(partial_softmax(s_ref[...], lambda pb, v_t=v_t: _dot_nt(pb, v_t)))
    update(parts)

    @pl.when(s_idx == pl.num_programs(1) - 1)
    def _():
        snew = jnp.concatenate([snew_ref[0], jnp.zeros((LANES - t_new, 2 * KV_WIDTH), F32)], axis=0)
        t_q = _row_iota((rows, LANES)) % t_new
        t_k = _lane_iota((rows, LANES))
        s = jnp.where((t_k <= t_q) & (t_k < t_new), _dot_nt(qbd, snew[:, :KV_WIDTH].astype(BF16)), NEG)
        v_new = snew[:, KV_WIDTH:].astype(BF16)
        update([partial_softmax(s, lambda pb: jnp.dot(pb, v_new, preferred_element_type=F32))])
        o_sel = acc_ref[...] / l_ref[...]
        sig = jax.nn.sigmoid(gl_ref[0])
        o_ref[0] = sig[:, 0:1] * ocmp_ref[0] + sig[:, 1:2] * o_sel + sig[:, 2:3] * owin_ref[0]


def _sample_sel(page_table, pool_t, qbd, bsel, onehot, snew, ocmp, owin, gl):
    b, n_pages = page_table.shape
    ns = n_pages // SEL_PAGES
    rows = qbd.shape[1]
    t_new = snew.shape[1]
    keys = SEL_PAGES * PAGE_SIZE

    def per_b(shape):
        return pl.BlockSpec((1,) + shape, lambda bi, s, pt: (bi,) + tuple(0 for _ in shape))

    grid_spec = pltpu.PrefetchScalarGridSpec(
        num_scalar_prefetch=1,
        grid=(b, ns),
        in_specs=_page_specs(SEL_PAGES) + [
            per_b((rows, MXU_DIM)),
            pl.BlockSpec((1, 1, rows, LANES), lambda bi, s, pt: (bi, s, 0, 0)),
            pl.BlockSpec((LANES, keys), lambda bi, s, pt: (0, 0)),
            per_b((t_new, 2 * KV_WIDTH)), per_b((rows, MXU_DIM)), per_b((rows, MXU_DIM)), per_b((rows, LANES))],
        out_specs=per_b((rows, MXU_DIM)),
        scratch_shapes=[pltpu.VMEM((rows, 1), F32), pltpu.VMEM((rows, 1), F32), pltpu.VMEM((rows, MXU_DIM), F32)]
        + [pltpu.VMEM((rows, SEL_SUB * PAGE_SIZE), F32) for _ in range(SEL_PAGES // SEL_SUB)],
    )
    return pl.pallas_call(
        functools.partial(_sample_sel_kernel, t_new=t_new),
        grid_spec=grid_spec,
        out_shape=jax.ShapeDtypeStruct((b, rows, MXU_DIM), F32),
        compiler_params=_cparams(("parallel", "arbitrary")),
        name="sample_sel",
    )(page_table, *([pool_t] * SEL_PAGES), qbd, bsel, onehot, snew, ocmp, owin, gl)


def _pack_weights(w_in, q_norm_g, k_norm_g, w_cmp, w_rg, w_ig, w_att_out, w_rnn_out, w_out):
    def block_diag(w):
        per = MXU_DIM // HEAD_DIM
        w4 = w.reshape(RNN_BLOCKS // per, per, HEAD_DIM, HEAD_DIM)
        eye = jnp.eye(per, dtype=w.dtype)
        return jnp.einsum('cpde,pq->cpdqe', w4, eye).reshape(RNN_BLOCKS // per, MXU_DIM, MXU_DIM).astype(BF16)

    wt = jnp.broadcast_to(w_cmp.transpose(1, 2, 0)[:, None], (2, KV_HEADS, HEAD_DIM, CMP_BLOCK))
    wt = jnp.tile(wt, (1, 1, 1, PAGE_SIZE // CMP_BLOCK)).reshape(2 * KV_WIDTH, PAGE_SIZE)
    lane = jnp.arange(2 * PAGE_SIZE)
    pair = jnp.arange(CMP_TILE_PAGES // 2)
    n_local = ((PAGE_SIZE // CMP_BLOCK) * (2 * pair[:, None] + lane[None, :] // PAGE_SIZE)
               + (lane[None, :] % PAGE_SIZE) // CMP_BLOCK)
    col = (n_local % 2) * (LANES // 2) + n_local // 2
    seg = (jnp.arange(LANES)[None, None, :] == col[:, :, None]).astype(BF16)

    return dict(
        w_t=w_in.T,
        q_gain=jnp.tile(q_norm_g, N_HEADS).reshape(1, ATT_WIDTH),
        kc_gain=jnp.tile(k_norm_g[0], KV_HEADS).reshape(1, KV_WIDTH),
        ks_gain=jnp.tile(k_norm_g[1], KV_HEADS).reshape(1, KV_WIDTH),
        kw_gain=jnp.tile(k_norm_g[2], KV_HEADS).reshape(1, KV_WIDTH),
        wc=jnp.broadcast_to(w_cmp[:, :, None, :], (CMP_BLOCK, 2, KV_HEADS, HEAD_DIM)).reshape(CMP_BLOCK, 2 * KV_WIDTH),
        wt=wt, seg=seg,
        wr_bd=block_diag(w_rg), wi_bd=block_diag(w_ig),
        wa=w_att_out.astype(BF16), wr=w_rnn_out.astype(BF16), wo=w_out.astype(BF16),
        bd=_block_diag_ones(),
    )


def _prompt_layer(x, pw, norm_g, conv_w, conv_b, b_rg, b_ig, lam):
    b, t, _ = x.shape
    tq = ATTN_TILE
    x2d = x.reshape(b * t, D_MODEL)
    z = _proj(x2d, norm_g, pw['w_t'], tm=min(1024, b * t))
    qa, leaf_c, leaf_s, leaf_w, cc, kas, vts, kaw, vtw = _prep_prompt(
        z, b, t, pw['q_gain'], pw['ks_gain'], pw['kw_gain'], pw['wc'], pw['bd'], te=256, tq=tq)
    n_cmp = t // CMP_BLOCK
    cc_perm = cc.reshape(b, n_cmp // 2, 2, 2 * KV_WIDTH).transpose(0, 2, 1, 3).reshape(b, n_cmp, 2 * KV_WIDTH)
    kca, vct = _cmp_prep(cc_perm, pw['kc_gain'], pw['bd'])
    o_att = _attn_prompt(z, qa, kca, vct, kas, vts, kaw, vtw, b, t, tq)
    h_rnn, h_last, conv_tail = _rglru(z, b, t, jnp.zeros((b, 8, D_RNN), F32), jnp.zeros((b, D_RNN), F32),
                                      conv_w, conv_b, pw['wr_bd'], b_rg, pw['wi_bd'], b_ig, lam, tt=256, nb=b)
    y = _out_proj(x2d, z, o_att, h_rnn, pw['wa'], pw['wr'], pw['wo'], tm=256)
    def kv(leaf_t):
        tokens = leaf_t.shape[2]
        return leaf_t.reshape(b, 2, KV_HEADS, HEAD_DIM, tokens).transpose(0, 4, 1, 2, 3)

    w_keep = min(WINDOW, t)
    return y.reshape(b, t, D_MODEL), (kv(leaf_c), kv(leaf_s), kv(leaf_w[:, :, t - w_keep:]),
                                      h_last.reshape(b, D_RNN), conv_tail[:, 8 - (CONV_W - 1):])


def _feature_major(cache):
    n, tokens = cache.shape[:2]
    return cache.transpose(0, 2, 3, 4, 1).reshape(n, 2 * KV_WIDTH, tokens)


def _sample_layer(x, cache_cmp, cache_sel, cache_win, state_h, state_conv, page_table, pw,
                  norm_g, conv_w, conv_b, b_rg, b_ig, lam):
    b, t, _ = x.shape
    n_pages = page_table.shape[1]
    past_len = n_pages * PAGE_SIZE
    rows = N_HEADS * t
    x2d = x.reshape(b * t, D_MODEL)
    z = _proj(x2d, norm_g, pw['w_t'], tm=b * t)
    qn, leaf_c, leaf_s, leaf_w = _prep_sample(z, pw['q_gain'], pw['ks_gain'], pw['kw_gain'], pw['bd'])

    q5 = qn.reshape(b, t, KV_HEADS, Q_PER_KV, HEAD_DIM).transpose(0, 3, 2, 1, 4)
    qbd = (q5[:, :, :, :, None, :] * jnp.eye(KV_HEADS, dtype=F32)[None, None, :, None, :, None])
    qbd = qbd.reshape(b, rows, KV_WIDTH).astype(BF16)

    ck_t, cv_t = _compress_pages(page_table, _feature_major(cache_cmp), pw['wt'], pw['seg'])
    wb = cache_win.shape[1]
    o_cmp, o_win, bias, win_state_t = _sample_cmp_win(
        qbd, ck_t, cv_t, pw['kc_gain'].reshape(KV_WIDTH, 1), _feature_major(cache_win),
        leaf_w.reshape(b, t, 2 * KV_WIDTH), past_len)
    win_state = win_state_t.reshape(b, 2, KV_HEADS, HEAD_DIM, wb).transpose(0, 4, 1, 2, 3)

    ns = n_pages // SEL_PAGES
    blocks_per_step = SEL_PAGES * PAGE_SIZE // SEL_BLOCK
    bsel = bias.reshape(b, KV_HEADS * t, ns, blocks_per_step).transpose(0, 2, 1, 3)
    bsel = jnp.tile(bsel, (1, 1, Q_PER_KV, 1))
    bsel = jnp.pad(bsel, ((0, 0), (0, 0), (0, 0), (0, LANES - blocks_per_step))).astype(BF16)
    key_block = jnp.arange(SEL_PAGES * PAGE_SIZE) // SEL_BLOCK
    onehot = (jnp.arange(LANES)[:, None] == key_block[None, :]).astype(BF16)
    gl = z[:, C_GNSA:C_GNSA + 3 * N_HEADS].reshape(b, t, 3, KV_HEADS, Q_PER_KV)
    gl = jnp.pad(gl.transpose(0, 4, 3, 1, 2).reshape(b, rows, 3), ((0, 0), (0, 0), (0, LANES - 3)))
    o_full = _sample_sel(page_table, _feature_major(cache_sel), qbd, bsel, onehot,
                         leaf_s.reshape(b, t, 2 * KV_WIDTH), o_cmp, o_win, gl)
    o6 = o_full.reshape(b, Q_PER_KV, KV_HEADS, t, KV_HEADS, HEAD_DIM)
    o_att = jnp.stack([o6[:, :, g, :, g, :] for g in range(KV_HEADS)], axis=2)
    o_att = o_att.transpose(0, 3, 2, 1, 4).reshape(b * t, ATT_WIDTH)

    hist8 = jnp.pad(state_conv, ((0, 0), (8 - (CONV_W - 1), 0), (0, 0)))
    h_rnn, h_last, conv_tail = _rglru(z, b, t, hist8, state_h, conv_w, conv_b,
                                      pw['wr_bd'], b_rg, pw['wi_bd'], b_ig, lam, tt=t,
                                      nb=8 if b % 8 == 0 else b)
    y = _out_proj(x2d, z, o_att, h_rnn, pw['wa'], pw['wr'], pw['wo'], tm=min(256, b * t))
    kv = lambda leaf: leaf.reshape(b, -1, 2, KV_HEADS, HEAD_DIM)
    return y.reshape(b, t, D_MODEL), (kv(leaf_c), kv(leaf_s), win_state,
                                      h_last.reshape(b, D_RNN), conv_tail[:, 8 - (CONV_W - 1):])


def kernel(x_prompt, x_sample, cache_cmp, cache_sel, cache_win, state_h, state_conv, page_table,
           norm_g, w_in, q_norm_g, k_norm_g, w_cmp, conv_w, conv_b, w_rg, b_rg, w_ig, b_ig,
           lru_lambda, w_att_out, w_rnn_out, w_out):
    yp, ys = x_prompt, x_sample
    outs_p, outs_s = [], []
    for l in range(w_in.shape[0]):
        pw = _pack_weights(w_in[l], q_norm_g[l], k_norm_g[l], w_cmp[l], w_rg[l], w_ig[l],
                           w_att_out[l], w_rnn_out[l], w_out[l])
        yp, st_p = _prompt_layer(yp, pw, norm_g[l], conv_w[l], conv_b[l], b_rg[l], b_ig[l], lru_lambda[l])
        ys, st_s = _sample_layer(ys, cache_cmp[l], cache_sel[l], cache_win[l], state_h[l], state_conv[l],
                                 page_table, pw, norm_g[l], conv_w[l], conv_b[l], b_rg[l], b_ig[l], lru_lambda[l])
        outs_p.append(st_p)
        outs_s.append(st_s)
    cmp_p, sel_p, win_p, h_p, conv_p = [jnp.stack(a) for a in zip(*outs_p)]
    cmp_s, sel_s, win_s, h_s, conv_s = [jnp.stack(a) for a in zip(*outs_s)]
    return (yp, ys, cmp_p, sel_p, win_p, h_p, conv_p, cmp_s, sel_s, win_s, h_s, conv_s)
```

```python
import functools

import jax
import jax.numpy as jnp
from jax import lax
from jax.experimental import pallas as pl
from jax.experimental.pallas import tpu as pltpu

F32 = jnp.float32
BF16 = jnp.bfloat16

D_MODEL = 2048
N_HEADS = 16
HEAD_DIM = 64
KV_HEADS = 4
Q_PER_KV = N_HEADS // KV_HEADS
ATT_WIDTH = N_HEADS * HEAD_DIM
KV_WIDTH = KV_HEADS * HEAD_DIM
CMP_BLOCK = 32
SEL_BLOCK = 64
N_SEL = 16
WINDOW = 512
SCALE = HEAD_DIM ** -0.5
LOG2E = 1.4426950408889634
D_RNN = 1024
RNN_BLOCKS = 16
CONV_W = 4
LRU_C = 8.0
PAGE_SIZE = 128
EPS = 1e-6
NEG = -1e30
TINY = 1e-30
SEL_BIAS = -1e30

LANES = 128
MXU_DIM = 256
VMEM_LIMIT = 56 * 1024 * 1024

C_Q = 0
C_GATT = 1024
C_XRNN = 2048
C_GRNN = 3072
C_GMATT = 4096
C_GMRNN = 6144
C_KV = 8192
C_GNSA = 9728
N_PACK = 10240
PROJ_TILE = 1024
_W_Q, _W_KV, _W_GNSA, _W_GATT, _W_XRNN, _W_GRNN, _W_GMATT, _W_GMRNN = 0, 1024, 2560, 2608, 3632, 4656, 5680, 7728
PROJ_ROW_STARTS = (_W_Q, _W_GATT, _W_XRNN, _W_GRNN, _W_GMATT, _W_GMATT + 1024, _W_GMRNN, _W_GMRNN + 1024,
                   _W_KV, _W_KV + 1024)

SEL_PAGES = 64
SEL_SUB = 16
CMP_PAGES = 64
CMP_TILE_PAGES = 32
GROUPS_PER_STEP = 2
ATTN_TILE = 256
SEL_CHUNKS = 2
VT_ROWS = 80


def _cparams(sem):
    return pltpu.CompilerParams(dimension_semantics=sem, vmem_limit_bytes=VMEM_LIMIT)


def _proj_kernel(starts_ref, x_ref, g_ref, w_ref, o_ref, xn_ref):
    del starts_ref
    @pl.when(pl.program_id(1) == 0)
    def _():
        x = x_ref[...]
        ms = jnp.mean(x * x, axis=-1, keepdims=True)
        xn_ref[...] = (x * lax.rsqrt(ms + EPS) * g_ref[...]).astype(BF16)

    o_ref[...] = lax.dot_general(xn_ref[...], w_ref[...].astype(BF16), (((1,), (1,)), ((), ())),
                                 preferred_element_type=F32)


def _proj(x2d, norm_g, w_t, tm):
    m = x2d.shape[0]
    tn = PROJ_TILE
    grid_spec = pltpu.PrefetchScalarGridSpec(
        num_scalar_prefetch=1,
        grid=(m // tm, N_PACK // tn),
        in_specs=[
            pl.BlockSpec((tm, D_MODEL), lambda i, j, st: (i, 0)),
            pl.BlockSpec((1, D_MODEL), lambda i, j, st: (0, 0)),
            pl.BlockSpec((pl.Element(tn), pl.Element(D_MODEL)), lambda i, j, st: (pl.multiple_of(st[j], 16), 0)),
        ],
        out_specs=pl.BlockSpec((tm, tn), lambda i, j, st: (i, j)),
        scratch_shapes=[pltpu.VMEM((tm, D_MODEL), BF16)],
    )
    return pl.pallas_call(
        _proj_kernel,
        grid_spec=grid_spec,
        out_shape=jax.ShapeDtypeStruct((m, N_PACK), F32),
        compiler_params=_cparams(("parallel", "arbitrary")),
        name="proj",
    )(jnp.asarray(PROJ_ROW_STARTS, jnp.int32), x2d, norm_g.reshape(1, D_MODEL), w_t)


def _lane_iota(shape):
    return lax.broadcasted_iota(jnp.int32, shape, len(shape) - 1)


def _row_iota(shape):
    return lax.broadcasted_iota(jnp.int32, shape, 0)


def _seg_rms(x, bd):
    outs = []
    for c in range(x.shape[1] // MXU_DIM):
        xc = x[:, c * MXU_DIM:(c + 1) * MXU_DIM]
        x2 = xc * xc
        hi = x2.astype(BF16)
        lo = (x2 - hi.astype(F32)).astype(BF16)
        ss = (jnp.dot(hi, bd, preferred_element_type=F32)
              + jnp.dot(lo, bd, preferred_element_type=F32))
        outs.append(xc * lax.rsqrt(ss * (1.0 / HEAD_DIM) + EPS))
    return outs[0] if len(outs) == 1 else jnp.concatenate(outs, axis=1)


def _head_lo(x, h):
    tile = x[:, (h // 2) * LANES:(h // 2 + 1) * LANES]
    if h % 2:
        tile = pltpu.roll(tile, HEAD_DIM, 1)
    return jnp.where(_lane_iota(tile.shape) < HEAD_DIM, tile, 0.0)


def _heads_t(x):
    outs = []
    for c in range(x.shape[1] // LANES):
        xt = x[:, c * LANES:(c + 1) * LANES].T
        outs += [xt[:HEAD_DIM], xt[HEAD_DIM:]]
    return outs


def _dot_nt(a, b):
    return lax.dot_general(a, b, (((1,), (1,)), ((), ())), preferred_element_type=F32)


def _block_diag_ones():
    r = jnp.arange(MXU_DIM) // HEAD_DIM
    return (r[:, None] == r[None, :]).astype(BF16)


def _prep_prompt_kernel(zq_ref, zc_ref, zs_ref, zw_ref, qg_ref, ksg_ref, kwg_ref, wc_ref, bd_ref,
                        qa_ref, lc_ref, ls_ref, lw_ref, cc_ref, kas_ref, vts_ref, kaw_ref, vtw_ref, *, te, tq):
    i = pl.program_id(1)
    bd = bd_ref[...]
    qn = _seg_rms(zq_ref[...], bd) * qg_ref[...] * (SCALE * LOG2E)
    for h in range(N_HEADS):
        qa_ref[0, h // Q_PER_KV, h % Q_PER_KV] = _head_lo(qn, h).astype(BF16)

    zc = zc_ref[...]
    for c in range(2 * KV_WIDTH // LANES):
        lc_ref[0, c * LANES:(c + 1) * LANES, :] = zc[:, c * LANES:(c + 1) * LANES].T
    cc_ref[0] = jnp.sum(zc.reshape(te // CMP_BLOCK, CMP_BLOCK, 2 * KV_WIDTH) * wc_ref[...][None], axis=1)

    lane = _lane_iota((te, LANES))
    own_block = (i * te + _row_iota((te, LANES))) // SEL_BLOCK
    onehot = jnp.where(lane - HEAD_DIM == own_block, 1.0, 0.0)
    ones_rows = jnp.where(_row_iota((VT_ROWS - HEAD_DIM, te)) == 0, 1.0, 0.0).astype(BF16)

    for z_ref, g_ref, l_ref, ka_ref, vt_ref, with_onehot in (
            (zs_ref, ksg_ref, ls_ref, kas_ref, vts_ref, True),
            (zw_ref, kwg_ref, lw_ref, kaw_ref, vtw_ref, False)):
        z = z_ref[...]
        kn = _seg_rms(z[:, :KV_WIDTH], bd) * g_ref[...]
        v = z[:, KV_WIDTH:]
        for c in range(KV_WIDTH // LANES):
            l_ref[0, c * LANES:(c + 1) * LANES, :] = kn[:, c * LANES:(c + 1) * LANES].T
        for g, vt in enumerate(_heads_t(v)):
            l_ref[0, KV_WIDTH + g * HEAD_DIM:KV_WIDTH + (g + 1) * HEAD_DIM, :] = vt
            ka = _head_lo(kn, g)
            if with_onehot:
                ka = jnp.where(lane < HEAD_DIM, ka, onehot)
            ka_ref[0, g] = ka.astype(BF16)
            vt_aug = jnp.concatenate([vt.astype(BF16), ones_rows], axis=0)
            for c in range(te // tq):
                vt_ref[0, g, c] = vt_aug[:, c * tq:(c + 1) * tq]


def _prep_prompt(z, b, t, q_gain, ks_gain, kw_gain, wc, bd, te, tq):
    nt = t // te

    def zspec(width, col):
        return pl.BlockSpec((te, width), lambda bi, i, c=col // width: (bi * nt + i, c))

    def const(shape):
        return pl.BlockSpec(shape, lambda bi, i: tuple(0 for _ in shape))

    leaf_spec = pl.BlockSpec((1, 2 * KV_WIDTH, te), lambda bi, i: (bi, 0, i))
    leaf_shape = jax.ShapeDtypeStruct((b, 2 * KV_WIDTH, t), F32)
    k_spec = pl.BlockSpec((1, KV_HEADS, te, LANES), lambda bi, i: (bi, 0, i, 0))
    k_shape = jax.ShapeDtypeStruct((b, KV_HEADS, t, LANES), BF16)
    vt_spec = pl.BlockSpec((1, KV_HEADS, te // tq, VT_ROWS, tq), lambda bi, i: (bi, 0, i, 0, 0))
    vt_shape = jax.ShapeDtypeStruct((b, KV_HEADS, t // tq, VT_ROWS, tq), BF16)
    return pl.pallas_call(
        functools.partial(_prep_prompt_kernel, te=te, tq=tq),
        grid=(b, nt),
        in_specs=[zspec(ATT_WIDTH, C_Q), zspec(512, C_KV), zspec(512, C_KV + 512), zspec(512, C_KV + 1024),
                  const((1, ATT_WIDTH)), const((1, KV_WIDTH)), const((1, KV_WIDTH)),
                  const((CMP_BLOCK, 2 * KV_WIDTH)), const((MXU_DIM, MXU_DIM))],
        out_specs=[pl.BlockSpec((1, KV_HEADS, Q_PER_KV, te, LANES), lambda bi, i: (bi, 0, 0, i, 0)),
                   leaf_spec, leaf_spec, leaf_spec,
                   pl.BlockSpec((1, te // CMP_BLOCK, 2 * KV_WIDTH), lambda bi, i: (bi, i, 0)),
                   k_spec, vt_spec, k_spec, vt_spec],
        out_shape=[jax.ShapeDtypeStruct((b, KV_HEADS, Q_PER_KV, t, LANES), BF16),
                   leaf_shape, leaf_shape, leaf_shape,
                   jax.ShapeDtypeStruct((b, t // CMP_BLOCK, 2 * KV_WIDTH), F32),
                   k_shape, vt_shape, k_shape, vt_shape],
        compiler_params=_cparams(("parallel", "parallel")),
        name="prep_prompt",
    )(z, z, z, z, q_gain, ks_gain, kw_gain, wc, bd)


def _cmp_prep_kernel(cc_ref, kg_ref, bd_ref, kca_ref, vct_ref):
    cc = cc_ref[0]
    kc = _seg_rms(cc[:, :KV_WIDTH], bd_ref[...]) * kg_ref[...]
    for g, vt in enumerate(_heads_t(cc[:, KV_WIDTH:])):
        kca_ref[0, g] = _head_lo(kc, g).astype(BF16)
        vct_ref[0, g] = vt.astype(BF16)


def _cmp_prep(cc_perm, kc_gain, bd):
    b, n, _ = cc_perm.shape
    return pl.pallas_call(
        _cmp_prep_kernel,
        grid=(b,),
        in_specs=[pl.BlockSpec((1, n, 2 * KV_WIDTH), lambda bi: (bi, 0, 0)),
                  pl.BlockSpec((1, KV_WIDTH), lambda bi: (0, 0)),
                  pl.BlockSpec((MXU_DIM, MXU_DIM), lambda bi: (0, 0))],
        out_specs=[pl.BlockSpec((1, KV_HEADS, n, LANES), lambda bi: (bi, 0, 0, 0)),
                   pl.BlockSpec((1, KV_HEADS, HEAD_DIM, n), lambda bi: (bi, 0, 0, 0))],
        out_shape=[jax.ShapeDtypeStruct((b, KV_HEADS, n, LANES), BF16),
                   jax.ShapeDtypeStruct((b, KV_HEADS, HEAD_DIM, n), BF16)],
        compiler_params=_cparams(("parallel",)),
        name="cmp_prep",
    )(cc_perm, kc_gain, bd)


def _rank_select(score, n_keep):
    n, w = score.shape
    sub = lax.broadcasted_iota(jnp.int32, (8, w), 0)
    groups = [score[8 * k:8 * k + 8] for k in range(n // 8)]
    counts = [jnp.zeros((8, w), jnp.int32) for _ in groups]
    for i in range(n):
        si = jnp.broadcast_to(score[i:i + 1, :], (8, w))
        for k, blk in enumerate(groups):
            if 8 * k + 7 < i:
                inc = jnp.where(si > blk, 1, 0)
            elif 8 * k > i:
                inc = jnp.where(si >= blk, 1, 0)
            else:
                inc = jnp.where(sub > (i - 8 * k), jnp.where(si >= blk, 1, 0), jnp.where(si > blk, 1, 0))
            counts[k] = counts[k] + inc
    return jnp.concatenate(counts, axis=0) < n_keep


def _attn_window_kernel(qa_ref, kaw_ref, vtw_ref, o_ref, *w_refs, tq):
    i = pl.program_id(2)
    rows = Q_PER_KV * tq
    r_q = _lane_iota((1, rows)) % tq
    k_off = _row_iota((tq, 1))
    n_w = WINDOW // tq + 1
    c_win = jnp.maximum(i - (n_w - 1), 0)
    def run(chunk_mask):
        for gi, w_ref in enumerate(w_refs):
            keys = kaw_ref[0, gi, pl.ds(pl.multiple_of(c_win * tq, tq), n_w * tq), :]
            w_ref[...] = _dot_nt(keys, qa_ref[0, gi].reshape(rows, LANES))
        state = [None] * len(w_refs)
        for j in range(n_w):
            mask = chunk_mask(j)
            for gi, w_ref in enumerate(w_refs):
                s = w_ref[j * tq:(j + 1) * tq, :]
                if mask is not None:
                    s = jnp.where(mask, s, NEG)
                m = jnp.max(s, axis=0, keepdims=True)
                acc = jnp.dot(vtw_ref[0, gi, c_win + j], jnp.exp2(s - m).astype(BF16),
                              preferred_element_type=F32)
                if state[gi] is None:
                    state[gi] = (m, acc)
                else:
                    m_old, acc_old = state[gi]
                    m_new = jnp.maximum(m_old, m)
                    state[gi] = (m_new, acc_old * jnp.exp2(m_old - m_new) + acc * jnp.exp2(m - m_new))
        for gi, (_, acc) in enumerate(state):
            o_ref[0, gi, 0] = acc[:HEAD_DIM] * (1.0 / acc[HEAD_DIM:HEAD_DIM + 1])

    @pl.when(i >= n_w - 1)
    def _():
        run(lambda j: k_off > r_q if j == 0 else (k_off <= r_q if j == n_w - 1 else None))

    @pl.when(i < n_w - 1)
    def _():
        def general(j):
            age = (i - c_win - j) * tq + r_q - k_off
            return lax.bitcast_convert_type(age, jnp.uint32) < jnp.uint32(WINDOW)
        run(general)


def _attn_window(qa, kaw, vtw, b, t, tq):
    nt = t // tq
    gps = GROUPS_PER_STEP
    rows = Q_PER_KV * tq
    return pl.pallas_call(
        functools.partial(_attn_window_kernel, tq=tq),
        grid=(b, KV_HEADS // gps, nt),
        in_specs=[pl.BlockSpec((1, gps, Q_PER_KV, tq, LANES), lambda bi, g, i: (bi, g, 0, i, 0)),
                  pl.BlockSpec((1, gps, t, LANES), lambda bi, g, i: (bi, g, 0, 0)),
                  pl.BlockSpec((1, gps, nt, VT_ROWS, tq), lambda bi, g, i: (bi, g, 0, 0, 0))],
        out_specs=pl.BlockSpec((1, gps, 1, HEAD_DIM, rows), lambda bi, g, i: (bi, g, i, 0, 0)),
        out_shape=jax.ShapeDtypeStruct((b, KV_HEADS, nt, HEAD_DIM, rows), F32),
        scratch_shapes=[pltpu.VMEM((WINDOW + tq, rows), F32) for _ in range(gps)],
        compiler_params=_cparams(("parallel", "parallel", "arbitrary")),
        name="attn_window",
    )(qa, kaw, vtw)


def _attn_prompt_kernel(qa_ref, kca_ref, vct_ref, kas_ref, vts_ref, ow_ref, gl_ref, o_ref, sig_ref,
                        *s_refs, tq):
    i = pl.program_id(2)
    rows = Q_PER_KV * tq
    t0 = i * tq
    n_cmp = kca_ref.shape[2]
    n_selb = n_cmp // 2
    t_q = t0 + _lane_iota((1, rows)) % tq


    n_row = _row_iota((n_cmp, tq))
    cmp_id = 2 * (n_row % n_selb) + n_row // n_selb
    vis = (cmp_id + 1) * CMP_BLOCK - 1 <= t0 + _lane_iota((n_cmp, tq))
    vis = jnp.concatenate([vis] * Q_PER_KV, axis=1)
    j_row = _row_iota((n_selb, tq))
    cur = (t0 + _lane_iota((n_selb, tq))) // SEL_BLOCK
    cand = j_row < cur
    forced = (j_row == 0) | (j_row == cur - 1)

    def prepare(gi):
        qa = qa_ref[0, gi].reshape(rows, LANES)
        s = _dot_nt(kca_ref[0, gi], qa)
        s = jnp.where(vis, s, NEG)
        e = jnp.where(vis, jnp.exp2(s - jnp.max(s, axis=0, keepdims=True)), 0.0)
        p = e / jnp.maximum(jnp.sum(e, axis=0, keepdims=True), TINY)
        o_cmp = jnp.dot(vct_ref[0, gi], p.astype(BF16), preferred_element_type=F32)

        imp = jnp.zeros((n_selb, tq), F32)
        for r in range(Q_PER_KV):
            imp = imp + (p[:n_selb, r * tq:(r + 1) * tq] + p[n_selb:, r * tq:(r + 1) * tq])
        score = jnp.where(cand, jnp.where(forced, jnp.inf, imp), -jnp.inf)
        keep = (_rank_select(score, N_SEL - 1) & cand) | (j_row == cur)
        bias_t = jnp.where(keep, 0.0, SEL_BIAS)
        if n_selb < HEAD_DIM:
            bias_t = jnp.concatenate([bias_t, jnp.full((HEAD_DIM - n_selb, tq), SEL_BIAS, F32)], axis=0)
        bias = jnp.concatenate([jnp.zeros((HEAD_DIM, tq), F32), bias_t], axis=0).T
        bias4 = jnp.concatenate([bias] * Q_PER_KV, axis=0).astype(BF16)
        q_sel = jnp.where(_lane_iota((rows, LANES)) >= HEAD_DIM, bias4, qa)
        return q_sel, o_cmp

    def partial_softmax(s, vt_ref, gi, c0):
        m = jnp.max(s, axis=0, keepdims=True)
        pb = jnp.exp2(s - m).astype(BF16)
        acc = None
        for j in range(s.shape[0] // tq):
            part = jnp.dot(vt_ref[0, gi, c0 + j], pb[j * tq:(j + 1) * tq], preferred_element_type=F32)
            acc = part if acc is None else acc + part
        return m, acc

    def merge(a, b):
        m = jnp.maximum(a[0], b[0])
        return m, a[1] * jnp.exp2(a[0] - m) + b[1] * jnp.exp2(b[0] - m)

    def normalised(acc):
        return acc[:HEAD_DIM] * (1.0 / acc[HEAD_DIM:HEAD_DIM + 1])

    def k_rows(ref, gi, c0, n):
        return ref[0, gi, pl.ds(pl.multiple_of(c0 * tq, tq), n * tq), :]

    groups = [prepare(gi) for gi in range(GROUPS_PER_STEP)]

    n_full = i // SEL_CHUNKS
    causal_last = n_full * SEL_CHUNKS * tq + _row_iota((SEL_CHUNKS * tq, 1)) <= t_q

    def produce(gi, slab):
        s_refs[gi][...] = _dot_nt(k_rows(kas_ref, gi, SEL_CHUNKS * slab, SEL_CHUNKS), groups[gi][0])

    def consume(gi, slab, carry, last):
        s = s_refs[gi][...]
        if last:
            s = jnp.where(causal_last, s, NEG)
        return merge(carry, partial_softmax(s, vts_ref, gi, SEL_CHUNKS * slab))

    def sel_body(it, carries):
        c0, c1 = carries
        produce(1, it)
        c0 = consume(0, it, c0, False)
        produce(0, it + 1)
        c1 = consume(1, it, c1, False)
        return c0, c1

    init = (jnp.full((1, rows), NEG, F32), jnp.zeros((VT_ROWS, rows), F32))
    produce(0, 0)
    c0, c1 = lax.fori_loop(0, n_full, sel_body, (init, init))
    produce(1, n_full)
    carries = (consume(0, n_full, c0, True), consume(1, n_full, c1, True))

    sig_ref[...] = jax.nn.sigmoid(gl_ref[...]).T
    for gi, (q_sel, o_cmp) in enumerate(groups):
        o_win = ow_ref[0, gi, 0]
        o_sel = normalised(carries[gi][1])
        group = pl.program_id(1) * GROUPS_PER_STEP + gi

        def gate(branch):
            r0 = branch * N_HEADS + group * Q_PER_KV
            return jnp.concatenate([sig_ref[pl.ds(r0 + r, 1), :] for r in range(Q_PER_KV)], axis=1)

        o_t = gate(0) * o_cmp + gate(1) * o_sel + gate(2) * o_win
        for h in range(0, Q_PER_KV, 2):
            pair_t = jnp.concatenate([o_t[:, h * tq:(h + 1) * tq], o_t[:, (h + 1) * tq:(h + 2) * tq]], axis=0)
            lane0 = (gi * Q_PER_KV + h) * HEAD_DIM
            o_ref[:, lane0:lane0 + LANES] = pair_t.T


def _attn_prompt(z, qa, kca, vct, kas, vts, o_win_t, b, t, tq):
    nt = t // tq
    n_cmp = kca.shape[2]
    gps = GROUPS_PER_STEP
    rows = Q_PER_KV * tq
    qspec = pl.BlockSpec((1, gps, Q_PER_KV, tq, LANES), lambda bi, g, i: (bi, g, 0, i, 0))
    kspec = pl.BlockSpec((1, gps, t, LANES), lambda bi, g, i: (bi, g, 0, 0))
    vspec = pl.BlockSpec((1, gps, nt, VT_ROWS, tq), lambda bi, g, i: (bi, g, 0, 0, 0))
    gspec = pl.BlockSpec((tq, LANES), lambda bi, g, i: (bi * nt + i, C_GNSA // LANES))
    return pl.pallas_call(
        functools.partial(_attn_prompt_kernel, tq=tq),
        grid=(b, KV_HEADS // gps, nt),
        in_specs=[qspec,
                  pl.BlockSpec((1, gps, n_cmp, LANES), lambda bi, g, i: (bi, g, 0, 0)),
                  pl.BlockSpec((1, gps, HEAD_DIM, n_cmp), lambda bi, g, i: (bi, g, 0, 0)),
                  kspec, vspec,
                  pl.BlockSpec((1, gps, 1, HEAD_DIM, rows), lambda bi, g, i: (bi, g, i, 0, 0)), gspec],
        out_specs=pl.BlockSpec((tq, gps * Q_PER_KV * HEAD_DIM), lambda bi, g, i: (bi * nt + i, g)),
        out_shape=jax.ShapeDtypeStruct((b * t, ATT_WIDTH), F32),
        scratch_shapes=[pltpu.VMEM((LANES, tq), F32)]
        + [pltpu.VMEM((SEL_CHUNKS * tq, rows), F32) for _ in range(gps)],
        compiler_params=_cparams(("parallel", "parallel", "arbitrary")),
        name="attn_prompt",
    )(qa, kca, vct, kas, vts, o_win_t, z)


def _rglru_kernel(x_ref, hist_ref, h0_ref, cw_ref, cb_ref, wr_ref, br_ref, wi_ref, bi_ref, lam_ref,
                  h_ref, hl_ref, cs_ref, carry_ref, tail_ref, a_ref, u_ref, xe_ref, *, tt, nb):
    i = pl.program_id(1)

    @pl.when(i == 0)
    def _():
        carry_ref[...] = h0_ref[...]
        tail_ref[...] = hist_ref[...]

    cw = cw_ref[...]
    nl = -lam_ref[...]
    softplus = jnp.maximum(nl, 0.0) + jnp.log1p(jnp.exp(-jnp.abs(nl)))
    for k in range(nb):
        x = x_ref[k]
        xe_ref[k, 0:8, :] = tail_ref[k]
        xe_ref[k, 8:8 + tt, :] = x
        xc = cb_ref[...]
        for j in range(CONV_W):
            lo = 8 - (CONV_W - 1) + j
            xc = xc + (x if lo == 8 else xe_ref[k, lo:lo + tt, :]) * cw[j:j + 1]
        xb = xc.astype(BF16)

        def gates(w_ref, b_ref):
            parts = [jnp.dot(xb[:, c * MXU_DIM:(c + 1) * MXU_DIM], w_ref[c], preferred_element_type=F32)
                     for c in range(D_RNN // MXU_DIM)]
            z = jnp.concatenate(parts, axis=1) + b_ref[...]
            return 0.5 * jnp.tanh(0.5 * z) + 0.5

        r = gates(wr_ref, br_ref)
        ig = gates(wi_ref, bi_ref)
        log_a = -LRU_C * r * softplus
        a_ref[k] = jnp.exp(log_a)
        th = jnp.tanh(log_a)
        gap = -2.0 * th / (1.0 - th)
        root = jnp.where(gap > 0.0, gap * lax.rsqrt(gap), 0.0)
        u_ref[k] = root * (ig * xc)
        tail_ref[k] = x[tt - 8:tt]

    def step(t, hs):
        out = []
        for k in range(nb):
            h = a_ref[k, pl.ds(t, 1), :] * hs[k] + u_ref[k, pl.ds(t, 1), :]
            h_ref[k, pl.ds(t, 1), :] = h
            out.append(h)
        return tuple(out)

    hs = lax.fori_loop(0, tt, step, tuple(carry_ref[k] for k in range(nb)), unroll=8)
    for k in range(nb):
        carry_ref[k] = hs[k]

    @pl.when(i == pl.num_programs(1) - 1)
    def _():
        hl_ref[...] = carry_ref[...]
        cs_ref[...] = tail_ref[...]


def _rglru(z, b, t, hist8, h0, conv_w, conv_b, wr_bd, b_rg, wi_bd, b_ig, lam, tt, nb):
    nt = t // tt

    def const(shape):
        return pl.BlockSpec(shape, lambda gb, i: tuple(0 for _ in shape))

    def per_seq(rows):
        return pl.BlockSpec((nb, rows, D_RNN), lambda gb, i: (gb, 0, 0))

    row = lambda v: v.reshape(1, D_RNN)
    h, h_last, conv_tail = pl.pallas_call(
        functools.partial(_rglru_kernel, tt=tt, nb=nb),
        grid=(b // nb, nt),
        in_specs=[pl.BlockSpec((nb, tt, D_RNN), lambda gb, i: (gb, i, C_XRNN // D_RNN)),
                  per_seq(8), per_seq(1),
                  const((CONV_W, D_RNN)), const((1, D_RNN)),
                  const((D_RNN // MXU_DIM, MXU_DIM, MXU_DIM)), const((1, D_RNN)),
                  const((D_RNN // MXU_DIM, MXU_DIM, MXU_DIM)), const((1, D_RNN)),
                  const((1, D_RNN))],
        out_specs=[pl.BlockSpec((nb, tt, D_RNN), lambda gb, i: (gb, i, 0)), per_seq(1), per_seq(8)],
        out_shape=[jax.ShapeDtypeStruct((b, t, D_RNN), F32),
                   jax.ShapeDtypeStruct((b, 1, D_RNN), F32),
                   jax.ShapeDtypeStruct((b, 8, D_RNN), F32)],
        scratch_shapes=[pltpu.VMEM((nb, 1, D_RNN), F32), pltpu.VMEM((nb, 8, D_RNN), F32),
                        pltpu.VMEM((nb, tt, D_RNN), F32), pltpu.VMEM((nb, tt, D_RNN), F32),
                        pltpu.VMEM((nb, tt + 8, D_RNN), F32)],
        compiler_params=_cparams(("parallel", "arbitrary")),
        name="rglru",
    )(z.reshape(b, t, N_PACK), hist8, h0.reshape(b, 1, D_RNN), conv_w, row(conv_b),
      wr_bd, row(b_rg), wi_bd, row(b_ig), row(lam))
    return h.reshape(b * t, D_RNN), h_last, conv_tail


def _out_kernel(x_ref, oatt_ref, gatt_ref, h_ref, grnn_ref, gma_ref, gmr_ref, wa_ref, wr_ref, wo_ref, y_ref):
    a = (oatt_ref[...] * jax.nn.silu(gatt_ref[...])).astype(BF16)
    u_att = jnp.dot(a, wa_ref[...], preferred_element_type=F32)
    r = (h_ref[...] * jax.nn.silu(grnn_ref[...])).astype(BF16)
    u_rnn = jnp.dot(r, wr_ref[...], preferred_element_type=F32)
    m = jax.nn.sigmoid(gma_ref[...]) * u_att + jax.nn.sigmoid(gmr_ref[...]) * u_rnn
    y_ref[...] = x_ref[...] + jnp.dot(m.astype(BF16), wo_ref[...], preferred_element_type=F32)


def _out_proj(x2d, z, o_att, h_rnn, wa, wr, wo, tm):
    m = x2d.shape[0]

    def zspec(width, col):
        return pl.BlockSpec((tm, width), lambda i, c=col // width: (i, c))

    def wspec(shape):
        return pl.BlockSpec(shape, lambda i: (0, 0), pipeline_mode=pl.Buffered(1))

    rows = lambda width: pl.BlockSpec((tm, width), lambda i: (i, 0))
    return pl.pallas_call(
        _out_kernel,
        grid=(m // tm,),
        in_specs=[rows(D_MODEL), rows(ATT_WIDTH), zspec(ATT_WIDTH, C_GATT), rows(D_RNN), zspec(D_RNN, C_GRNN),
                  zspec(D_MODEL, C_GMATT), zspec(D_MODEL, C_GMRNN),
                  wspec((ATT_WIDTH, D_MODEL)), wspec((D_RNN, D_MODEL)), wspec((D_MODEL, D_MODEL))],
        out_specs=rows(D_MODEL),
        out_shape=jax.ShapeDtypeStruct((m, D_MODEL), F32),
        compiler_params=_cparams(("parallel",)),
        name="out_proj",
    )(x2d, o_att, z, h_rnn, z, z, z, wa, wr, wo)


def _prep_sample_kernel(zq_ref, zc_ref, zs_ref, zw_ref, qg_ref, ksg_ref, kwg_ref, bd_ref,
                        qn_ref, lc_ref, ls_ref, lw_ref):
    bd = bd_ref[...]
    qn_ref[...] = _seg_rms(zq_ref[...], bd) * qg_ref[...] * SCALE
    lc_ref[...] = zc_ref[...]
    for z_ref, g_ref, l_ref in ((zs_ref, ksg_ref, ls_ref), (zw_ref, kwg_ref, lw_ref)):
        z = z_ref[...]
        l_ref[:, :KV_WIDTH] = _seg_rms(z[:, :KV_WIDTH], bd) * g_ref[...]
        l_ref[:, KV_WIDTH:] = z[:, KV_WIDTH:]


def _prep_sample(z, q_gain, ks_gain, kw_gain, bd):
    m = z.shape[0]

    def zspec(width, col):
        return pl.BlockSpec((m, width), lambda i, c=col // width: (0, c))

    def const(shape):
        return pl.BlockSpec(shape, lambda i: tuple(0 for _ in shape))

    full = lambda width: pl.BlockSpec((m, width), lambda i: (0, 0))
    leaf = jax.ShapeDtypeStruct((m, 2 * KV_WIDTH), F32)
    return pl.pallas_call(
        _prep_sample_kernel,
        grid=(1,),
        in_specs=[zspec(ATT_WIDTH, C_Q), zspec(512, C_KV), zspec(512, C_KV + 512), zspec(512, C_KV + 1024),
                  const((1, ATT_WIDTH)), const((1, KV_WIDTH)), const((1, KV_WIDTH)), const((MXU_DIM, MXU_DIM))],
        out_specs=[full(ATT_WIDTH), full(512), full(512), full(512)],
        out_shape=[jax.ShapeDtypeStruct((m, ATT_WIDTH), F32), leaf, leaf, leaf],
        compiler_params=_cparams(("arbitrary",)),
        name="prep_sample",
    )(z, z, z, z, q_gain, ks_gain, kw_gain, bd)


def _page_specs(n):
    return [pl.BlockSpec((1, 2 * KV_WIDTH, PAGE_SIZE),
                         lambda bi, s, pt, k=k, n=n: (pt[bi, s * n + k], 0, 0)) for k in range(n)]


def _compress_pages_kernel(pt_ref, *refs):
    del pt_ref
    pages = refs[:CMP_PAGES]
    wt_ref, seg_ref, ok_ref, ov_ref = refs[CMP_PAGES:]
    wt = wt_ref[...]
    for tile in range(CMP_PAGES // CMP_TILE_PAGES):
        acc_k = jnp.zeros((KV_WIDTH, LANES), F32)
        acc_v = jnp.zeros((KV_WIDTH, LANES), F32)
        for pair in range(CMP_TILE_PAGES // 2):
            first = tile * CMP_TILE_PAGES + 2 * pair
            pa = pages[first][0] * wt
            pb = pages[first + 1][0] * wt
            seg = seg_ref[pair]
            lhs_k = jnp.concatenate([pa[:KV_WIDTH], pb[:KV_WIDTH]], axis=1).astype(BF16)
            lhs_v = jnp.concatenate([pa[KV_WIDTH:], pb[KV_WIDTH:]], axis=1).astype(BF16)
            acc_k = acc_k + jnp.dot(lhs_k, seg, preferred_element_type=F32)
            acc_v = acc_v + jnp.dot(lhs_v, seg, preferred_element_type=F32)
        ok_ref[0, :, tile * LANES:(tile + 1) * LANES] = acc_k
        ov_ref[0, :, tile * LANES:(tile + 1) * LANES] = acc_v


def _compress_pages(page_table, pool_t, wt, seg):
    b, n_pages = page_table.shape
    ns = n_pages // CMP_PAGES
    cols = CMP_PAGES // CMP_TILE_PAGES * LANES
    out_spec = pl.BlockSpec((1, KV_WIDTH, cols), lambda bi, s, pt: (bi, 0, s))
    grid_spec = pltpu.PrefetchScalarGridSpec(
        num_scalar_prefetch=1,
        grid=(b, ns),
        in_specs=_page_specs(CMP_PAGES) + [
            pl.BlockSpec((2 * KV_WIDTH, PAGE_SIZE), lambda bi, s, pt: (0, 0)),
            pl.BlockSpec((CMP_TILE_PAGES // 2, 2 * PAGE_SIZE, LANES), lambda bi, s, pt: (0, 0, 0))],
        out_specs=[out_spec, out_spec],
    )
    shape = jax.ShapeDtypeStruct((b, KV_WIDTH, ns * cols), F32)
    return pl.pallas_call(
        _compress_pages_kernel,
        grid_spec=grid_spec,
        out_shape=[shape, shape],
        compiler_params=_cparams(("parallel", "arbitrary")),
        name="compress_pages",
    )(page_table, *([pool_t] * CMP_PAGES), wt, seg)


def _softmax_lanes(parts, masks):
    parts = [jnp.where(mk, s, NEG) for s, mk in zip(parts, masks)]
    mx = functools.reduce(jnp.maximum, [jnp.max(s, axis=1, keepdims=True) for s in parts])
    es = [jnp.where(mk, jnp.exp(s - mx), 0.0) for s, mk in zip(parts, masks)]
    den = jnp.maximum(sum(jnp.sum(e, axis=1, keepdims=True) for e in es), TINY)
    return [e / den for e in es]


def _sample_cmp_win_kernel(qbd_ref, ck_ref, cv_ref, kg_ref, cwin_ref, wnew_ref,
                           ocmp_ref, owin_ref, bias_ref, wst_ref, *, past_len, t_new):
    qbd = qbd_ref[0]
    rows = qbd.shape[0]
    ck = ck_ref[0]
    n_cmp = ck.shape[1]
    parts = []
    for g in range(KV_HEADS):
        xs = ck[g * HEAD_DIM:(g + 1) * HEAD_DIM]
        parts.append(xs * lax.rsqrt(jnp.mean(xs * xs, axis=0, keepdims=True) + EPS))
    kc = (jnp.concatenate(parts, axis=0) * kg_ref[...]).astype(BF16)

    t_row = _row_iota((rows, n_cmp)) % t_new
    n_lane = _lane_iota((rows, n_cmp))
    cmp_id = LANES * (n_lane // LANES) + 2 * (n_lane % (LANES // 2)) + (n_lane % LANES) // (LANES // 2)
    vis = (cmp_id + 1) * CMP_BLOCK - 1 <= past_len + t_row
    (p,) = _softmax_lanes([jnp.dot(qbd, kc, preferred_element_type=F32)], [vis])
    ocmp_ref[0] = _dot_nt(p.astype(BF16), cv_ref[0].astype(BF16))

    gt = KV_HEADS * t_new
    psum = p[0:gt]
    for r in range(1, Q_PER_KV):
        psum = psum + p[r * gt:(r + 1) * gt]
    halves = []
    for c in range(n_cmp // LANES):
        tile = psum[:, c * LANES:(c + 1) * LANES]
        halves.append(tile + pltpu.roll(tile, LANES // 2, 1))
    if len(halves) == 1:
        imp = halves[0][:, :LANES // 2]
    else:
        low = _lane_iota((gt, LANES)) < LANES // 2
        imp = jnp.concatenate([jnp.where(low, halves[c], halves[c + 1]) for c in range(0, len(halves), 2)], axis=1)
    n_blk = n_cmp // 2
    j = _lane_iota((gt, n_blk))
    cur = (past_len + _row_iota((gt, n_blk)) % t_new) // SEL_BLOCK
    cand = j < cur
    forced = (j == 0) | (j == cur - 1)
    score = jnp.where(cand, jnp.where(forced, jnp.inf, imp), -jnp.inf)
    width = min(LANES, n_blk)
    tiles = [score[:, k * width:(k + 1) * width] for k in range(n_blk // width)]
    counts = [jnp.zeros((gt, width), jnp.int32) for _ in tiles]
    lane_w = _lane_iota((gt, width))
    for c in range(n_blk):
        col = score[:, c:c + 1]
        for k, tile in enumerate(tiles):
            if k < c // width:
                inc = jnp.where(col > tile, 1, 0)
            elif k > c // width:
                inc = jnp.where(col >= tile, 1, 0)
            else:
                inc = jnp.where(lane_w > c % width, jnp.where(col >= tile, 1, 0), jnp.where(col > tile, 1, 0))
            counts[k] = counts[k] + inc
    count = counts[0] if len(counts) == 1 else jnp.concatenate(counts, axis=1)
    keep = (count < N_SEL - 1) & cand
    bias_ref[0] = jnp.where(keep, 0.0, SEL_BIAS)

    cwin = cwin_ref[0]
    wb = cwin.shape[1]
    wnew = wnew_ref[0]
    wnew_p = jnp.concatenate([wnew, jnp.zeros((LANES - t_new, 2 * KV_WIDTH), F32)], axis=0)
    t_w = _row_iota((rows, wb)) % t_new
    idx = _lane_iota((rows, wb))
    ok_w = (idx <= wb + t_w) & (idx > wb + t_w - WINDOW)
    t_n = _row_iota((rows, LANES)) % t_new
    idx_n = wb + _lane_iota((rows, LANES))
    ok_n = (idx_n <= wb + t_n) & (idx_n > wb + t_n - WINDOW) & (_lane_iota((rows, LANES)) < t_new)
    p_w, p_n = _softmax_lanes([jnp.dot(qbd, cwin[:KV_WIDTH].astype(BF16), preferred_element_type=F32),
                               _dot_nt(qbd, wnew_p[:, :KV_WIDTH].astype(BF16))], [ok_w, ok_n])
    owin_ref[0] = (_dot_nt(p_w.astype(BF16), cwin[KV_WIDTH:].astype(BF16))
                   + jnp.dot(p_n.astype(BF16), wnew_p[:, KV_WIDTH:].astype(BF16), preferred_element_type=F32))

    keep_lanes = LANES - t_new
    new_t = jnp.concatenate([jnp.zeros((keep_lanes, 2 * KV_WIDTH), F32), wnew], axis=0).T
    rolled = [pltpu.roll(cwin[:, c * LANES:(c + 1) * LANES], keep_lanes, 1) for c in range(wb // LANES)]
    rolled.append(new_t)
    first = _lane_iota((2 * KV_WIDTH, LANES)) < keep_lanes
    for c in range(wb // LANES):
        wst_ref[0, :, c * LANES:(c + 1) * LANES] = jnp.where(first, rolled[c], rolled[c + 1])


def _sample_cmp_win(qbd, ck_t, cv_t, kc_gain_col, cwin_t, wnew, past_len):
    b, rows, _ = qbd.shape
    n_cmp = ck_t.shape[2]
    wb = cwin_t.shape[2]
    t_new = wnew.shape[1]
    gt = KV_HEADS * t_new

    def per_b(shape):
        return pl.BlockSpec((1,) + shape, lambda bi: (bi, 0, 0))

    return pl.pallas_call(
        functools.partial(_sample_cmp_win_kernel, past_len=past_len, t_new=t_new),
        grid=(b,),
        in_specs=[per_b((rows, MXU_DIM)), per_b((KV_WIDTH, n_cmp)), per_b((KV_WIDTH, n_cmp)),
                  pl.BlockSpec((KV_WIDTH, 1), lambda bi: (0, 0)),
                  per_b((2 * KV_WIDTH, wb)), per_b((t_new, 2 * KV_WIDTH))],
        out_specs=[per_b((rows, MXU_DIM)), per_b((rows, MXU_DIM)), per_b((gt, n_cmp // 2)), per_b((2 * KV_WIDTH, wb))],
        out_shape=[jax.ShapeDtypeStruct((b, rows, MXU_DIM), F32), jax.ShapeDtypeStruct((b, rows, MXU_DIM), F32),
                   jax.ShapeDtypeStruct((b, gt, n_cmp // 2), F32), jax.ShapeDtypeStruct((b, 2 * KV_WIDTH, wb), F32)],
        compiler_params=_cparams(("parallel",)),
        name="sample_cmp_win",
    )(qbd, ck_t, cv_t, kc_gain_col, cwin_t, wnew)


def _sample_sel_kernel(pt_ref, *refs, t_new):
    del pt_ref
    pages = refs[:SEL_PAGES]
    (qbd_ref, bsel_ref, onehot_ref, snew_ref, ocmp_ref, owin_ref, gl_ref,
     o_ref, m_ref, l_ref, acc_ref) = refs[SEL_PAGES:SEL_PAGES + 11]
    s_refs = refs[SEL_PAGES + 11:]
    s_idx = pl.program_id(1)
    qbd = qbd_ref[0]
    rows = qbd.shape[0]

    @pl.when(s_idx == 0)
    def _():
        m_ref[...] = jnp.full(m_ref.shape, NEG, F32)
        l_ref[...] = jnp.zeros(l_ref.shape, F32)
        acc_ref[...] = jnp.zeros(acc_ref.shape, F32)

    def partial_softmax(s, pv):
        m = jnp.max(s, axis=1, keepdims=True)
        pe = jnp.exp(s - m)
        return m, jnp.sum(pe, axis=1, keepdims=True), pv(pe.astype(BF16))

    def update(parts):
        m_old = m_ref[...]
        m_new = functools.reduce(jnp.maximum, [p[0] for p in parts], m_old)
        scale = jnp.exp(m_old - m_new)
        l = scale * l_ref[...]
        acc = scale * acc_ref[...]
        for m, l_part, acc_part in parts:
            c = jnp.exp(m - m_new)
            l = l + c * l_part
            acc = acc + c * acc_part
        m_ref[...] = m_new
        l_ref[...] = l
        acc_ref[...] = acc

    bsel = bsel_ref[0, 0]
    keys = SEL_SUB * PAGE_SIZE
    for g, s_ref in enumerate(s_refs):
        pgs = pages[g * SEL_SUB:(g + 1) * SEL_SUB]
        k_t = jnp.concatenate([pg[0, :KV_WIDTH, :].astype(BF16) for pg in pgs], axis=1)
        s_ref[...] = (jnp.dot(qbd, k_t, preferred_element_type=F32)
                      + jnp.dot(bsel, onehot_ref[:, g * keys:(g + 1) * keys], preferred_element_type=F32))
    parts = []
    for g, s_ref in enumerate(s_refs):
        pgs = pages[g * SEL_SUB:(g + 1) * SEL_SUB]
        v_t = jnp.concatenate([pg[0, KV_WIDTH:, :].astype(BF16) for pg in pgs], axis=1)
        parts.append(partial_softmax(s_ref[...], lambda pb, v_t=v_t: _dot_nt(pb, v_t)))
    update(parts)

    @pl.when(s_idx == pl.num_programs(1) - 1)
    def _():
        snew = jnp.concatenate([snew_ref[0], jnp.zeros((LANES - t_new, 2 * KV_WIDTH), F32)], axis=0)
        t_q = _row_iota((rows, LANES)) % t_new
        t_k = _lane_iota((rows, LANES))
        s = jnp.where((t_k <= t_q) & (t_k < t_new), _dot_nt(qbd, snew[:, :KV_WIDTH].astype(BF16)), NEG)
        v_new = snew[:, KV_WIDTH:].astype(BF16)
        update([partial_softmax(s, lambda pb: jnp.dot(pb, v_new, preferred_element_type=F32))])
        o_sel = acc_ref[...] / l_ref[...]
        sig = jax.nn.sigmoid(gl_ref[0])
        o_ref[0] = sig[:, 0:1] * ocmp_ref[0] + sig[:, 1:2] * o_sel + sig[:, 2:3] * owin_ref[0]


def _sample_sel(page_table, pool_t, qbd, bsel, onehot, snew, ocmp, owin, gl):
    b, n_pages = page_table.shape
    ns = n_pages // SEL_PAGES
    rows = qbd.shape[1]
    t_new = snew.shape[1]
    keys = SEL_PAGES * PAGE_SIZE

    def per_b(shape):
        return pl.BlockSpec((1,) + shape, lambda bi, s, pt: (bi,) + tuple(0 for _ in shape))

    grid_spec = pltpu.PrefetchScalarGridSpec(
        num_scalar_prefetch=1,
        grid=(b, ns),
        in_specs=_page_specs(SEL_PAGES) + [
            per_b((rows, MXU_DIM)),
            pl.BlockSpec((1, 1, rows, LANES), lambda bi, s, pt: (bi, s, 0, 0)),
            pl.BlockSpec((LANES, keys), lambda bi, s, pt: (0, 0)),
            per_b((t_new, 2 * KV_WIDTH)), per_b((rows, MXU_DIM)), per_b((rows, MXU_DIM)), per_b((rows, LANES))],
        out_specs=per_b((rows, MXU_DIM)),
        scratch_shapes=[pltpu.VMEM((rows, 1), F32), pltpu.VMEM((rows, 1), F32), pltpu.VMEM((rows, MXU_DIM), F32)]
        + [pltpu.VMEM((rows, SEL_SUB * PAGE_SIZE), F32) for _ in range(SEL_PAGES // SEL_SUB)],
    )
    return pl.pallas_call(
        functools.partial(_sample_sel_kernel, t_new=t_new),
        grid_spec=grid_spec,
        out_shape=jax.ShapeDtypeStruct((b, rows, MXU_DIM), F32),
        compiler_params=_cparams(("parallel", "arbitrary")),
        name="sample_sel",
    )(page_table, *([pool_t] * SEL_PAGES), qbd, bsel, onehot, snew, ocmp, owin, gl)


def _pack_weights(w_in, q_norm_g, k_norm_g, w_cmp, w_rg, w_ig, w_att_out, w_rnn_out, w_out):
    def block_diag(w):
        per = MXU_DIM // HEAD_DIM
        w4 = w.reshape(RNN_BLOCKS // per, per, HEAD_DIM, HEAD_DIM)
        eye = jnp.eye(per, dtype=w.dtype)
        return jnp.einsum('cpde,pq->cpdqe', w4, eye).reshape(RNN_BLOCKS // per, MXU_DIM, MXU_DIM).astype(BF16)

    wt = jnp.broadcast_to(w_cmp.transpose(1, 2, 0)[:, None], (2, KV_HEADS, HEAD_DIM, CMP_BLOCK))
    wt = jnp.tile(wt, (1, 1, 1, PAGE_SIZE // CMP_BLOCK)).reshape(2 * KV_WIDTH, PAGE_SIZE)
    lane = jnp.arange(2 * PAGE_SIZE)
    pair = jnp.arange(CMP_TILE_PAGES // 2)
    n_local = ((PAGE_SIZE // CMP_BLOCK) * (2 * pair[:, None] + lane[None, :] // PAGE_SIZE)
               + (lane[None, :] % PAGE_SIZE) // CMP_BLOCK)
    col = (n_local % 2) * (LANES // 2) + n_local // 2
    seg = (jnp.arange(LANES)[None, None, :] == col[:, :, None]).astype(BF16)

    return dict(
        w_t=w_in.T,
        q_gain=jnp.tile(q_norm_g, N_HEADS).reshape(1, ATT_WIDTH),
        kc_gain=jnp.tile(k_norm_g[0], KV_HEADS).reshape(1, KV_WIDTH),
        ks_gain=jnp.tile(k_norm_g[1], KV_HEADS).reshape(1, KV_WIDTH),
        kw_gain=jnp.tile(k_norm_g[2], KV_HEADS).reshape(1, KV_WIDTH),
        wc=jnp.broadcast_to(w_cmp[:, :, None, :], (CMP_BLOCK, 2, KV_HEADS, HEAD_DIM)).reshape(CMP_BLOCK, 2 * KV_WIDTH),
        wt=wt, seg=seg,
        wr_bd=block_diag(w_rg), wi_bd=block_diag(w_ig),
        wa=w_att_out.astype(BF16), wr=w_rnn_out.astype(BF16), wo=w_out.astype(BF16),
        bd=_block_diag_ones(),
    )


def _prompt_layer(x, pw, norm_g, conv_w, conv_b, b_rg, b_ig, lam):
    b, t, _ = x.shape
    tq = ATTN_TILE
    x2d = x.reshape(b * t, D_MODEL)
    z = _proj(x2d, norm_g, pw['w_t'], tm=min(1024, b * t))
    qa, leaf_c, leaf_s, leaf_w, cc, kas, vts, kaw, vtw = _prep_prompt(
        z, b, t, pw['q_gain'], pw['ks_gain'], pw['kw_gain'], pw['wc'], pw['bd'], te=256, tq=tq)
    n_cmp = t // CMP_BLOCK
    cc_perm = cc.reshape(b, n_cmp // 2, 2, 2 * KV_WIDTH).transpose(0, 2, 1, 3).reshape(b, n_cmp, 2 * KV_WIDTH)
    kca, vct = _cmp_prep(cc_perm, pw['kc_gain'], pw['bd'])
    o_win_t = _attn_window(qa, kaw, vtw, b, t, tq)
    o_att = _attn_prompt(z, qa, kca, vct, kas, vts, o_win_t, b, t, tq)
    h_rnn, h_last, conv_tail = _rglru(z, b, t, jnp.zeros((b, 8, D_RNN), F32), jnp.zeros((b, D_RNN), F32),
                                      conv_w, conv_b, pw['wr_bd'], b_rg, pw['wi_bd'], b_ig, lam, tt=256, nb=b)
    y = _out_proj(x2d, z, o_att, h_rnn, pw['wa'], pw['wr'], pw['wo'], tm=256)
    def kv(leaf_t):
        tokens = leaf_t.shape[2]
        return leaf_t.reshape(b, 2, KV_HEADS, HEAD_DIM, tokens).transpose(0, 4, 1, 2, 3)

    w_keep = min(WINDOW, t)
    return y.reshape(b, t, D_MODEL), (kv(leaf_c), kv(leaf_s), kv(leaf_w[:, :, t - w_keep:]),
                                      h_last.reshape(b, D_RNN), conv_tail[:, 8 - (CONV_W - 1):])


def _feature_major(cache):
    n, tokens = cache.shape[:2]
    return cache.transpose(0, 2, 3, 4, 1).reshape(n, 2 * KV_WIDTH, tokens)


def _sample_layer(x, cache_cmp, cache_sel, cache_win, state_h, state_conv, page_table, pw,
                  norm_g, conv_w, conv_b, b_rg, b_ig, lam):
    b, t, _ = x.shape
    n_pages = page_table.shape[1]
    past_len = n_pages * PAGE_SIZE
    rows = N_HEADS * t
    x2d = x.reshape(b * t, D_MODEL)
    z = _proj(x2d, norm_g, pw['w_t'], tm=b * t)
    qn, leaf_c, leaf_s, leaf_w = _prep_sample(z, pw['q_gain'], pw['ks_gain'], pw['kw_gain'], pw['bd'])

    q5 = qn.reshape(b, t, KV_HEADS, Q_PER_KV, HEAD_DIM).transpose(0, 3, 2, 1, 4)
    qbd = (q5[:, :, :, :, None, :] * jnp.eye(KV_HEADS, dtype=F32)[None, None, :, None, :, None])
    qbd = qbd.reshape(b, rows, KV_WIDTH).astype(BF16)

    ck_t, cv_t = _compress_pages(page_table, _feature_major(cache_cmp), pw['wt'], pw['seg'])
    wb = cache_win.shape[1]
    o_cmp, o_win, bias, win_state_t = _sample_cmp_win(
        qbd, ck_t, cv_t, pw['kc_gain'].reshape(KV_WIDTH, 1), _feature_major(cache_win),
        leaf_w.reshape(b, t, 2 * KV_WIDTH), past_len)
    win_state = win_state_t.reshape(b, 2, KV_HEADS, HEAD_DIM, wb).transpose(0, 4, 1, 2, 3)

    ns = n_pages // SEL_PAGES
    blocks_per_step = SEL_PAGES * PAGE_SIZE // SEL_BLOCK
    bsel = bias.reshape(b, KV_HEADS * t, ns, blocks_per_step).transpose(0, 2, 1, 3)
    bsel = jnp.tile(bsel, (1, 1, Q_PER_KV, 1))
    bsel = jnp.pad(bsel, ((0, 0), (0, 0), (0, 0), (0, LANES - blocks_per_step))).astype(BF16)
    key_block = jnp.arange(SEL_PAGES * PAGE_SIZE) // SEL_BLOCK
    onehot = (jnp.arange(LANES)[:, None] == key_block[None, :]).astype(BF16)
    gl = z[:, C_GNSA:C_GNSA + 3 * N_HEADS].reshape(b, t, 3, KV_HEADS, Q_PER_KV)
    gl = jnp.pad(gl.transpose(0, 4, 3, 1, 2).reshape(b, rows, 3), ((0, 0), (0, 0), (0, LANES - 3)))
    o_full = _sample_sel(page_table, _feature_major(cache_sel), qbd, bsel, onehot,
                         leaf_s.reshape(b, t, 2 * KV_WIDTH), o_cmp, o_win, gl)
    o6 = o_full.reshape(b, Q_PER_KV, KV_HEADS, t, KV_HEADS, HEAD_DIM)
    o_att = jnp.stack([o6[:, :, g, :, g, :] for g in range(KV_HEADS)], axis=2)
    o_att = o_att.transpose(0, 3, 2, 1, 4).reshape(b * t, ATT_WIDTH)

    hist8 = jnp.pad(state_conv, ((0, 0), (8 - (CONV_W - 1), 0), (0, 0)))
    h_rnn, h_last, conv_tail = _rglru(z, b, t, hist8, state_h, conv_w, conv_b,
                                      pw['wr_bd'], b_rg, pw['wi_bd'], b_ig, lam, tt=t,
                                      nb=8 if b % 8 == 0 else b)
    y = _out_proj(x2d, z, o_att, h_rnn, pw['wa'], pw['wr'], pw['wo'], tm=min(256, b * t))
    kv = lambda leaf: leaf.reshape(b, -1, 2, KV_HEADS, HEAD_DIM)
    return y.reshape(b, t, D_MODEL), (kv(leaf_c), kv(leaf_s), win_state,
                                      h_last.reshape(b, D_RNN), conv_tail[:, 8 - (CONV_W - 1):])


def kernel(x_prompt, x_sample, cache_cmp, cache_sel, cache_win, state_h, state_conv, page_table,
           norm_g, w_in, q_norm_g, k_norm_g, w_cmp, conv_w, conv_b, w_rg, b_rg, w_ig, b_ig,
           lru_lambda, w_att_out, w_rnn_out, w_out):
    yp, ys = x_prompt, x_sample
    outs_p, outs_s = [], []
    for l in range(w_in.shape[0]):
        pw = _pack_weights(w_in[l], q_norm_g[l], k_norm_g[l], w_cmp[l], w_rg[l], w_ig[l],
                           w_att_out[l], w_rnn_out[l], w_out[l])
        yp, st_p = _prompt_layer(yp, pw, norm_g[l], conv_w[l], conv_b[l], b_rg[l], b_ig[l], lru_lambda[l])
        ys, st_s = _sample_layer(ys, cache_cmp[l], cache_sel[l], cache_win[l], state_h[l], state_conv[l],
                                 page_table, pw, norm_g[l], conv_w[l], conv_b[l], b_rg[l], b_ig[l], lru_lambda[l])
        outs_p.append(st_p)
        outs_s.append(st_s)
    cmp_p, sel_p, win_p, h_p, conv_p = [jnp.stack(a) for a in zip(*outs_p)]
    cmp_s, sel_s, win_s, h_s, conv_s = [jnp.stack(a) for a in zip(*outs_s)]
    return (yp, ys, cmp_p, sel_p, win_p, h_p, conv_p, cmp_s, sel_s, win_s, h_s, conv_s)
```
